```python
import jax, jax.numpy as jnp
from jax import lax
import numpy as np

D_MODEL = 1024
BATCH = 4
SEQ = 4096
DEPTH = 1

HEAD_DIM = 64
A_Q_HEADS = 8
A_KV_HEADS = 2
B_GROUPS = ((128, 1), (512, 4), (2048, 16))
B_HEADS_PER_GROUP = 4
B_HEADS = B_HEADS_PER_GROUP * len(B_GROUPS)
A_Q_W = A_Q_HEADS * HEAD_DIM
A_KV_W = A_KV_HEADS * HEAD_DIM
B_W = B_HEADS * HEAD_DIM
B_OUT_W = B_HEADS_PER_GROUP * HEAD_DIM
QKV_COLS = A_Q_W + 2 * A_KV_W + 3 * B_W
QKV_SPLITS = [A_Q_W, A_Q_W + A_KV_W, A_Q_W + 2 * A_KV_W,
              A_Q_W + 2 * A_KV_W + B_W, A_Q_W + 2 * A_KV_W + 2 * B_W]
D_FF = -(-(8 * D_MODEL) // (3 * 256)) * 256
GRID_W = 64
Q_BLOCK = 128
AXIAL_THETA = 10000.0
PARTIAL_THETA = 500000.0
PARTIAL_ROT_DIM = HEAD_DIM // 4
EPS = 1e-6
NEG_INF = -1e30

kernel_name = "hybrid_gqa_axial_dilated_swa_adaln_block"


def _rms(x, g):
    xf = x.astype(jnp.float32)
    y = xf * lax.rsqrt(jnp.mean(xf * xf, axis=-1, keepdims=True) + EPS)
    return (y * g.astype(jnp.float32)).astype(x.dtype)


def _rope_angles(pos, dim, theta):
    inv = theta ** (-jnp.arange(0, dim, 2, dtype=jnp.float32) / dim)
    ang = pos.astype(jnp.float32)[:, None] * inv[None, :]
    return jnp.cos(ang), jnp.sin(ang)


def _rotate(x, cos, sin):
    half = x.shape[-1] // 2
    x1 = x[..., :half].astype(jnp.float32)
    x2 = x[..., half:].astype(jnp.float32)
    c = cos[None, :, None, :]
    s = sin[None, :, None, :]
    return jnp.concatenate([x1 * c - x2 * s, x2 * c + x1 * s], axis=-1).astype(x.dtype)


def _axial_rope(x, row, col):
    half = HEAD_DIM // 2
    cr, sr = _rope_angles(row, half, AXIAL_THETA)
    cc, sc = _rope_angles(col, half, AXIAL_THETA)
    return jnp.concatenate([_rotate(x[..., :half], cr, sr),
                            _rotate(x[..., half:], cc, sc)], axis=-1)


def _partial_rope(x, pos):
    cr, sr = _rope_angles(pos, PARTIAL_ROT_DIM, PARTIAL_THETA)
    return jnp.concatenate([_rotate(x[..., :PARTIAL_ROT_DIM], cr, sr),
                            x[..., PARTIAL_ROT_DIM:]], axis=-1)


def _global_gqa(q, k, v):
    b, s, _, d = q.shape
    g = A_Q_HEADS // A_KV_HEADS
    nq = s // Q_BLOCK
    qb = q.reshape(b, nq, Q_BLOCK, A_KV_HEADS, g, d).transpose(1, 0, 2, 3, 4, 5)
    scale = d ** -0.5

    def attend(qblk):
        sc = jnp.einsum('bqhgd,bkhd->bhgqk', qblk, k).astype(jnp.float32) * scale
        p = jax.nn.softmax(sc, axis=-1)
        return jnp.einsum('bhgqk,bkhd->bqhgd', p.astype(v.dtype), v)

    o = lax.map(attend, qb)
    return o.transpose(1, 0, 2, 3, 4, 5).reshape(b, s, A_Q_HEADS * d)


def _banded(q, k, v, radius):
    L, d = q.shape[-2], q.shape[-1]
    blk = radius
    nb = -(-L // blk)
    lp = nb * blk
    lead = q.shape[:-2]
    pad = [(0, 0)] * len(lead)
    qp = jnp.pad(q, pad + [(0, lp - L), (0, 0)]).reshape(*lead, nb, blk, d)

    def windows(t):
        tp = jnp.pad(t, pad + [(blk, lp - L + blk), (0, 0)]).reshape(*lead, nb + 2, blk, d)
        return jnp.concatenate([tp[..., :-2, :, :], tp[..., 1:-1, :, :], tp[..., 2:, :, :]], axis=-2)

    kw, vw = windows(k), windows(v)
    s = jnp.einsum('...nqd,...nkd->...nqk', qp, kw).astype(jnp.float32) * (d ** -0.5)
    qpos = jnp.arange(nb)[:, None] * blk + jnp.arange(blk)[None, :]
    kpos = (jnp.arange(nb)[:, None] - 1) * blk + jnp.arange(3 * blk)[None, :]
    kp = kpos[:, None, :]
    valid = (jnp.abs(qpos[:, :, None] - kp) <= radius) & (kp >= 0) & (kp < L)
    s = jnp.where(valid, s, NEG_INF)
    lse = jax.nn.logsumexp(s, axis=-1, keepdims=True)
    p = jnp.exp(s - lse)
    o = jnp.einsum('...nqk,...nkd->...nqd', p.astype(v.dtype), vw)
    o = o.reshape(*lead, lp, d)[..., :L, :]
    return o, lse.reshape(*lead, lp)[..., :L]


def _dilated_swa(q, k, v):
    b, s, _, d = q.shape
    h = B_HEADS_PER_GROUP
    outs, lses = [], []
    for gi, (window, dil) in enumerate(B_GROUPS):
        lo, hi = gi * h, (gi + 1) * h
        L = s // dil
        qs = q[:, :, lo:hi].reshape(b, L, dil, h, d).transpose(0, 2, 3, 1, 4)
        ks = k[:, :, lo:hi].reshape(b, L, dil, h, d).transpose(0, 2, 3, 1, 4)
        vs = v[:, :, lo:hi].reshape(b, L, dil, h, d).transpose(0, 2, 3, 1, 4)
        o, lse = _banded(qs, ks, vs, window // 2 // dil)
        outs.append(o.transpose(0, 3, 1, 2, 4).reshape(b, s, h, d))
        lses.append(lse.transpose(0, 3, 1, 2).reshape(b, s, h))
    o_all = jnp.stack(outs, axis=0)
    lse_all = jnp.stack(lses, axis=0)
    w = jax.nn.softmax(lse_all, axis=0)
    out = jnp.sum(w[..., None].astype(o_all.dtype) * o_all, axis=0)
    return out.reshape(b, s, h * d)


def setup_inputs(seed: int = 0) -> dict:
    key = jax.random.key(seed)
    ks = jax.random.split(key, 17)

    def nrm(k, shape, fan_in, s=1.0):
        return jax.random.normal(k, shape, jnp.float32) * (s * fan_in ** -0.5)

    def gain(k, shape):
        return 1.0 + 0.1 * jax.random.normal(k, shape, jnp.float32)

    return {
        "x": jax.random.normal(ks[0], (BATCH, SEQ, D_MODEL), jnp.float32),
        "c": jax.random.normal(ks[1], (BATCH, D_MODEL), jnp.float32),
        "w_ada": nrm(ks[2], (DEPTH, D_MODEL, 6 * D_MODEL), D_MODEL, 0.5),
        "b_ada": 0.02 * jax.random.normal(ks[3], (DEPTH, 6 * D_MODEL), jnp.float32),
        "norm1_g": gain(ks[4], (DEPTH, D_MODEL)),
        "w_qkv": nrm(ks[5], (DEPTH, D_MODEL, QKV_COLS), D_MODEL),
        "q_norm_a": gain(ks[6], (DEPTH, HEAD_DIM)),
        "k_norm_a": gain(ks[7], (DEPTH, HEAD_DIM)),
        "w_proj_a": nrm(ks[8], (DEPTH, A_Q_W, D_MODEL), A_Q_W),
        "w_proj_b": nrm(ks[9], (DEPTH, B_OUT_W, D_MODEL), B_OUT_W),
        "w_gate": nrm(ks[10], (DEPTH, D_MODEL, 2 * D_MODEL), D_MODEL),
        "b_gate": 0.1 * jax.random.normal(ks[11], (DEPTH, 2 * D_MODEL), jnp.float32),
        "w_o": nrm(ks[12], (DEPTH, D_MODEL, D_MODEL), D_MODEL),
        "norm2_g": gain(ks[13], (DEPTH, D_MODEL)),
        "w_ffn_in": nrm(ks[14], (DEPTH, D_MODEL, 2 * D_FF), D_MODEL),
        "w_ffn_out": nrm(ks[15], (DEPTH, D_FF, D_MODEL), D_FF),
        "final_norm_g": gain(ks[16], (D_MODEL,)),
    }


def reference(x, c, w_ada, b_ada, norm1_g, w_qkv, q_norm_a, k_norm_a, w_proj_a, w_proj_b,
              w_gate, b_gate, w_o, norm2_g, w_ffn_in, w_ffn_out, final_norm_g):
    b, s, _ = x.shape
    rows = s // GRID_W
    row = jnp.repeat(jnp.arange(rows, dtype=jnp.int32), GRID_W)
    col = jnp.tile(jnp.arange(GRID_W, dtype=jnp.int32), rows)
    pos = jnp.arange(s, dtype=jnp.int32)
    cond = jax.nn.silu(c)
    for l in range(DEPTH):
        mod = cond @ w_ada[l] + b_ada[l]
        sh1, sc1, g1, sh2, sc2, g2 = jnp.split(mod, 6, axis=-1)
        u = _rms(x, norm1_g[l]) * (1.0 + sc1[:, None, :]) + sh1[:, None, :]
        qkv = u @ w_qkv[l]
        qa, ka, va, qb, kb, vb = jnp.split(qkv, QKV_SPLITS, axis=-1)
        qa = qa.reshape(b, s, A_Q_HEADS, HEAD_DIM)
        ka = ka.reshape(b, s, A_KV_HEADS, HEAD_DIM)
        va = va.reshape(b, s, A_KV_HEADS, HEAD_DIM)
        qa = _axial_rope(_rms(qa, q_norm_a[l]), row, col)
        ka = _axial_rope(_rms(ka, k_norm_a[l]), row, col)
        ya = _global_gqa(qa, ka, va) @ w_proj_a[l]
        qb = _partial_rope(qb.reshape(b, s, B_HEADS, HEAD_DIM), pos)
        kb = _partial_rope(kb.reshape(b, s, B_HEADS, HEAD_DIM), pos)
        vb = vb.reshape(b, s, B_HEADS, HEAD_DIM)
        yb = _dilated_swa(qb, kb, vb) @ w_proj_b[l]
        gates = jax.nn.sigmoid(u @ w_gate[l] + b_gate[l])
        ga, gb = jnp.split(gates, 2, axis=-1)
        mix = (ga * ya + gb * yb) @ w_o[l]
        x = x + g1[:, None, :] * mix
        u2 = _rms(x, norm2_g[l]) * (1.0 + sc2[:, None, :]) + sh2[:, None, :]
        hg, hu = jnp.split(u2 @ w_ffn_in[l], 2, axis=-1)
        x = x + g2[:, None, :] * ((jax.nn.silu(hg) * hu) @ w_ffn_out[l])
    return _rms(x, final_norm_g)
```

```python
import functools

import jax
import jax.numpy as jnp
from jax import lax
from jax.experimental import pallas as pl
from jax.experimental.pallas import tpu as pltpu

HEAD_DIM = 64
A_Q_HEADS = 8
A_KV_HEADS = 2
A_GROUP = A_Q_HEADS // A_KV_HEADS
B_GROUPS = ((128, 1), (512, 4), (2048, 16))
B_HEADS_PER_GROUP = 4
B_HEADS = B_HEADS_PER_GROUP * len(B_GROUPS)
A_Q_W = A_Q_HEADS * HEAD_DIM
A_KV_W = A_KV_HEADS * HEAD_DIM
B_W = B_HEADS * HEAD_DIM
B_OUT_W = B_HEADS_PER_GROUP * HEAD_DIM
GRID_W = 64
AXIAL_THETA = 10000.0
PARTIAL_THETA = 500000.0
PARTIAL_ROT_DIM = HEAD_DIM // 4
EPS = 1e-6
NEG_INF = -1e30
SCALE = HEAD_DIM ** -0.5

LANES = 128
VMEM_LIMIT = 56 * 1024 * 1024

F32 = jnp.float32
BF16 = jnp.bfloat16

TM_QKV = 512
TQ = 128
TK = 256
TM_MIX = 512
TM_FFN = 512
BLK = 128
WIN = 256


def _params(*sem):
    return pltpu.CompilerParams(dimension_semantics=sem, vmem_limit_bytes=VMEM_LIMIT)


def _mod_kernel(c_ref, w_ref, b_ref, o_ref):
    c = c_ref[...]
    cond = c * jax.nn.sigmoid(c)
    c_hi = cond.astype(BF16)
    c_lo = (cond - c_hi.astype(F32)).astype(BF16)
    w = w_ref[...]
    w_hi = w.astype(BF16)
    w_lo = (w - w_hi.astype(F32)).astype(BF16)
    acc = jnp.dot(c_hi, w_hi, preferred_element_type=F32)
    acc += jnp.dot(c_hi, w_lo, preferred_element_type=F32)
    acc += jnp.dot(c_lo, w_hi, preferred_element_type=F32)
    o_ref[...] = acc + b_ref[...]


def _modulation(c_pad, w_ada, b_ada):
    rows, d = c_pad.shape
    n = w_ada.shape[1]
    tn = 1024
    return pl.pallas_call(
        _mod_kernel,
        grid=(n // tn,),
        in_specs=[pl.BlockSpec((rows, d), lambda j: (0, 0)),
                  pl.BlockSpec((d, tn), lambda j: (0, j)),
                  pl.BlockSpec((1, tn), lambda j: (0, j))],
        out_specs=pl.BlockSpec((rows, tn), lambda j: (0, j)),
        out_shape=jax.ShapeDtypeStruct((rows, n), F32),
        compiler_params=_params("parallel"),
        name="mod",
    )(c_pad, w_ada, b_ada.reshape(1, n))


def _modulated_norm(x, gain, shift, scale):
    ms = jnp.mean(x * x, axis=-1, keepdims=True)
    y = (x * lax.rsqrt(ms + EPS)) * gain
    return (y * (1.0 + scale) + shift).astype(BF16)


def _rope(y, cos, sin_lo, sin_hi, shift):
    return (y * cos + pltpu.roll(y, LANES - shift, 1) * sin_lo
            + pltpu.roll(y, shift, 1) * sin_hi)


def _qkv_kernel(x_ref, mod_ref, g1_ref, w_ref, qg_ref, kg_ref, bd_ref,
                ac_ref, asl_ref, ash_ref, pc_ref, psl_ref, psh_ref,
                qat_ref, ka_ref, vat_ref, qb_ref, kb_ref, vb_ref):
    u = _modulated_norm(x_ref[...], g1_ref[...], mod_ref[0:1, :], mod_ref[1:2, :])

    def proj(lo, width):
        return jnp.dot(u, w_ref[:, lo:lo + width], preferred_element_type=F32)

    def head_norm_rope(chunk, gain):
        ss = jnp.dot((chunk * chunk).astype(BF16), bd_ref[...], preferred_element_type=F32)
        y = (chunk * lax.rsqrt(ss * (1.0 / HEAD_DIM) + EPS)) * gain
        return _rope(y, ac_ref[...], asl_ref[...], ash_ref[...], HEAD_DIM // 4)

    qa = proj(0, A_Q_W)
    qa = jnp.concatenate(
        [head_norm_rope(qa[:, i * LANES:(i + 1) * LANES], qg_ref[...]) for i in range(A_Q_W // LANES)],
        axis=1)
    qat = qa.T.astype(BF16)
    for i in range(qat_ref.shape[0]):
        qat_ref[i] = qat[:, i * TQ:(i + 1) * TQ]

    ka = proj(A_Q_W, A_KV_W)
    ka_ref[...] = head_norm_rope(ka, kg_ref[...]).astype(BF16)

    vat = proj(A_Q_W + A_KV_W, A_KV_W).T.astype(BF16)
    for i in range(vat_ref.shape[0]):
        vat_ref[i] = vat[:, i * TK:(i + 1) * TK]

    base = A_Q_W + 2 * A_KV_W
    for out_ref, lo in ((qb_ref, base), (kb_ref, base + B_W)):
        for i in range(B_W // LANES):
            chunk = proj(lo + i * LANES, LANES)
            out_ref[:, i * LANES:(i + 1) * LANES] = _rope(
                chunk, pc_ref[...], psl_ref[...], psh_ref[...], PARTIAL_ROT_DIM // 2).astype(BF16)
    vb_ref[...] = proj(base + 2 * B_W, B_W).astype(BF16)


def _qkv_project(x, mod3, g1, w_qkv, qg, kg, bd, tabs):
    b, s, d = x.shape
    tm = TM_QKV
    ncols = w_qkv.shape[1]
    row_tab = pl.BlockSpec((tm, LANES), lambda i, bb: (i, 0))
    const2 = lambda shape: pl.BlockSpec(shape, lambda i, bb: (0, 0))
    tok = lambda w: pl.BlockSpec((None, tm, w), lambda i, bb: (bb, i, 0))
    out_shapes = (
        jax.ShapeDtypeStruct((b, s // TQ, A_Q_W, TQ), BF16),
        jax.ShapeDtypeStruct((b, s, A_KV_W), BF16),
        jax.ShapeDtypeStruct((b, s // TK, A_KV_W, TK), BF16),
        jax.ShapeDtypeStruct((b, s, B_W), BF16),
        jax.ShapeDtypeStruct((b, s, B_W), BF16),
        jax.ShapeDtypeStruct((b, s, B_W), BF16),
    )
    out_specs = (
        pl.BlockSpec((None, tm // TQ, A_Q_W, TQ), lambda i, bb: (bb, i, 0, 0)),
        tok(A_KV_W),
        pl.BlockSpec((None, tm // TK, A_KV_W, TK), lambda i, bb: (bb, i, 0, 0)),
        tok(B_W), tok(B_W), tok(B_W),
    )
    return pl.pallas_call(
        _qkv_kernel,
        grid=(s // tm, b),
        in_specs=[tok(d),
                  pl.BlockSpec((None, 6, d), lambda i, bb: (bb, 0, 0)),
                  const2((1, d)),
                  const2((d, ncols)),
                  const2((1, LANES)), const2((1, LANES)), const2((LANES, LANES)),
                  row_tab, row_tab, row_tab, row_tab, row_tab, row_tab],
        out_specs=out_specs,
        out_shape=out_shapes,
        compiler_params=_params("parallel", "parallel"),
        name="qkv",
    )(x, mod3, g1, w_qkv, qg, kg, bd, *tabs)


def _gqa_kernel(qt_ref, k_ref, vt_ref, o_ref, qx_ref):
    nk = vt_ref.shape[0]
    ncol = A_GROUP * TQ
    for hk in range(A_KV_HEADS):
        qx_ref[...] = jnp.zeros(qx_ref.shape, BF16)
        for g in range(A_GROUP):
            h = hk * A_GROUP + g
            qx_ref[hk * HEAD_DIM:(hk + 1) * HEAD_DIM, g * TQ:(g + 1) * TQ] = (
                qt_ref[h * HEAD_DIM:(h + 1) * HEAD_DIM, :])
        qx = qx_ref[...]

        def body(j, carry):
            m, l, acc = carry
            k = k_ref[pl.ds(pl.multiple_of(j * TK, TK), TK), :]
            st = jnp.dot(k, qx, preferred_element_type=F32)
            m_new = jnp.maximum(m, jnp.max(st, axis=0, keepdims=True))
            p = jnp.exp(st - m_new)
            alpha = jnp.exp(m - m_new)
            l = alpha * l + jnp.sum(p, axis=0, keepdims=True)
            vt = vt_ref[j, hk * HEAD_DIM:(hk + 1) * HEAD_DIM, :]
            acc = alpha * acc + jnp.dot(vt, p.astype(BF16), preferred_element_type=F32)
            return m_new, l, acc

        m0 = jnp.full((1, ncol), NEG_INF, F32)
        l0 = jnp.zeros((1, ncol), F32)
        a0 = jnp.zeros((HEAD_DIM, ncol), F32)
        _, l, acc = lax.fori_loop(0, nk, body, (m0, l0, a0))
        out_t = acc / l
        stacked = jnp.concatenate([out_t[:, g * TQ:(g + 1) * TQ] for g in range(A_GROUP)], axis=0)
        o_ref[:, hk * A_GROUP * HEAD_DIM:(hk + 1) * A_GROUP * HEAD_DIM] = stacked.T.astype(BF16)


def _global_attention(qat, ka, vat):
    b, nq = qat.shape[0], qat.shape[1]
    s = ka.shape[1]
    nk = vat.shape[1]
    return pl.pallas_call(
        _gqa_kernel,
        grid=(b, nq),
        in_specs=[pl.BlockSpec((None, None, A_Q_W, TQ), lambda bb, i: (bb, i, 0, 0)),
                  pl.BlockSpec((None, s, A_KV_W), lambda bb, i: (bb, 0, 0)),
                  pl.BlockSpec((None, nk, A_KV_W, TK), lambda bb, i: (bb, 0, 0, 0))],
        out_specs=pl.BlockSpec((None, TQ, A_Q_W), lambda bb, i: (bb, i, 0)),
        out_shape=jax.ShapeDtypeStruct((b, s, A_Q_W), BF16),
        scratch_shapes=[pltpu.VMEM((A_KV_W, A_GROUP * TQ), BF16)],
        compiler_params=_params("parallel", "parallel"),
        name="gqa",
    )(qat, ka, vat)


def _swa_kernel(q_ref, k_ref, v_ref, o_ref, lse_ref, *, radius):
    seq = q_ref.shape[0]
    lane = lax.broadcasted_iota(jnp.int32, (BLK, LANES), 1)
    first_head = lane < HEAD_DIM
    rel = (lax.broadcasted_iota(jnp.int32, (BLK, WIN), 0)
           - lax.broadcasted_iota(jnp.int32, (BLK, WIN), 1))

    def block(i, _):
        i0 = pl.multiple_of(i * BLK, BLK)
        ws = pl.multiple_of(jnp.clip(i0 - radius, 0, seq - WIN), radius)
        valid = jnp.abs(rel + (i0 - ws)) <= radius
        for c in range(B_OUT_W // LANES):
            cols = slice(c * LANES, (c + 1) * LANES)
            q = q_ref[pl.ds(i0, BLK), cols]
            k = k_ref[pl.ds(ws, WIN), cols]
            v = v_ref[pl.ds(ws, WIN), cols]
            outs, lses = [], []
            for hh in range(2):
                sel = first_head if hh == 0 else jnp.logical_not(first_head)
                qm = jnp.where(sel, q, jnp.zeros_like(q))
                sc = lax.dot_general(qm, k, (((1,), (1,)), ((), ())), preferred_element_type=F32)
                sc = jnp.where(valid, sc, NEG_INF)
                m = jnp.max(sc, axis=-1, keepdims=True)
                p = jnp.exp(sc - m)
                l = jnp.sum(p, axis=-1, keepdims=True)
                outs.append(jnp.dot(p.astype(BF16), v, preferred_element_type=F32) / l)
                lses.append(m + jnp.log(l))
            o_ref[pl.ds(i0, BLK), cols] = jnp.where(first_head, outs[0], outs[1]).astype(BF16)
            lse_ref[pl.ds(i0, BLK), cols] = jnp.where(first_head, lses[0], lses[1])
        return 0

    lax.fori_loop(0, seq // BLK, block, 0)


def _banded_attention(qb, kb, vb, group):
    window, dil = B_GROUPS[group]
    b, s, _ = qb.shape
    seq = s // dil
    radius = window // 2 // dil
    view = lambda t: t.reshape(b, seq, dil * B_W)
    nblk = B_W // B_OUT_W
    in_spec = pl.BlockSpec((None, seq, B_OUT_W), lambda bb, r: (bb, 0, r * nblk + group))
    out_spec = pl.BlockSpec((None, seq, B_OUT_W), lambda bb, r: (bb, 0, r))
    o, lse = pl.pallas_call(
        functools.partial(_swa_kernel, radius=radius),
        grid=(b, dil),
        in_specs=[in_spec, in_spec, in_spec],
        out_specs=(out_spec, out_spec),
        out_shape=(jax.ShapeDtypeStruct((b, seq, dil * B_OUT_W), BF16),
                   jax.ShapeDtypeStruct((b, seq, dil * B_OUT_W), F32)),
        compiler_params=_params("parallel", "parallel"),
        name=f"swa{group}",
    )(view(qb), view(kb), view(vb))
    return o.reshape(b, s, B_OUT_W), lse.reshape(b, s, B_OUT_W)


def _mix_kernel(x_ref, mod_ref, g1_ref, attn_ref, o0_ref, o1_ref, o2_ref, l0_ref, l1_ref, l2_ref,
                wg_ref, bg_ref, wpa_ref, wpb_ref, wo_ref, out_ref):
    x = x_ref[...]
    d = x.shape[-1]
    u = _modulated_norm(x, g1_ref[...], mod_ref[0:1, :], mod_ref[1:2, :])
    gates = jax.nn.sigmoid(jnp.dot(u, wg_ref[...], preferred_element_type=F32) + bg_ref[...])
    ya = jnp.dot(attn_ref[...], wpa_ref[...], preferred_element_type=F32)
    lses = [l0_ref[...], l1_ref[...], l2_ref[...]]
    top = jnp.maximum(jnp.maximum(lses[0], lses[1]), lses[2])
    es = [jnp.exp(l - top) for l in lses]
    den = es[0] + es[1] + es[2]
    comb = sum((e / den) * o[...].astype(F32) for e, o in zip(es, (o0_ref, o1_ref, o2_ref)))
    yb = jnp.dot(comb.astype(BF16), wpb_ref[...], preferred_element_type=F32)
    merged = (gates[:, :d] * ya + gates[:, d:] * yb).astype(BF16)
    mix = jnp.dot(merged, wo_ref[...], preferred_element_type=F32)
    out_ref[...] = x + mod_ref[2:3, :] * mix


def _mix(x, mod3, g1, attn, os_, lses, wg, bg, wpa, wpb, wo):
    b, s, d = x.shape
    tm = TM_MIX
    tok = lambda w: pl.BlockSpec((None, tm, w), lambda bb, i: (bb, i, 0))
    const2 = lambda shape: pl.BlockSpec(shape, lambda bb, i: (0, 0))
    return pl.pallas_call(
        _mix_kernel,
        grid=(b, s // tm),
        in_specs=[tok(d),
                  pl.BlockSpec((None, 6, d), lambda bb, i: (bb, 0, 0)),
                  const2((1, d)),
                  tok(A_Q_W),
                  tok(B_OUT_W), tok(B_OUT_W), tok(B_OUT_W),
                  tok(B_OUT_W), tok(B_OUT_W), tok(B_OUT_W),
                  const2(wg.shape), const2(bg.shape), const2(wpa.shape), const2(wpb.shape),
                  const2(wo.shape)],
        out_specs=tok(d),
        out_shape=jax.ShapeDtypeStruct((b, s, d), F32),
        compiler_params=_params("parallel", "parallel"),
        name="mix",
    )(x, mod3, g1, attn, *os_, *lses, wg, bg, wpa, wpb, wo)


def _ffn_kernel(x_ref, mod_ref, g2_ref, win_ref, wout_ref, fg_ref, out_ref, *, final, chunk):
    x = x_ref[...]
    d_ff = wout_ref.shape[0]
    u = _modulated_norm(x, g2_ref[...], mod_ref[3:4, :], mod_ref[4:5, :])
    acc = jnp.zeros(x.shape, F32)
    for lo in range(0, d_ff, chunk):
        hg = jnp.dot(u, win_ref[:, lo:lo + chunk], preferred_element_type=F32)
        hu = jnp.dot(u, win_ref[:, d_ff + lo:d_ff + lo + chunk], preferred_element_type=F32)
        act = ((hg * jax.nn.sigmoid(hg)) * hu).astype(BF16)
        acc += jnp.dot(act, wout_ref[lo:lo + chunk, :], preferred_element_type=F32)
    y = x + mod_ref[5:6, :] * acc
    if final:
        ms = jnp.mean(y * y, axis=-1, keepdims=True)
        y = (y * lax.rsqrt(ms + EPS)) * fg_ref[...]
    out_ref[...] = y


def _ffn(x, mod3, g2, win, wout, fg, final):
    b, s, d = x.shape
    tm = TM_FFN
    d_ff = wout.shape[0]
    chunk = d_ff // 2 if (d_ff // 2) % LANES == 0 else d_ff
    tok = pl.BlockSpec((None, tm, d), lambda bb, i: (bb, i, 0))
    resident = lambda shape: pl.BlockSpec(shape, lambda bb, i: (0, 0), pipeline_mode=pl.Buffered(1))
    return pl.pallas_call(
        functools.partial(_ffn_kernel, final=final, chunk=chunk),
        grid=(b, s // tm),
        in_specs=[tok,
                  pl.BlockSpec((None, 6, d), lambda bb, i: (bb, 0, 0)),
                  pl.BlockSpec((1, d), lambda bb, i: (0, 0)),
                  resident(win.shape), resident(wout.shape),
                  pl.BlockSpec((1, d), lambda bb, i: (0, 0))],
        out_specs=tok,
        out_shape=jax.ShapeDtypeStruct((b, s, d), F32),
        compiler_params=_params("parallel", "parallel"),
        name="ffn",
    )(x, mod3, g2, win, wout, fg)


def _angles(pos, dim, theta):
    inv = theta ** (-jnp.arange(0, dim, 2, dtype=F32) / dim)
    return pos.astype(F32)[:, None] * inv[None, :]


def _rope_tables(s):
    t = jnp.arange(s, dtype=jnp.int32)
    half = HEAD_DIM // 2
    ar = _angles(t // GRID_W, half, AXIAL_THETA)
    ac = _angles(t % GRID_W, half, AXIAL_THETA)
    zr = jnp.zeros_like(ar)
    a_cos = jnp.concatenate([jnp.cos(ar), jnp.cos(ar), jnp.cos(ac), jnp.cos(ac)], axis=1)
    a_lo = jnp.concatenate([-jnp.sin(ar), zr, -jnp.sin(ac), zr], axis=1)
    a_hi = jnp.concatenate([zr, jnp.sin(ar), zr, jnp.sin(ac)], axis=1)
    ap = _angles(t, PARTIAL_ROT_DIM, PARTIAL_THETA)
    rest = HEAD_DIM - PARTIAL_ROT_DIM
    zp = jnp.zeros_like(ap)
    p_cos = jnp.concatenate([jnp.cos(ap), jnp.cos(ap), jnp.ones((s, rest), F32)], axis=1)
    p_lo = jnp.concatenate([-jnp.sin(ap), zp, jnp.zeros((s, rest), F32)], axis=1)
    p_hi = jnp.concatenate([zp, jnp.sin(ap), jnp.zeros((s, rest), F32)], axis=1)
    two = lambda a: jnp.tile(a, (1, LANES // HEAD_DIM))
    return tuple(two(a) for a in (a_cos, a_lo, a_hi, p_cos, p_lo, p_hi))


def kernel(x, c, w_ada, b_ada, norm1_g, w_qkv, q_norm_a, k_norm_a, w_proj_a, w_proj_b, w_gate,
           b_gate, w_o, norm2_g, w_ffn_in, w_ffn_out, final_norm_g):
    b, s, d = x.shape
    depth = w_ada.shape[0]
    assert s % TM_QKV == 0 and s % (B_GROUPS[-1][1] * WIN) == 0 and TM_QKV % TK == 0
    tabs = _rope_tables(s)
    heads_per_vreg = LANES // HEAD_DIM
    lane_head = jnp.arange(LANES) // HEAD_DIM
    bd = (lane_head[:, None] == lane_head[None, :]).astype(BF16)
    b_q_lo = A_Q_W + 2 * A_KV_W
    col_scale = jnp.ones((w_qkv.shape[-1],), F32).at[b_q_lo:b_q_lo + B_W].set(SCALE)
    c_pad = jnp.pad(c, ((0, 16 - b), (0, 0)))
    for l in range(depth):
        mod3 = _modulation(c_pad, w_ada[l], b_ada[l])[:b].reshape(b, 6, d)
        qg = jnp.tile(q_norm_a[l] * SCALE, heads_per_vreg).reshape(1, LANES)
        kg = jnp.tile(k_norm_a[l], heads_per_vreg).reshape(1, LANES)
        g1 = norm1_g[l].reshape(1, d)
        wq = (w_qkv[l] * col_scale[None, :]).astype(BF16)
        qat, ka, vat, qb, kb, vb = _qkv_project(x, mod3, g1, wq, qg, kg, bd, tabs)
        attn = _global_attention(qat, ka, vat)
        branch = [_banded_attention(qb, kb, vb, g) for g in range(len(B_GROUPS))]
        x = _mix(x, mod3, g1, attn, [o for o, _ in branch], [e for _, e in branch],
                 w_gate[l].astype(BF16), b_gate[l].reshape(1, -1), w_proj_a[l].astype(BF16),
                 w_proj_b[l].astype(BF16), w_o[l].astype(BF16))
        x = _ffn(x, mod3, norm2_g[l].reshape(1, d), w_ffn_in[l].astype(BF16),
                 w_ffn_out[l].astype(BF16), final_norm_g.reshape(1, d), final=(l == depth - 1))
    return x
```

```python
import functools

import jax
import jax.numpy as jnp
from jax import lax
from jax.experimental import pallas as pl
from jax.experimental.pallas import tpu as pltpu

HEAD_DIM = 64
A_Q_HEADS = 8
A_KV_HEADS = 2
A_GROUP = A_Q_HEADS // A_KV_HEADS
B_GROUPS = ((128, 1), (512, 4), (2048, 16))
B_HEADS_PER_GROUP = 4
B_HEADS = B_HEADS_PER_GROUP * len(B_GROUPS)
A_Q_W = A_Q_HEADS * HEAD_DIM
A_KV_W = A_KV_HEADS * HEAD_DIM
B_W = B_HEADS * HEAD_DIM
B_OUT_W = B_HEADS_PER_GROUP * HEAD_DIM
GRID_W = 64
AXIAL_THETA = 10000.0
PARTIAL_THETA = 500000.0
PARTIAL_ROT_DIM = HEAD_DIM // 4
EPS = 1e-6
NEG_INF = -1e30
SCALE = HEAD_DIM ** -0.5
LOG2E = 1.4426950408889634

LANES = 128
VMEM_LIMIT = 56 * 1024 * 1024

F32 = jnp.float32
BF16 = jnp.bfloat16

TM_QKV = 512
TQ = 128
TK = 1024
TM_MIX = 512
TM_FFN = 512
BLK = 128
WIN = 256
VT_ROWS = HEAD_DIM + 16
GQA_UNROLL = 2
GQA_TILES = 4


def _params(*sem):
    return pltpu.CompilerParams(dimension_semantics=sem, vmem_limit_bytes=VMEM_LIMIT)


def _mod_kernel(c_ref, w_ref, b_ref, o_ref):
    c = c_ref[...]
    cond = c * jax.nn.sigmoid(c)
    c_hi = cond.astype(BF16)
    c_lo = (cond - c_hi.astype(F32)).astype(BF16)
    w = w_ref[...]
    w_hi = w.astype(BF16)
    w_lo = (w - w_hi.astype(F32)).astype(BF16)
    acc = jnp.dot(c_hi, w_hi, preferred_element_type=F32)
    acc += jnp.dot(c_hi, w_lo, preferred_element_type=F32)
    acc += jnp.dot(c_lo, w_hi, preferred_element_type=F32)
    o_ref[...] = acc + b_ref[...]


def _modulation(c_pad, w_ada, b_ada):
    rows, d = c_pad.shape
    n = w_ada.shape[1]
    tn = 1024
    return pl.pallas_call(
        _mod_kernel,
        grid=(n // tn,),
        in_specs=[pl.BlockSpec((rows, d), lambda j: (0, 0)),
                  pl.BlockSpec((d, tn), lambda j: (0, j)),
                  pl.BlockSpec((1, tn), lambda j: (0, j))],
        out_specs=pl.BlockSpec((rows, tn), lambda j: (0, j)),
        out_shape=jax.ShapeDtypeStruct((rows, n), F32),
        compiler_params=_params("parallel"),
        name="mod",
    )(c_pad, w_ada, b_ada.reshape(1, n))


def _modulated_norm(x, gain, shift, scale):
    ms = jnp.mean(x * x, axis=-1, keepdims=True)
    y = (x * lax.rsqrt(ms + EPS)) * gain
    return (y * (1.0 + scale) + shift).astype(BF16)


def _rope(y, cos, sin_lo, sin_hi, shift):
    return (y * cos + pltpu.roll(y, LANES - shift, 1) * sin_lo
            + pltpu.roll(y, shift, 1) * sin_hi)


def _qkv_kernel(x_ref, mod_ref, g1_ref, w_ref, qg_ref, kg_ref, bd_ref,
                ac_ref, asl_ref, ash_ref, pc_ref, psl_ref, psh_ref,
                qat_ref, ka_ref, vat_ref, qb_ref, kb_ref, vb_ref):
    u = _modulated_norm(x_ref[...], g1_ref[...], mod_ref[0:1, :], mod_ref[1:2, :])

    def proj(lo, width):
        return jnp.dot(u, w_ref[:, lo:lo + width], preferred_element_type=F32)

    def head_norm_rope(chunk, gain):
        ss = jnp.dot((chunk * chunk).astype(BF16), bd_ref[...], preferred_element_type=F32)
        y = (chunk * lax.rsqrt(ss * (1.0 / HEAD_DIM) + EPS)) * gain
        return _rope(y, ac_ref[...], asl_ref[...], ash_ref[...], HEAD_DIM // 4)

    qa = proj(0, A_Q_W)
    qa = jnp.concatenate(
        [head_norm_rope(qa[:, i * LANES:(i + 1) * LANES], qg_ref[...]) for i in range(A_Q_W // LANES)],
        axis=1)
    qat = qa.T.astype(BF16)
    for i in range(qat_ref.shape[0]):
        qat_ref[i] = qat[:, i * TQ:(i + 1) * TQ]

    ka = proj(A_Q_W, A_KV_W)
    ka_ref[...] = head_norm_rope(ka, kg_ref[...]).astype(BF16)

    vat = proj(A_Q_W + A_KV_W, A_KV_W).T.astype(BF16)
    pad_row = lax.broadcasted_iota(jnp.int32, (VT_ROWS - HEAD_DIM, vat.shape[1]), 0)
    ones_row = jnp.where(pad_row == 0, 1.0, 0.0).astype(BF16)
    for hk in range(A_KV_HEADS):
        vat_ref[hk, 0:HEAD_DIM, :] = vat[hk * HEAD_DIM:(hk + 1) * HEAD_DIM, :]
        vat_ref[hk, HEAD_DIM:VT_ROWS, :] = ones_row

    base = A_Q_W + 2 * A_KV_W
    for out_ref, lo in ((qb_ref, base), (kb_ref, base + B_W)):
        for i in range(B_W // LANES):
            chunk = proj(lo + i * LANES, LANES)
            out_ref[:, i * LANES:(i + 1) * LANES] = _rope(
                chunk, pc_ref[...], psl_ref[...], psh_ref[...], PARTIAL_ROT_DIM // 2).astype(BF16)
    vb_ref[...] = proj(base + 2 * B_W, B_W).astype(BF16)


def _qkv_project(x, mod3, g1, w_qkv, qg, kg, bd, tabs):
    b, s, d = x.shape
    tm = TM_QKV
    ncols = w_qkv.shape[1]
    row_tab = pl.BlockSpec((tm, LANES), lambda i, bb: (i, 0))
    const2 = lambda shape: pl.BlockSpec(shape, lambda i, bb: (0, 0))
    tok = lambda w: pl.BlockSpec((None, tm, w), lambda i, bb: (bb, i, 0))
    out_shapes = (
        jax.ShapeDtypeStruct((b, s // TQ, A_Q_W, TQ), BF16),
        jax.ShapeDtypeStruct((b, s, A_KV_W), BF16),
        jax.ShapeDtypeStruct((b, s // TK, A_KV_HEADS, VT_ROWS, TK), BF16),
        jax.ShapeDtypeStruct((b, s, B_W), BF16),
        jax.ShapeDtypeStruct((b, s, B_W), BF16),
        jax.ShapeDtypeStruct((b, s, B_W), BF16),
    )
    out_specs = (
        pl.BlockSpec((None, tm // TQ, A_Q_W, TQ), lambda i, bb: (bb, i, 0, 0)),
        tok(A_KV_W),
        pl.BlockSpec((None, None, A_KV_HEADS, VT_ROWS, tm),
                     lambda i, bb: (bb, i // (TK // tm), 0, 0, i % (TK // tm))),
        tok(B_W), tok(B_W), tok(B_W),
    )
    return pl.pallas_call(
        _qkv_kernel,
        grid=(s // tm, b),
        in_specs=[tok(d),
                  pl.BlockSpec((None, 6, d), lambda i, bb: (bb, 0, 0)),
                  const2((1, d)),
                  const2((d, ncols)),
                  const2((1, LANES)), const2((1, LANES)), const2((LANES, LANES)),
                  row_tab, row_tab, row_tab, row_tab, row_tab, row_tab],
        out_specs=out_specs,
        out_shape=out_shapes,
        compiler_params=_params("parallel", "parallel"),
        name="qkv",
    )(x, mod3, g1, w_qkv, qg, kg, bd, *tabs)


def _gqa_kernel(qt_ref, k_ref, vt_ref, o_ref, qx_ref, s_ref):
    nk = vt_ref.shape[0]
    ncol = A_GROUP * TQ
    units = [(t, hk) for t in range(GQA_TILES) for hk in range(A_KV_HEADS)]

    qx_ref[...] = jnp.zeros(qx_ref.shape, BF16)
    for u, (t, hk) in enumerate(units):
        for g in range(A_GROUP):
            h = hk * A_GROUP + g
            qx_ref[u, hk * HEAD_DIM:(hk + 1) * HEAD_DIM, g * TQ:(g + 1) * TQ] = (
                qt_ref[t, h * HEAD_DIM:(h + 1) * HEAD_DIM, :])

    def phase(score_unit, value_unit, m_prev):
        qx = None if score_unit is None else qx_ref[score_unit]

        def body(j, carry):
            mrun, acc = carry
            rows = pl.ds(pl.multiple_of(j * TK, TK), TK)
            if score_unit is not None:
                st = jnp.dot(k_ref[rows, :], qx, preferred_element_type=F32)
                s_ref[score_unit % 2, rows, :] = st
                mrun = jnp.maximum(mrun, jnp.max(st.reshape(TK // 8, 8, ncol), axis=0))
            if value_unit is not None:
                p = jnp.exp2((s_ref[value_unit % 2, rows, :] - m_prev).astype(BF16))
                acc = acc + jnp.dot(vt_ref[j, units[value_unit][1]], p, preferred_element_type=F32)
            return mrun, acc

        init = (jnp.full((8, ncol), NEG_INF, F32), jnp.zeros((VT_ROWS, ncol), F32))
        mrun, acc = lax.fori_loop(0, nk, body, init, unroll=GQA_UNROLL)
        return jnp.max(mrun, axis=0, keepdims=True), acc

    m = None
    for idx in range(len(units) + 1):
        score_unit = idx if idx < len(units) else None
        value_unit = idx - 1 if idx > 0 else None
        m, acc = phase(score_unit, value_unit, m)
        if value_unit is not None:
            t, hk = units[value_unit]
            out_t = acc[0:HEAD_DIM, :] / acc[HEAD_DIM:HEAD_DIM + 1, :]
            stacked = jnp.concatenate([out_t[:, g * TQ:(g + 1) * TQ] for g in range(A_GROUP)], axis=0)
            o_ref[t * TQ:(t + 1) * TQ, hk * A_GROUP * HEAD_DIM:(hk + 1) * A_GROUP * HEAD_DIM] = (
                stacked.T.astype(BF16))


def _global_attention(qat, ka, vat):
    b, nq = qat.shape[0], qat.shape[1]
    s = ka.shape[1]
    nk = vat.shape[1]
    ncol = A_GROUP * TQ
    return pl.pallas_call(
        _gqa_kernel,
        grid=(b, nq // GQA_TILES),
        in_specs=[pl.BlockSpec((None, GQA_TILES, A_Q_W, TQ), lambda bb, i: (bb, i, 0, 0)),
                  pl.BlockSpec((None, s, A_KV_W), lambda bb, i: (bb, 0, 0)),
                  pl.BlockSpec((None, nk, A_KV_HEADS, VT_ROWS, TK), lambda bb, i: (bb, 0, 0, 0, 0))],
        out_specs=pl.BlockSpec((None, GQA_TILES * TQ, A_Q_W), lambda bb, i: (bb, i, 0)),
        out_shape=jax.ShapeDtypeStruct((b, s, A_Q_W), BF16),
        scratch_shapes=[pltpu.VMEM((GQA_TILES * A_KV_HEADS, A_KV_W, ncol), BF16),
                        pltpu.VMEM((2, s, ncol), F32)],
        compiler_params=_params("parallel", "parallel"),
        name="gqa",
    )(qat, ka, vat)


def _swa_kernel(q_ref, k_ref, v_ref, o_ref, lse_ref, *, radius):
    seq = q_ref.shape[0]
    lane = lax.broadcasted_iota(jnp.int32, (BLK, LANES), 1)
    first_head = lane < HEAD_DIM
    rel = (lax.broadcasted_iota(jnp.int32, (BLK, WIN), 0)
           - lax.broadcasted_iota(jnp.int32, (BLK, WIN), 1))

    def block(i, _):
        i0 = pl.multiple_of(i * BLK, BLK)
        ws = pl.multiple_of(jnp.clip(i0 - radius, 0, seq - WIN), radius)
        valid = jnp.abs(rel + (i0 - ws)) <= radius
        for c in range(B_OUT_W // LANES):
            cols = slice(c * LANES, (c + 1) * LANES)
            q = q_ref[pl.ds(i0, BLK), cols]
            k = k_ref[pl.ds(ws, WIN), cols]
            v = v_ref[pl.ds(ws, WIN), cols]
            outs, lses = [], []
            for hh in range(2):
                sel = first_head if hh == 0 else jnp.logical_not(first_head)
                qm = jnp.where(sel, q, jnp.zeros_like(q))
                sc = lax.dot_general(qm, k, (((1,), (1,)), ((), ())), preferred_element_type=F32)
                sc = jnp.where(valid, sc, NEG_INF)
                m = jnp.max(sc, axis=-1, keepdims=True)
                p = jnp.exp(sc - m)
                l = jnp.sum(p, axis=-1, keepdims=True)
                outs.append(jnp.dot(p.astype(BF16), v, preferred_element_type=F32) / l)
                lses.append(m + jnp.log(l))
            o_ref[pl.ds(i0, BLK), cols] = jnp.where(first_head, outs[0], outs[1]).astype(BF16)
            lse_ref[pl.ds(i0, BLK), cols] = jnp.where(first_head, lses[0], lses[1])
        return 0

    lax.fori_loop(0, seq // BLK, block, 0)


def _banded_attention(qb, kb, vb, group):
    window, dil = B_GROUPS[group]
    b, s, _ = qb.shape
    seq = s // dil
    radius = window // 2 // dil
    view = lambda t: t.reshape(b, seq, dil * B_W)
    nblk = B_W // B_OUT_W
    in_spec = pl.BlockSpec((None, seq, B_OUT_W), lambda bb, r: (bb, 0, r * nblk + group))
    out_spec = pl.BlockSpec((None, seq, B_OUT_W), lambda bb, r: (bb, 0, r))
    o, lse = pl.pallas_call(
        functools.partial(_swa_kernel, radius=radius),
        grid=(b, dil),
        in_specs=[in_spec, in_spec, in_spec],
        out_specs=(out_spec, out_spec),
        out_shape=(jax.ShapeDtypeStruct((b, seq, dil * B_OUT_W), BF16),
                   jax.ShapeDtypeStruct((b, seq, dil * B_OUT_W), F32)),
        compiler_params=_params("parallel", "parallel"),
        name=f"swa{group}",
    )(view(qb), view(kb), view(vb))
    return o.reshape(b, s, B_OUT_W), lse.reshape(b, s, B_OUT_W)


def _mix_kernel(x_ref, mod_ref, g1_ref, attn_ref, o0_ref, o1_ref, o2_ref, l0_ref, l1_ref, l2_ref,
                wg_ref, bg_ref, wpa_ref, wpb_ref, wo_ref, out_ref):
    x = x_ref[...]
    d = x.shape[-1]
    u = _modulated_norm(x, g1_ref[...], mod_ref[0:1, :], mod_ref[1:2, :])
    gates = jax.nn.sigmoid(jnp.dot(u, wg_ref[...], preferred_element_type=F32) + bg_ref[...])
    ya = jnp.dot(attn_ref[...], wpa_ref[...], preferred_element_type=F32)
    lses = [l0_ref[...], l1_ref[...], l2_ref[...]]
    top = jnp.maximum(jnp.maximum(lses[0], lses[1]), lses[2])
    es = [jnp.exp(l - top) for l in lses]
    den = es[0] + es[1] + es[2]
    comb = sum((e / den) * o[...].astype(F32) for e, o in zip(es, (o0_ref, o1_ref, o2_ref)))
    yb = jnp.dot(comb.astype(BF16), wpb_ref[...], preferred_element_type=F32)
    merged = (gates[:, :d] * ya + gates[:, d:] * yb).astype(BF16)
    mix = jnp.dot(merged, wo_ref[...], preferred_element_type=F32)
    out_ref[...] = x + mod_ref[2:3, :] * mix


def _mix(x, mod3, g1, attn, os_, lses, wg, bg, wpa, wpb, wo):
    b, s, d = x.shape
    tm = TM_MIX
    tok = lambda w: pl.BlockSpec((None, tm, w), lambda bb, i: (bb, i, 0))
    const2 = lambda shape: pl.BlockSpec(shape, lambda bb, i: (0, 0))
    return pl.pallas_call(
        _mix_kernel,
        grid=(b, s // tm),
        in_specs=[tok(d),
                  pl.BlockSpec((None, 6, d), lambda bb, i: (bb, 0, 0)),
                  const2((1, d)),
                  tok(A_Q_W),
                  tok(B_OUT_W), tok(B_OUT_W), tok(B_OUT_W),
                  tok(B_OUT_W), tok(B_OUT_W), tok(B_OUT_W),
                  const2(wg.shape), const2(bg.shape), const2(wpa.shape), const2(wpb.shape),
                  const2(wo.shape)],
        out_specs=tok(d),
        out_shape=jax.ShapeDtypeStruct((b, s, d), F32),
        compiler_params=_params("parallel", "parallel"),
        name="mix",
    )(x, mod3, g1, attn, *os_, *lses, wg, bg, wpa, wpb, wo)


def _ffn_kernel(x_ref, mod_ref, g2_ref, win_ref, wout_ref, fg_ref, out_ref, *, final, chunk):
    x = x_ref[...]
    d_ff = wout_ref.shape[0]
    u = _modulated_norm(x, g2_ref[...], mod_ref[3:4, :], mod_ref[4:5, :])
    acc = jnp.zeros(x.shape, F32)
    for lo in range(0, d_ff, chunk):
        hg = jnp.dot(u, win_ref[:, lo:lo + chunk], preferred_element_type=F32)
        hu = jnp.dot(u, win_ref[:, d_ff + lo:d_ff + lo + chunk], preferred_element_type=F32)
        act = ((hg * jax.nn.sigmoid(hg)) * hu).astype(BF16)
        acc += jnp.dot(act, wout_ref[lo:lo + chunk, :], preferred_element_type=F32)
    y = x + mod_ref[5:6, :] * acc
    if final:
        ms = jnp.mean(y * y, axis=-1, keepdims=True)
        y = (y * lax.rsqrt(ms + EPS)) * fg_ref[...]
    out_ref[...] = y


def _ffn(x, mod3, g2, win, wout, fg, final):
    b, s, d = x.shape
    tm = TM_FFN
    d_ff = wout.shape[0]
    chunk = d_ff // 2 if (d_ff // 2) % LANES == 0 else d_ff
    tok = pl.BlockSpec((None, tm, d), lambda bb, i: (bb, i, 0))
    resident = lambda shape: pl.BlockSpec(shape, lambda bb, i: (0, 0), pipeline_mode=pl.Buffered(1))
    return pl.pallas_call(
        functools.partial(_ffn_kernel, final=final, chunk=chunk),
        grid=(b, s // tm),
        in_specs=[tok,
                  pl.BlockSpec((None, 6, d), lambda bb, i: (bb, 0, 0)),
                  pl.BlockSpec((1, d), lambda bb, i: (0, 0)),
                  resident(win.shape), resident(wout.shape),
                  pl.BlockSpec((1, d), lambda bb, i: (0, 0))],
        out_specs=tok,
        out_shape=jax.ShapeDtypeStruct((b, s, d), F32),
        compiler_params=_params("parallel", "parallel"),
        name="ffn",
    )(x, mod3, g2, win, wout, fg)


def _angles(pos, dim, theta):
    inv = theta ** (-jnp.arange(0, dim, 2, dtype=F32) / dim)
    return pos.astype(F32)[:, None] * inv[None, :]


def _rope_tables(s):
    t = jnp.arange(s, dtype=jnp.int32)
    half = HEAD_DIM // 2
    ar = _angles(t // GRID_W, half, AXIAL_THETA)
    ac = _angles(t % GRID_W, half, AXIAL_THETA)
    zr = jnp.zeros_like(ar)
    a_cos = jnp.concatenate([jnp.cos(ar), jnp.cos(ar), jnp.cos(ac), jnp.cos(ac)], axis=1)
    a_lo = jnp.concatenate([-jnp.sin(ar), zr, -jnp.sin(ac), zr], axis=1)
    a_hi = jnp.concatenate([zr, jnp.sin(ar), zr, jnp.sin(ac)], axis=1)
    ap = _angles(t, PARTIAL_ROT_DIM, PARTIAL_THETA)
    rest = HEAD_DIM - PARTIAL_ROT_DIM
    zp = jnp.zeros_like(ap)
    p_cos = jnp.concatenate([jnp.cos(ap), jnp.cos(ap), jnp.ones((s, rest), F32)], axis=1)
    p_lo = jnp.concatenate([-jnp.sin(ap), zp, jnp.zeros((s, rest), F32)], axis=1)
    p_hi = jnp.concatenate([zp, jnp.sin(ap), jnp.zeros((s, rest), F32)], axis=1)
    two = lambda a: jnp.tile(a, (1, LANES // HEAD_DIM))
    return tuple(two(a) for a in (a_cos, a_lo, a_hi, p_cos, p_lo, p_hi))


def kernel(x, c, w_ada, b_ada, norm1_g, w_qkv, q_norm_a, k_norm_a, w_proj_a, w_proj_b, w_gate,
           b_gate, w_o, norm2_g, w_ffn_in, w_ffn_out, final_norm_g):
    b, s, d = x.shape
    depth = w_ada.shape[0]
    assert s % TM_QKV == 0 and s % (B_GROUPS[-1][1] * WIN) == 0 and TK % TM_QKV == 0 and s % TK == 0
    tabs = _rope_tables(s)
    heads_per_vreg = LANES // HEAD_DIM
    lane_head = jnp.arange(LANES) // HEAD_DIM
    bd = (lane_head[:, None] == lane_head[None, :]).astype(BF16)
    b_q_lo = A_Q_W + 2 * A_KV_W
    col_scale = jnp.ones((w_qkv.shape[-1],), F32).at[b_q_lo:b_q_lo + B_W].set(SCALE)
    c_pad = jnp.pad(c, ((0, 16 - b), (0, 0)))
    for l in range(depth):
        mod3 = _modulation(c_pad, w_ada[l], b_ada[l])[:b].reshape(b, 6, d)
        qg = jnp.tile(q_norm_a[l] * (SCALE * LOG2E), heads_per_vreg).reshape(1, LANES)
        kg = jnp.tile(k_norm_a[l], heads_per_vreg).reshape(1, LANES)
        g1 = norm1_g[l].reshape(1, d)
        wq = (w_qkv[l] * col_scale[None, :]).astype(BF16)
        qat, ka, vat, qb, kb, vb = _qkv_project(x, mod3, g1, wq, qg, kg, bd, tabs)
        attn = _global_attention(qat, ka, vat)
        branch = [_banded_attention(qb, kb, vb, g) for g in range(len(B_GROUPS))]
        x = _mix(x, mod3, g1, attn, [o for o, _ in branch], [e for _, e in branch],
                 w_gate[l].astype(BF16), b_gate[l].reshape(1, -1), w_proj_a[l].astype(BF16),
                 w_proj_b[l].astype(BF16), w_o[l].astype(BF16))
        x = _ffn(x, mod3, norm2_g[l].reshape(1, d), w_ffn_in[l].astype(BF16),
                 w_ffn_out[l].astype(BF16), final_norm_g.reshape(1, d), final=(l == depth - 1))
    return x
```

```python
import functools

import jax
import jax.numpy as jnp
from jax import lax
from jax.experimental import pallas as pl
from jax.experimental.pallas import tpu as pltpu

HEAD_DIM = 64
A_Q_HEADS = 8
A_KV_HEADS = 2
A_GROUP = A_Q_HEADS // A_KV_HEADS
B_GROUPS = ((128, 1), (512, 4), (2048, 16))
B_HEADS_PER_GROUP = 4
B_HEADS = B_HEADS_PER_GROUP * len(B_GROUPS)
A_Q_W = A_Q_HEADS * HEAD_DIM
A_KV_W = A_KV_HEADS * HEAD_DIM
B_W = B_HEADS * HEAD_DIM
B_OUT_W = B_HEADS_PER_GROUP * HEAD_DIM
GRID_W = 64
AXIAL_THETA = 10000.0
PARTIAL_THETA = 500000.0
PARTIAL_ROT_DIM = HEAD_DIM // 4
EPS = 1e-6
NEG_INF = -1e30
SCALE = HEAD_DIM ** -0.5
LOG2E = 1.4426950408889634

LANES = 128
VMEM_LIMIT = 56 * 1024 * 1024

F32 = jnp.float32
BF16 = jnp.bfloat16

TM_QKV = 512
TQ = 128
TK = 1024
TM_MIX = 512
TM_FFN = 512
BLK = 128
WIN = 256
VT_ROWS = HEAD_DIM + 16
GQA_UNROLL = 2
GQA_TILES = 4


def _params(*sem):
    return pltpu.CompilerParams(dimension_semantics=sem, vmem_limit_bytes=VMEM_LIMIT)


def _mod_kernel(c_ref, w_ref, b_ref, o_ref):
    c = c_ref[...]
    cond = c * jax.nn.sigmoid(c)
    c_hi = cond.astype(BF16)
    c_lo = (cond - c_hi.astype(F32)).astype(BF16)
    w = w_ref[...]
    w_hi = w.astype(BF16)
    w_lo = (w - w_hi.astype(F32)).astype(BF16)
    acc = jnp.dot(c_hi, w_hi, preferred_element_type=F32)
    acc += jnp.dot(c_hi, w_lo, preferred_element_type=F32)
    acc += jnp.dot(c_lo, w_hi, preferred_element_type=F32)
    o_ref[...] = acc + b_ref[...]


def _modulation(c_pad, w_ada, b_ada):
    rows, d = c_pad.shape
    n = w_ada.shape[1]
    tn = 1024
    return pl.pallas_call(
        _mod_kernel,
        grid=(n // tn,),
        in_specs=[pl.BlockSpec((rows, d), lambda j: (0, 0)),
                  pl.BlockSpec((d, tn), lambda j: (0, j)),
                  pl.BlockSpec((1, tn), lambda j: (0, j))],
        out_specs=pl.BlockSpec((rows, tn), lambda j: (0, j)),
        out_shape=jax.ShapeDtypeStruct((rows, n), F32),
        compiler_params=_params("parallel"),
        name="mod",
    )(c_pad, w_ada, b_ada.reshape(1, n))


def _modulated_norm(x, gain, shift, scale):
    ms = jnp.mean(x * x, axis=-1, keepdims=True)
    y = (x * lax.rsqrt(ms + EPS)) * gain
    return (y * (1.0 + scale) + shift).astype(BF16)


def _rope(y, cos, sin_lo, sin_hi, shift):
    return (y * cos + pltpu.roll(y, LANES - shift, 1) * sin_lo
            + pltpu.roll(y, shift, 1) * sin_hi)


def _qkv_kernel(x_ref, mod_ref, g1_ref, w_ref, qg_ref, kg_ref, bd_ref,
                ac_ref, asl_ref, ash_ref, pc_ref, psl_ref, psh_ref,
                qat_ref, ka_ref, vat_ref, grp0_ref, grp1_ref, grp2_ref, slab_ref):
    u = _modulated_norm(x_ref[...], g1_ref[...], mod_ref[0:1, :], mod_ref[1:2, :])

    def proj(lo, width):
        return jnp.dot(u, w_ref[:, lo:lo + width], preferred_element_type=F32)

    def head_norm_rope(chunk, gain):
        ss = jnp.dot((chunk * chunk).astype(BF16), bd_ref[...], preferred_element_type=F32)
        y = (chunk * lax.rsqrt(ss * (1.0 / HEAD_DIM) + EPS)) * gain
        return _rope(y, ac_ref[...], asl_ref[...], ash_ref[...], HEAD_DIM // 4)

    qa = proj(0, A_Q_W)
    qa = jnp.concatenate(
        [head_norm_rope(qa[:, i * LANES:(i + 1) * LANES], qg_ref[...]) for i in range(A_Q_W // LANES)],
        axis=1)
    qat = qa.T.astype(BF16)
    for i in range(qat_ref.shape[0]):
        qat_ref[i] = qat[:, i * TQ:(i + 1) * TQ]

    ka = proj(A_Q_W, A_KV_W)
    ka_ref[...] = head_norm_rope(ka, kg_ref[...]).astype(BF16)

    vat = proj(A_Q_W + A_KV_W, A_KV_W).T.astype(BF16)
    pad_row = lax.broadcasted_iota(jnp.int32, (VT_ROWS - HEAD_DIM, vat.shape[1]), 0)
    ones_row = jnp.where(pad_row == 0, 1.0, 0.0).astype(BF16)
    for hk in range(A_KV_HEADS):
        vat_ref[hk, 0:HEAD_DIM, :] = vat[hk * HEAD_DIM:(hk + 1) * HEAD_DIM, :]
        vat_ref[hk, HEAD_DIM:VT_ROWS, :] = ones_row

    base = A_Q_W + 2 * A_KV_W
    tm = u.shape[0]
    halves = B_OUT_W // LANES
    for g, (out_ref, (_, dil)) in enumerate(zip((grp0_ref, grp1_ref, grp2_ref), B_GROUPS)):
        for part in range(3):
            wide = proj(base + part * B_W + g * B_OUT_W, B_OUT_W)
            for c in range(halves):
                chunk = wide[:, c * LANES:(c + 1) * LANES]
                if part < 2:
                    chunk = _rope(chunk, pc_ref[...], psl_ref[...], psh_ref[...], PARTIAL_ROT_DIM // 2)
                slab = part * halves + c
                if dil == 1:
                    out_ref[0, :, slab * LANES:(slab + 1) * LANES] = chunk.astype(BF16)
                else:
                    slab_ref[slab] = chunk
        if dil > 1:
            for r in range(dil):
                for slab in range(3 * halves):
                    out_ref[r, :, slab * LANES:(slab + 1) * LANES] = (
                        slab_ref[slab, pl.ds(r, tm // dil, stride=dil), :].astype(BF16))


def _qkv_project(x, mod3, g1, w_qkv, qg, kg, bd, tabs):
    b, s, d = x.shape
    tm = TM_QKV
    ncols = w_qkv.shape[1]
    row_tab = pl.BlockSpec((tm, LANES), lambda i, bb: (i, 0))
    const2 = lambda shape: pl.BlockSpec(shape, lambda i, bb: (0, 0))
    tok = lambda w: pl.BlockSpec((None, tm, w), lambda i, bb: (bb, i, 0))
    out_shapes = (
        jax.ShapeDtypeStruct((b, s // TQ, A_Q_W, TQ), BF16),
        jax.ShapeDtypeStruct((b, s, A_KV_W), BF16),
        jax.ShapeDtypeStruct((b, s // TK, A_KV_HEADS, VT_ROWS, TK), BF16),
    ) + tuple(jax.ShapeDtypeStruct((b, dil, s // dil, 3 * B_OUT_W), BF16) for _, dil in B_GROUPS)
    out_specs = (
        pl.BlockSpec((None, tm // TQ, A_Q_W, TQ), lambda i, bb: (bb, i, 0, 0)),
        tok(A_KV_W),
        pl.BlockSpec((None, None, A_KV_HEADS, VT_ROWS, tm),
                     lambda i, bb: (bb, i // (TK // tm), 0, 0, i % (TK // tm))),
    ) + tuple(pl.BlockSpec((None, dil, tm // dil, 3 * B_OUT_W), lambda i, bb: (bb, 0, i, 0))
              for _, dil in B_GROUPS)
    return pl.pallas_call(
        _qkv_kernel,
        grid=(s // tm, b),
        in_specs=[tok(d),
                  pl.BlockSpec((None, 6, d), lambda i, bb: (bb, 0, 0)),
                  const2((1, d)),
                  const2((d, ncols)),
                  const2((1, LANES)), const2((1, LANES)), const2((LANES, LANES)),
                  row_tab, row_tab, row_tab, row_tab, row_tab, row_tab],
        out_specs=out_specs,
        out_shape=out_shapes,
        scratch_shapes=[pltpu.VMEM((3 * B_OUT_W // LANES, tm, LANES), F32)],
        compiler_params=_params("parallel", "parallel"),
        name="qkv",
    )(x, mod3, g1, w_qkv, qg, kg, bd, *tabs)


def _gqa_kernel(qt_ref, k_ref, vt_ref, o_ref, qx_ref, s_ref):
    nk = vt_ref.shape[0]
    ncol = A_GROUP * TQ
    units = [(t, hk) for t in range(GQA_TILES) for hk in range(A_KV_HEADS)]

    qx_ref[...] = jnp.zeros(qx_ref.shape, BF16)
    for u, (t, hk) in enumerate(units):
        for g in range(A_GROUP):
            h = hk * A_GROUP + g
            qx_ref[u, hk * HEAD_DIM:(hk + 1) * HEAD_DIM, g * TQ:(g + 1) * TQ] = (
                qt_ref[t, h * HEAD_DIM:(h + 1) * HEAD_DIM, :])

    def phase(score_unit, value_unit, m_prev):
        qx = None if score_unit is None else qx_ref[score_unit]

        def body(j, carry):
            mrun, acc = carry
            rows = pl.ds(pl.multiple_of(j * TK, TK), TK)
            if score_unit is not None:
                st = jnp.dot(k_ref[rows, :], qx, preferred_element_type=F32)
                s_ref[score_unit % 2, rows, :] = st
                mrun = jnp.maximum(mrun, jnp.max(st.reshape(TK // 8, 8, ncol), axis=0))
            if value_unit is not None:
                p = jnp.exp2((s_ref[value_unit % 2, rows, :] - m_prev).astype(BF16))
                acc = acc + jnp.dot(vt_ref[j, units[value_unit][1]], p, preferred_element_type=F32)
            return mrun, acc

        init = (jnp.full((8, ncol), NEG_INF, F32), jnp.zeros((VT_ROWS, ncol), F32))
        mrun, acc = lax.fori_loop(0, nk, body, init, unroll=GQA_UNROLL)
        return jnp.max(mrun, axis=0, keepdims=True), acc

    m = None
    for idx in range(len(units) + 1):
        score_unit = idx if idx < len(units) else None
        value_unit = idx - 1 if idx > 0 else None
        m, acc = phase(score_unit, value_unit, m)
        if value_unit is not None:
            t, hk = units[value_unit]
            out_t = acc[0:HEAD_DIM, :] / acc[HEAD_DIM:HEAD_DIM + 1, :]
            stacked = jnp.concatenate([out_t[:, g * TQ:(g + 1) * TQ] for g in range(A_GROUP)], axis=0)
            o_ref[t * TQ:(t + 1) * TQ, hk * A_GROUP * HEAD_DIM:(hk + 1) * A_GROUP * HEAD_DIM] = (
                stacked.T.astype(BF16))


def _global_attention(qat, ka, vat):
    b, nq = qat.shape[0], qat.shape[1]
    s = ka.shape[1]
    nk = vat.shape[1]
    ncol = A_GROUP * TQ
    return pl.pallas_call(
        _gqa_kernel,
        grid=(b, nq // GQA_TILES),
        in_specs=[pl.BlockSpec((None, GQA_TILES, A_Q_W, TQ), lambda bb, i: (bb, i, 0, 0)),
                  pl.BlockSpec((None, s, A_KV_W), lambda bb, i: (bb, 0, 0)),
                  pl.BlockSpec((None, nk, A_KV_HEADS, VT_ROWS, TK), lambda bb, i: (bb, 0, 0, 0, 0))],
        out_specs=pl.BlockSpec((None, GQA_TILES * TQ, A_Q_W), lambda bb, i: (bb, i, 0)),
        out_shape=jax.ShapeDtypeStruct((b, s, A_Q_W), BF16),
        scratch_shapes=[pltpu.VMEM((GQA_TILES * A_KV_HEADS, A_KV_W, ncol), BF16),
                        pltpu.VMEM((2, s, ncol), F32)],
        compiler_params=_params("parallel", "parallel"),
        name="gqa",
    )(qat, ka, vat)


def _swa_kernel(q_ref, k_ref, v_ref, o_ref, lse_ref, *, radius):
    seq = q_ref.shape[0]
    lane = lax.broadcasted_iota(jnp.int32, (BLK, LANES), 1)
    first_head = lane < HEAD_DIM
    rel = (lax.broadcasted_iota(jnp.int32, (BLK, WIN), 0)
           - lax.broadcasted_iota(jnp.int32, (BLK, WIN), 1))

    def block(i, _):
        i0 = pl.multiple_of(i * BLK, BLK)
        ws = pl.multiple_of(jnp.clip(i0 - radius, 0, seq - WIN), radius)
        valid = jnp.abs(rel + (i0 - ws)) <= radius
        for c in range(B_OUT_W // LANES):
            cols = slice(c * LANES, (c + 1) * LANES)
            q = q_ref[pl.ds(i0, BLK), cols]
            k = k_ref[pl.ds(ws, WIN), cols]
            v = v_ref[pl.ds(ws, WIN), cols]
            outs, lses = [], []
            for hh in range(2):
                sel = first_head if hh == 0 else jnp.logical_not(first_head)
                qm = jnp.where(sel, q, jnp.zeros_like(q))
                sc = lax.dot_general(qm, k, (((1,), (1,)), ((), ())), preferred_element_type=F32)
                sc = jnp.where(valid, sc, NEG_INF)
                m = jnp.max(sc, axis=-1, keepdims=True)
                p = jnp.exp(sc - m)
                l = jnp.sum(p, axis=-1, keepdims=True)
                outs.append(jnp.dot(p.astype(BF16), v, preferred_element_type=F32) / l)
                lses.append(m + jnp.log(l))
            o_ref[pl.ds(i0, BLK), cols] = jnp.where(first_head, outs[0], outs[1]).astype(BF16)
            lse_ref[pl.ds(i0, BLK), cols] = jnp.where(first_head, lses[0], lses[1])
        return 0

    lax.fori_loop(0, seq // BLK, block, 0, unroll=min(4, seq // BLK))


def _banded_attention(qkv, group):
    window, dil = B_GROUPS[group]
    b, _, seq, _ = qkv.shape
    radius = window // 2 // dil
    part = lambda p: pl.BlockSpec((None, None, seq, B_OUT_W), lambda bb, r: (bb, r, 0, p))
    out_spec = part(0)
    return pl.pallas_call(
        functools.partial(_swa_kernel, radius=radius),
        grid=(b, dil),
        in_specs=[part(0), part(1), part(2)],
        out_specs=(out_spec, out_spec),
        out_shape=(jax.ShapeDtypeStruct((b, dil, seq, B_OUT_W), BF16),
                   jax.ShapeDtypeStruct((b, dil, seq, B_OUT_W), F32)),
        compiler_params=_params("parallel", "parallel"),
        name=f"swa{group}",
    )(qkv, qkv, qkv)


def _mix_kernel(x_ref, mod_ref, g1_ref, attn_ref, o0_ref, o1_ref, o2_ref, l0_ref, l1_ref, l2_ref,
                wg_ref, bg_ref, wpa_ref, wpb_ref, wo_ref, out_ref, nat_ref):
    x = x_ref[...]
    d = x.shape[-1]
    u = _modulated_norm(x, g1_ref[...], mod_ref[0:1, :], mod_ref[1:2, :])
    gates = jax.nn.sigmoid(jnp.dot(u, wg_ref[...], preferred_element_type=F32) + bg_ref[...])
    ya = jnp.dot(attn_ref[...], wpa_ref[...], preferred_element_type=F32)

    halves = B_OUT_W // LANES
    slot = 0
    natural = []
    for src_refs in ((o0_ref, l0_ref), (o1_ref, l1_ref), (o2_ref, l2_ref)):
        per_kind = []
        for src in src_refs:
            dil, n = src.shape[0], src.shape[1]
            if dil == 1:
                per_kind.append([src[0, :, c * LANES:(c + 1) * LANES].astype(F32) for c in range(halves)])
                continue
            for r in range(dil):
                for c in range(halves):
                    nat_ref[slot + c, pl.ds(r, n, stride=dil), :] = (
                        src[r, :, c * LANES:(c + 1) * LANES].astype(F32))
            per_kind.append([nat_ref[slot + c] for c in range(halves)])
            slot += halves
        natural.append(per_kind)

    comb = []
    for c in range(halves):
        lses = [grp[1][c] for grp in natural]
        top = jnp.maximum(jnp.maximum(lses[0], lses[1]), lses[2])
        es = [jnp.exp(l - top) for l in lses]
        den = es[0] + es[1] + es[2]
        comb.append(sum((e / den) * grp[0][c] for e, grp in zip(es, natural)).astype(BF16))
    yb = jnp.dot(jnp.concatenate(comb, axis=1), wpb_ref[...], preferred_element_type=F32)
    merged = (gates[:, :d] * ya + gates[:, d:] * yb).astype(BF16)
    mix = jnp.dot(merged, wo_ref[...], preferred_element_type=F32)
    out_ref[...] = x + mod_ref[2:3, :] * mix


def _mix(x, mod3, g1, attn, os_, lses, wg, bg, wpa, wpb, wo):
    b, s, d = x.shape
    tm = TM_MIX
    tok = lambda w: pl.BlockSpec((None, tm, w), lambda bb, i: (bb, i, 0))
    const2 = lambda shape: pl.BlockSpec(shape, lambda bb, i: (0, 0))
    sub = [pl.BlockSpec((None, dil, tm // dil, B_OUT_W), lambda bb, i: (bb, 0, i, 0)) for _, dil in B_GROUPS]
    n_strided = sum(dil > 1 for _, dil in B_GROUPS) * 2 * (B_OUT_W // LANES)
    return pl.pallas_call(
        _mix_kernel,
        grid=(b, s // tm),
        in_specs=[tok(d),
                  pl.BlockSpec((None, 6, d), lambda bb, i: (bb, 0, 0)),
                  const2((1, d)),
                  tok(A_Q_W),
                  *sub, *sub,
                  const2(wg.shape), const2(bg.shape), const2(wpa.shape), const2(wpb.shape),
                  const2(wo.shape)],
        out_specs=tok(d),
        out_shape=jax.ShapeDtypeStruct((b, s, d), F32),
        scratch_shapes=[pltpu.VMEM((n_strided, tm, LANES), F32)],
        compiler_params=_params("parallel", "parallel"),
        name="mix",
    )(x, mod3, g1, attn, *os_, *lses, wg, bg, wpa, wpb, wo)


def _ffn_kernel(x_ref, mod_ref, g2_ref, win_ref, wout_ref, fg_ref, out_ref, *, final, chunk):
    x = x_ref[...]
    d_ff = wout_ref.shape[0]
    u = _modulated_norm(x, g2_ref[...], mod_ref[3:4, :], mod_ref[4:5, :])
    acc = jnp.zeros(x.shape, F32)
    for lo in range(0, d_ff, chunk):
        hg = jnp.dot(u, win_ref[:, lo:lo + chunk], preferred_element_type=F32)
        hu = jnp.dot(u, win_ref[:, d_ff + lo:d_ff + lo + chunk], preferred_element_type=F32)
        act = ((hg * jax.nn.sigmoid(hg)) * hu).astype(BF16)
        acc += jnp.dot(act, wout_ref[lo:lo + chunk, :], preferred_element_type=F32)
    y = x + mod_ref[5:6, :] * acc
    if final:
        ms = jnp.mean(y * y, axis=-1, keepdims=True)
        y = (y * lax.rsqrt(ms + EPS)) * fg_ref[...]
    out_ref[...] = y


def _ffn(x, mod3, g2, win, wout, fg, final):
    b, s, d = x.shape
    tm = TM_FFN
    d_ff = wout.shape[0]
    chunk = d_ff // 2 if (d_ff // 2) % LANES == 0 else d_ff
    tok = pl.BlockSpec((None, tm, d), lambda bb, i: (bb, i, 0))
    resident = lambda shape: pl.BlockSpec(shape, lambda bb, i: (0, 0), pipeline_mode=pl.Buffered(1))
    return pl.pallas_call(
        functools.partial(_ffn_kernel, final=final, chunk=chunk),
        grid=(b, s // tm),
        in_specs=[tok,
                  pl.BlockSpec((None, 6, d), lambda bb, i: (bb, 0, 0)),
                  pl.BlockSpec((1, d), lambda bb, i: (0, 0)),
                  resident(win.shape), resident(wout.shape),
                  pl.BlockSpec((1, d), lambda bb, i: (0, 0))],
        out_specs=tok,
        out_shape=jax.ShapeDtypeStruct((b, s, d), F32),
        compiler_params=_params("parallel", "parallel"),
        name="ffn",
    )(x, mod3, g2, win, wout, fg)


def _angles(pos, dim, theta):
    inv = theta ** (-jnp.arange(0, dim, 2, dtype=F32) / dim)
    return pos.astype(F32)[:, None] * inv[None, :]


def _rope_tables(s):
    t = jnp.arange(s, dtype=jnp.int32)
    half = HEAD_DIM // 2
    ar = _angles(t // GRID_W, half, AXIAL_THETA)
    ac = _angles(t % GRID_W, half, AXIAL_THETA)
    zr = jnp.zeros_like(ar)
    a_cos = jnp.concatenate([jnp.cos(ar), jnp.cos(ar), jnp.cos(ac), jnp.cos(ac)], axis=1)
    a_lo = jnp.concatenate([-jnp.sin(ar), zr, -jnp.sin(ac), zr], axis=1)
    a_hi = jnp.concatenate([zr, jnp.sin(ar), zr, jnp.sin(ac)], axis=1)
    ap = _angles(t, PARTIAL_ROT_DIM, PARTIAL_THETA)
    rest = HEAD_DIM - PARTIAL_ROT_DIM
    zp = jnp.zeros_like(ap)
    p_cos = jnp.concatenate([jnp.cos(ap), jnp.cos(ap), jnp.ones((s, rest), F32)], axis=1)
    p_lo = jnp.concatenate([-jnp.sin(ap), zp, jnp.zeros((s, rest), F32)], axis=1)
    p_hi = jnp.concatenate([zp, jnp.sin(ap), jnp.zeros((s, rest), F32)], axis=1)
    two = lambda a: jnp.tile(a, (1, LANES // HEAD_DIM))
    return tuple(two(a) for a in (a_cos, a_lo, a_hi, p_cos, p_lo, p_hi))


def kernel(x, c, w_ada, b_ada, norm1_g, w_qkv, q_norm_a, k_norm_a, w_proj_a, w_proj_b, w_gate,
           b_gate, w_o, norm2_g, w_ffn_in, w_ffn_out, final_norm_g):
    b, s, d = x.shape
    depth = w_ada.shape[0]
    assert s % TM_QKV == 0 and s % (B_GROUPS[-1][1] * WIN) == 0 and TK % TM_QKV == 0 and s % TK == 0
    tabs = _rope_tables(s)
    heads_per_vreg = LANES // HEAD_DIM
    lane_head = jnp.arange(LANES) // HEAD_DIM
    bd = (lane_head[:, None] == lane_head[None, :]).astype(BF16)
    b_q_lo = A_Q_W + 2 * A_KV_W
    col_scale = jnp.ones((w_qkv.shape[-1],), F32).at[b_q_lo:b_q_lo + B_W].set(SCALE)
    c_pad = jnp.pad(c, ((0, 16 - b), (0, 0)))
    for l in range(depth):
        mod3 = _modulation(c_pad, w_ada[l], b_ada[l])[:b].reshape(b, 6, d)
        qg = jnp.tile(q_norm_a[l] * (SCALE * LOG2E), heads_per_vreg).reshape(1, LANES)
        kg = jnp.tile(k_norm_a[l], heads_per_vreg).reshape(1, LANES)
        g1 = norm1_g[l].reshape(1, d)
        wq = (w_qkv[l] * col_scale[None, :]).astype(BF16)
        qat, ka, vat, *groups = _qkv_project(x, mod3, g1, wq, qg, kg, bd, tabs)
        attn = _global_attention(qat, ka, vat)
        branch = [_banded_attention(qkv_g, g) for g, qkv_g in enumerate(groups)]
        x = _mix(x, mod3, g1, attn, [o for o, _ in branch], [e for _, e in branch],
                 w_gate[l].astype(BF16), b_gate[l].reshape(1, -1), w_proj_a[l].astype(BF16),
                 w_proj_b[l].astype(BF16), w_o[l].astype(BF16))
        x = _ffn(x, mod3, norm2_g[l].reshape(1, d), w_ffn_in[l].astype(BF16),
                 w_ffn_out[l].astype(BF16), final_norm_g.reshape(1, d), final=(l == depth - 1))
    return x
```

```python
import functools

import jax
import jax.numpy as jnp
from jax import lax
from jax.experimental import pallas as pl
from jax.experimental.pallas import tpu as pltpu

HEAD_DIM = 64
A_Q_HEADS = 8
A_KV_HEADS = 2
A_GROUP = A_Q_HEADS // A_KV_HEADS
B_GROUPS = ((128, 1), (512, 4), (2048, 16))
B_HEADS_PER_GROUP = 4
B_HEADS = B_HEADS_PER_GROUP * len(B_GROUPS)
A_Q_W = A_Q_HEADS * HEAD_DIM
A_KV_W = A_KV_HEADS * HEAD_DIM
B_W = B_HEADS * HEAD_DIM
B_OUT_W = B_HEADS_PER_GROUP * HEAD_DIM
GRID_W = 64
AXIAL_THETA = 10000.0
PARTIAL_THETA = 500000.0
PARTIAL_ROT_DIM = HEAD_DIM // 4
EPS = 1e-6
NEG_INF = -1e30
SCALE = HEAD_DIM ** -0.5
LOG2E = 1.4426950408889634
LN2 = 0.6931471805599453

LANES = 128
MXU_DIM = 256
VMEM_LIMIT = 56 * 1024 * 1024

F32 = jnp.float32
BF16 = jnp.bfloat16

TM_QKV = 512
TQ = 128
TK = 1024
TM_MIX = 512
TM_FFN = 512
BLK = 128
WIN = 256
VT_ROWS = HEAD_DIM + 16
GQA_UNROLL = 2
GQA_TILES = 4


def _params(*sem):
    return pltpu.CompilerParams(dimension_semantics=sem, vmem_limit_bytes=VMEM_LIMIT)


def _mod_kernel(c_ref, w_ref, b_ref, o_ref):
    c = c_ref[...]
    cond = c * jax.nn.sigmoid(c)
    c_hi = cond.astype(BF16)
    c_lo = (cond - c_hi.astype(F32)).astype(BF16)
    w = w_ref[...]
    w_hi = w.astype(BF16)
    w_lo = (w - w_hi.astype(F32)).astype(BF16)
    acc = jnp.dot(c_hi, w_hi, preferred_element_type=F32)
    acc += jnp.dot(c_hi, w_lo, preferred_element_type=F32)
    acc += jnp.dot(c_lo, w_hi, preferred_element_type=F32)
    o_ref[...] = acc + b_ref[...]


def _modulation(c_pad, w_ada, b_ada):
    rows, d = c_pad.shape
    n = w_ada.shape[1]
    tn = 1024
    return pl.pallas_call(
        _mod_kernel,
        grid=(n // tn,),
        in_specs=[pl.BlockSpec((rows, d), lambda j: (0, 0)),
                  pl.BlockSpec((d, tn), lambda j: (0, j)),
                  pl.BlockSpec((1, tn), lambda j: (0, j))],
        out_specs=pl.BlockSpec((rows, tn), lambda j: (0, j)),
        out_shape=jax.ShapeDtypeStruct((rows, n), F32),
        compiler_params=_params("parallel"),
        name="mod",
    )(c_pad, w_ada, b_ada.reshape(1, n))


def _modulated_norm(x, gain, shift, scale):
    ms = jnp.mean(x * x, axis=-1, keepdims=True)
    y = (x * lax.rsqrt(ms + EPS)) * gain
    return (y * (1.0 + scale) + shift).astype(BF16)


def _rope(y, cos, sin_lo, sin_hi, shift):
    return (y * cos + pltpu.roll(y, LANES - shift, 1) * sin_lo
            + pltpu.roll(y, shift, 1) * sin_hi)


def _qkv_kernel(x_ref, mod_ref, g1_ref, w_ref, qg_ref, kg_ref, bd_ref,
                ac_ref, asl_ref, ash_ref, pc_ref, psl_ref, psh_ref,
                qat_ref, ka_ref, vat_ref, grp0_ref, grp1_ref, grp2_ref, slab_ref):
    u = _modulated_norm(x_ref[...], g1_ref[...], mod_ref[0:1, :], mod_ref[1:2, :])

    def proj(lo, width):
        return jnp.dot(u, w_ref[:, lo:lo + width], preferred_element_type=F32)

    def head_norm_rope(chunk, gain):
        ss = jnp.dot((chunk * chunk).astype(BF16), bd_ref[...], preferred_element_type=F32)
        y = (chunk * lax.rsqrt(ss * (1.0 / HEAD_DIM) + EPS)) * gain
        return _rope(y, ac_ref[...], asl_ref[...], ash_ref[...], HEAD_DIM // 4)

    qa = proj(0, A_Q_W)
    qa = jnp.concatenate(
        [head_norm_rope(qa[:, i * LANES:(i + 1) * LANES], qg_ref[...]) for i in range(A_Q_W // LANES)],
        axis=1)
    qat = qa.T.astype(BF16)
    for i in range(qat_ref.shape[0]):
        qat_ref[i] = qat[:, i * TQ:(i + 1) * TQ]

    kva = proj(A_Q_W, 2 * A_KV_W)
    ka_ref[...] = head_norm_rope(kva[:, :A_KV_W], kg_ref[...]).astype(BF16)

    vat = kva[:, A_KV_W:].T.astype(BF16)
    pad_row = lax.broadcasted_iota(jnp.int32, (VT_ROWS - HEAD_DIM, vat.shape[1]), 0)
    ones_row = jnp.where(pad_row == 0, 1.0, 0.0).astype(BF16)
    for hk in range(A_KV_HEADS):
        vat_ref[hk, 0:HEAD_DIM, :] = vat[hk * HEAD_DIM:(hk + 1) * HEAD_DIM, :]
        vat_ref[hk, HEAD_DIM:VT_ROWS, :] = ones_row

    base = A_Q_W + 2 * A_KV_W
    tm = u.shape[0]
    halves = B_OUT_W // LANES
    for g, (out_ref, (_, dil)) in enumerate(zip((grp0_ref, grp1_ref, grp2_ref), B_GROUPS)):
        for part in range(3):
            wide = proj(base + part * B_W + g * B_OUT_W, B_OUT_W)
            for c in range(halves):
                chunk = wide[:, c * LANES:(c + 1) * LANES]
                if part < 2:
                    chunk = _rope(chunk, pc_ref[...], psl_ref[...], psh_ref[...], PARTIAL_ROT_DIM // 2)
                slab = part * halves + c
                if dil == 1:
                    out_ref[0, :, slab * LANES:(slab + 1) * LANES] = chunk.astype(BF16)
                else:
                    slab_ref[slab] = chunk
        if dil > 1:
            for r in range(dil):
                for slab in range(3 * halves):
                    out_ref[r, :, slab * LANES:(slab + 1) * LANES] = (
                        slab_ref[slab, pl.ds(r, tm // dil, stride=dil), :].astype(BF16))


def _qkv_project(x, mod3, g1, w_qkv, qg, kg, bd, tabs):
    b, s, d = x.shape
    tm = TM_QKV
    ncols = w_qkv.shape[1]
    row_tab = pl.BlockSpec((tm, LANES), lambda i, bb: (i, 0))
    const2 = lambda shape: pl.BlockSpec(shape, lambda i, bb: (0, 0))
    tok = lambda w: pl.BlockSpec((None, tm, w), lambda i, bb: (bb, i, 0))
    out_shapes = (
        jax.ShapeDtypeStruct((b, s // TQ, A_Q_W, TQ), BF16),
        jax.ShapeDtypeStruct((b, s, A_KV_W), BF16),
        jax.ShapeDtypeStruct((b, s // TK, A_KV_HEADS, VT_ROWS, TK), BF16),
    ) + tuple(jax.ShapeDtypeStruct((b, dil, s // dil, 3 * B_OUT_W), BF16) for _, dil in B_GROUPS)
    out_specs = (
        pl.BlockSpec((None, tm // TQ, A_Q_W, TQ), lambda i, bb: (bb, i, 0, 0)),
        tok(A_KV_W),
        pl.BlockSpec((None, None, A_KV_HEADS, VT_ROWS, tm),
                     lambda i, bb: (bb, i // (TK // tm), 0, 0, i % (TK // tm))),
    ) + tuple(pl.BlockSpec((None, dil, tm // dil, 3 * B_OUT_W), lambda i, bb: (bb, 0, i, 0))
              for _, dil in B_GROUPS)
    return pl.pallas_call(
        _qkv_kernel,
        grid=(s // tm, b),
        in_specs=[tok(d),
                  pl.BlockSpec((None, 6, d), lambda i, bb: (bb, 0, 0)),
                  const2((1, d)),
                  const2((d, ncols)),
                  const2((1, LANES)), const2((1, LANES)), const2((LANES, LANES)),
                  row_tab, row_tab, row_tab, row_tab, row_tab, row_tab],
        out_specs=out_specs,
        out_shape=out_shapes,
        scratch_shapes=[pltpu.VMEM((3 * B_OUT_W // LANES, tm, LANES), F32)],
        compiler_params=_params("parallel", "parallel"),
        name="qkv",
    )(x, mod3, g1, w_qkv, qg, kg, bd, *tabs)


def _gqa_kernel(qt_ref, k_ref, vt_ref, o_ref, qx_ref, s_ref):
    nk = vt_ref.shape[0]
    ncol = A_GROUP * TQ
    units = [(t, hk) for t in range(GQA_TILES) for hk in range(A_KV_HEADS)]

    qx_ref[...] = jnp.zeros(qx_ref.shape, BF16)
    for u, (t, hk) in enumerate(units):
        for g in range(A_GROUP):
            h = hk * A_GROUP + g
            qx_ref[u, hk * HEAD_DIM:(hk + 1) * HEAD_DIM, g * TQ:(g + 1) * TQ] = (
                qt_ref[t, h * HEAD_DIM:(h + 1) * HEAD_DIM, :])

    def phase(score_unit, value_unit, m_prev):
        qx = None if score_unit is None else qx_ref[score_unit]

        def body(j, carry):
            mrun, acc = carry
            rows = pl.ds(pl.multiple_of(j * TK, TK), TK)
            if score_unit is not None:
                st = jnp.dot(k_ref[rows, :], qx, preferred_element_type=F32)
                s_ref[score_unit % 2, rows, :] = st
                mrun = jnp.maximum(mrun, jnp.max(st.reshape(TK // 8, 8, ncol), axis=0))
            if value_unit is not None:
                p = jnp.exp2((s_ref[value_unit % 2, rows, :] - m_prev).astype(BF16))
                acc = acc + jnp.dot(vt_ref[j, units[value_unit][1]], p, preferred_element_type=F32)
            return mrun, acc

        init = (jnp.full((8, ncol), NEG_INF, F32), jnp.zeros((VT_ROWS, ncol), F32))
        mrun, acc = lax.fori_loop(0, nk, body, init, unroll=GQA_UNROLL)
        return jnp.max(mrun, axis=0, keepdims=True), acc

    m = None
    for idx in range(len(units) + 1):
        score_unit = idx if idx < len(units) else None
        value_unit = idx - 1 if idx > 0 else None
        m, acc = phase(score_unit, value_unit, m)
        if value_unit is not None:
            t, hk = units[value_unit]
            out_t = acc[0:HEAD_DIM, :] / acc[HEAD_DIM:HEAD_DIM + 1, :]
            stacked = jnp.concatenate([out_t[:, g * TQ:(g + 1) * TQ] for g in range(A_GROUP)], axis=0)
            o_ref[t * TQ:(t + 1) * TQ, hk * A_GROUP * HEAD_DIM:(hk + 1) * A_GROUP * HEAD_DIM] = (
                stacked.T.astype(BF16))


def _global_attention(qat, ka, vat):
    b, nq = qat.shape[0], qat.shape[1]
    s = ka.shape[1]
    nk = vat.shape[1]
    ncol = A_GROUP * TQ
    return pl.pallas_call(
        _gqa_kernel,
        grid=(b, nq // GQA_TILES),
        in_specs=[pl.BlockSpec((None, GQA_TILES, A_Q_W, TQ), lambda bb, i: (bb, i, 0, 0)),
                  pl.BlockSpec((None, s, A_KV_W), lambda bb, i: (bb, 0, 0)),
                  pl.BlockSpec((None, nk, A_KV_HEADS, VT_ROWS, TK), lambda bb, i: (bb, 0, 0, 0, 0))],
        out_specs=pl.BlockSpec((None, GQA_TILES * TQ, A_Q_W), lambda bb, i: (bb, i, 0)),
        out_shape=jax.ShapeDtypeStruct((b, s, A_Q_W), BF16),
        scratch_shapes=[pltpu.VMEM((GQA_TILES * A_KV_HEADS, A_KV_W, ncol), BF16),
                        pltpu.VMEM((2, s, ncol), F32)],
        compiler_params=_params("parallel", "parallel"),
        name="gqa",
    )(qat, ka, vat)


def _swa_kernel(qkv_ref, o_ref, lse_ref, p_ref, top_ref, bias_ref, *, radius):
    dil, seq = qkv_ref.shape[0], qkv_ref.shape[1]
    nblk = seq // BLK
    halves = B_OUT_W // LANES
    lane = lax.broadcasted_iota(jnp.int32, (BLK, LANES), 1)
    first_head = lane < HEAD_DIM
    head_mask = [jnp.where(first_head, 1.0, 0.0).astype(BF16), jnp.where(first_head, 0.0, 1.0).astype(BF16)]
    ones_win = jnp.ones((WIN, LANES), BF16)

    rel = (lax.broadcasted_iota(jnp.int32, (BLK, WIN), 0)
           - lax.broadcasted_iota(jnp.int32, (BLK, WIN), 1))
    for n in range(bias_ref.shape[0]):
        bias_ref[n] = jnp.where(jnp.abs(rel + n * radius) <= radius, 0.0, NEG_INF)

    def geometry(i):
        r = i // nblk
        i0 = pl.multiple_of((i % nblk) * BLK, BLK)
        ws = pl.multiple_of(jnp.clip(i0 - radius, 0, seq - WIN), radius)
        return r, i0, ws

    def probabilities(i, slot):
        r, i0, ws = geometry(i)
        bias = bias_ref[(i0 - ws) // radius]
        for c in range(halves):
            q = qkv_ref[r, pl.ds(i0, BLK), c * LANES:(c + 1) * LANES]
            k = qkv_ref[r, pl.ds(ws, WIN), B_OUT_W + c * LANES:B_OUT_W + (c + 1) * LANES]
            tops = []
            for hh in range(2):
                sc = lax.dot_general(q * head_mask[hh], k, (((1,), (1,)), ((), ())),
                                     preferred_element_type=F32) + bias
                m = jnp.max(sc, axis=-1, keepdims=True)
                p_ref[slot, 2 * c + hh] = jnp.exp2((sc - m).astype(BF16))
                tops.append(m)
            top_ref[slot, c] = jnp.where(first_head, tops[0], tops[1])

    def outputs(i, slot):
        r, i0, ws = geometry(i)
        for c in range(halves):
            cols = slice(c * LANES, (c + 1) * LANES)
            v = qkv_ref[r, pl.ds(ws, WIN), 2 * B_OUT_W + c * LANES:2 * B_OUT_W + (c + 1) * LANES]
            v_ones = jnp.concatenate([v, ones_win], axis=1)
            outs = [jnp.dot(p_ref[slot, 2 * c + hh], v_ones, preferred_element_type=F32) for hh in range(2)]
            num = jnp.where(first_head, outs[0][:, :LANES], outs[1][:, :LANES])
            den = jnp.where(first_head, outs[0][:, LANES:], outs[1][:, LANES:])
            o_ref[r, pl.ds(i0, BLK), cols] = (num / den).astype(BF16)
            lse_ref[r, pl.ds(i0, BLK), cols] = (top_ref[slot, c] + jnp.log2(den)) * LN2

    total = dil * nblk
    probabilities(0, 0)

    def body(i, _):
        outputs(i - 1, (i - 1) % 2)
        probabilities(i, i % 2)
        return 0

    lax.fori_loop(1, total, body, 0, unroll=4)
    outputs(total - 1, (total - 1) % 2)


def _banded_attention(qkv, group):
    window, dil = B_GROUPS[group]
    b, _, seq, width = qkv.shape
    radius = window // 2 // dil
    whole = lambda w: pl.BlockSpec((None, dil, seq, w), lambda bb: (bb, 0, 0, 0))
    return pl.pallas_call(
        functools.partial(_swa_kernel, radius=radius),
        grid=(b,),
        in_specs=[whole(width)],
        out_specs=(whole(B_OUT_W), whole(B_OUT_W)),
        out_shape=(jax.ShapeDtypeStruct((b, dil, seq, B_OUT_W), BF16),
                   jax.ShapeDtypeStruct((b, dil, seq, B_OUT_W), F32)),
        scratch_shapes=[pltpu.VMEM((2, B_HEADS_PER_GROUP, BLK, WIN), BF16),
                        pltpu.VMEM((2, B_OUT_W // LANES, BLK, LANES), F32),
                        pltpu.VMEM(((WIN - BLK) // radius + 1, BLK, WIN), F32)],
        compiler_params=_params("parallel"),
        name=f"swa{group}",
    )(qkv)


def _mix_kernel(x_ref, mod_ref, g1_ref, attn_ref, o0_ref, o1_ref, o2_ref, l0_ref, l1_ref, l2_ref,
                wg_ref, bg_ref, wpa_ref, wpb_ref, wo_ref, out_ref, nat_ref):
    x = x_ref[...]
    d = x.shape[-1]
    u = _modulated_norm(x, g1_ref[...], mod_ref[0:1, :], mod_ref[1:2, :])
    gates = jax.nn.sigmoid(jnp.dot(u, wg_ref[...], preferred_element_type=F32) + bg_ref[...])
    ya = jnp.dot(attn_ref[...], wpa_ref[...], preferred_element_type=F32)

    halves = B_OUT_W // LANES
    slot = 0
    natural = []
    for src_refs in ((o0_ref, l0_ref), (o1_ref, l1_ref), (o2_ref, l2_ref)):
        per_kind = []
        for src in src_refs:
            dil, n = src.shape[0], src.shape[1]
            if dil == 1:
                per_kind.append([src[0, :, c * LANES:(c + 1) * LANES].astype(F32) for c in range(halves)])
                continue
            for r in range(dil):
                for c in range(halves):
                    nat_ref[slot + c, pl.ds(r, n, stride=dil), :] = (
                        src[r, :, c * LANES:(c + 1) * LANES].astype(F32))
            per_kind.append([nat_ref[slot + c] for c in range(halves)])
            slot += halves
        natural.append(per_kind)

    comb = []
    for c in range(halves):
        lses = [grp[1][c] for grp in natural]
        top = jnp.maximum(jnp.maximum(lses[0], lses[1]), lses[2])
        es = [jnp.exp(l - top) for l in lses]
        den = es[0] + es[1] + es[2]
        comb.append(sum((e / den) * grp[0][c] for e, grp in zip(es, natural)).astype(BF16))
    yb = jnp.dot(jnp.concatenate(comb, axis=1), wpb_ref[...], preferred_element_type=F32)
    merged = (gates[:, :d] * ya + gates[:, d:] * yb).astype(BF16)
    mix = jnp.dot(merged, wo_ref[...], preferred_element_type=F32)
    out_ref[...] = x + mod_ref[2:3, :] * mix


def _mix(x, mod3, g1, attn, os_, lses, wg, bg, wpa, wpb, wo):
    b, s, d = x.shape
    tm = TM_MIX
    tok = lambda w: pl.BlockSpec((None, tm, w), lambda bb, i: (bb, i, 0))
    const2 = lambda shape: pl.BlockSpec(shape, lambda bb, i: (0, 0))
    sub = [pl.BlockSpec((None, dil, tm // dil, B_OUT_W), lambda bb, i: (bb, 0, i, 0)) for _, dil in B_GROUPS]
    n_strided = sum(dil > 1 for _, dil in B_GROUPS) * 2 * (B_OUT_W // LANES)
    return pl.pallas_call(
        _mix_kernel,
        grid=(b, s // tm),
        in_specs=[tok(d),
                  pl.BlockSpec((None, 6, d), lambda bb, i: (bb, 0, 0)),
                  const2((1, d)),
                  tok(A_Q_W),
                  *sub, *sub,
                  const2(wg.shape), const2(bg.shape), const2(wpa.shape), const2(wpb.shape),
                  const2(wo.shape)],
        out_specs=tok(d),
        out_shape=jax.ShapeDtypeStruct((b, s, d), F32),
        scratch_shapes=[pltpu.VMEM((n_strided, tm, LANES), F32)],
        compiler_params=_params("parallel", "parallel"),
        name="mix",
    )(x, mod3, g1, attn, *os_, *lses, wg, bg, wpa, wpb, wo)


def _ffn_kernel(x_ref, mod_ref, g2_ref, win_ref, wout_ref, fg_ref, out_ref, *, final, bounds):
    x = x_ref[...]
    d_ff = wout_ref.shape[0]
    u = _modulated_norm(x, g2_ref[...], mod_ref[3:4, :], mod_ref[4:5, :])
    acc = jnp.zeros(x.shape, F32)
    for lo, hi in zip(bounds[:-1], bounds[1:]):
        hg = jnp.dot(u, win_ref[:, lo:hi], preferred_element_type=F32)
        hu = jnp.dot(u, win_ref[:, d_ff + lo:d_ff + hi], preferred_element_type=F32)
        act = ((hg * jax.nn.sigmoid(hg)) * hu).astype(BF16)
        acc += jnp.dot(act, wout_ref[lo:hi, :], preferred_element_type=F32)
    y = x + mod_ref[5:6, :] * acc
    if final:
        ms = jnp.mean(y * y, axis=-1, keepdims=True)
        y = (y * lax.rsqrt(ms + EPS)) * fg_ref[...]
    out_ref[...] = y


def _ffn(x, mod3, g2, win, wout, fg, final):
    b, s, d = x.shape
    tm = TM_FFN
    d_ff = wout.shape[0]
    tiles = pl.cdiv(d_ff, MXU_DIM)
    bounds = (0, min(d_ff, pl.cdiv(tiles, 2) * MXU_DIM), d_ff)
    tok = pl.BlockSpec((None, tm, d), lambda bb, i: (bb, i, 0))
    resident = lambda shape: pl.BlockSpec(shape, lambda bb, i: (0, 0), pipeline_mode=pl.Buffered(1))
    return pl.pallas_call(
        functools.partial(_ffn_kernel, final=final, bounds=bounds),
        grid=(b, s // tm),
        in_specs=[tok,
                  pl.BlockSpec((None, 6, d), lambda bb, i: (bb, 0, 0)),
                  pl.BlockSpec((1, d), lambda bb, i: (0, 0)),
                  resident(win.shape), resident(wout.shape),
                  pl.BlockSpec((1, d), lambda bb, i: (0, 0))],
        out_specs=tok,
        out_shape=jax.ShapeDtypeStruct((b, s, d), F32),
        compiler_params=_params("parallel", "parallel"),
        name="ffn",
    )(x, mod3, g2, win, wout, fg)


def _angles(pos, dim, theta):
    inv = theta ** (-jnp.arange(0, dim, 2, dtype=F32) / dim)
    return pos.astype(F32)[:, None] * inv[None, :]


def _rope_tables(s):
    t = jnp.arange(s, dtype=jnp.int32)
    half = HEAD_DIM // 2
    ar = _angles(t // GRID_W, half, AXIAL_THETA)
    ac = _angles(t % GRID_W, half, AXIAL_THETA)
    zr = jnp.zeros_like(ar)
    a_cos = jnp.concatenate([jnp.cos(ar), jnp.cos(ar), jnp.cos(ac), jnp.cos(ac)], axis=1)
    a_lo = jnp.concatenate([-jnp.sin(ar), zr, -jnp.sin(ac), zr], axis=1)
    a_hi = jnp.concatenate([zr, jnp.sin(ar), zr, jnp.sin(ac)], axis=1)
    ap = _angles(t, PARTIAL_ROT_DIM, PARTIAL_THETA)
    rest = HEAD_DIM - PARTIAL_ROT_DIM
    zp = jnp.zeros_like(ap)
    p_cos = jnp.concatenate([jnp.cos(ap), jnp.cos(ap), jnp.ones((s, rest), F32)], axis=1)
    p_lo = jnp.concatenate([-jnp.sin(ap), zp, jnp.zeros((s, rest), F32)], axis=1)
    p_hi = jnp.concatenate([zp, jnp.sin(ap), jnp.zeros((s, rest), F32)], axis=1)
    two = lambda a: jnp.tile(a, (1, LANES // HEAD_DIM))
    return tuple(two(a) for a in (a_cos, a_lo, a_hi, p_cos, p_lo, p_hi))


def kernel(x, c, w_ada, b_ada, norm1_g, w_qkv, q_norm_a, k_norm_a, w_proj_a, w_proj_b, w_gate,
           b_gate, w_o, norm2_g, w_ffn_in, w_ffn_out, final_norm_g):
    b, s, d = x.shape
    depth = w_ada.shape[0]
    assert s % TM_QKV == 0 and s % (B_GROUPS[-1][1] * WIN) == 0 and TK % TM_QKV == 0 and s % TK == 0
    tabs = _rope_tables(s)
    heads_per_vreg = LANES // HEAD_DIM
    lane_head = jnp.arange(LANES) // HEAD_DIM
    bd = (lane_head[:, None] == lane_head[None, :]).astype(BF16)
    b_q_lo = A_Q_W + 2 * A_KV_W
    col_scale = jnp.ones((w_qkv.shape[-1],), F32).at[b_q_lo:b_q_lo + B_W].set(SCALE * LOG2E)
    c_pad = jnp.pad(c, ((0, 16 - b), (0, 0)))
    for l in range(depth):
        mod3 = _modulation(c_pad, w_ada[l], b_ada[l])[:b].reshape(b, 6, d)
        qg = jnp.tile(q_norm_a[l] * (SCALE * LOG2E), heads_per_vreg).reshape(1, LANES)
        kg = jnp.tile(k_norm_a[l], heads_per_vreg).reshape(1, LANES)
        g1 = norm1_g[l].reshape(1, d)
        wq = (w_qkv[l] * col_scale[None, :]).astype(BF16)
        qat, ka, vat, *groups = _qkv_project(x, mod3, g1, wq, qg, kg, bd, tabs)
        attn = _global_attention(qat, ka, vat)
        branch = [_banded_attention(qkv_g, g) for g, qkv_g in enumerate(groups)]
        x = _mix(x, mod3, g1, attn, [o for o, _ in branch], [e for _, e in branch],
                 w_gate[l].astype(BF16), b_gate[l].reshape(1, -1), w_proj_a[l].astype(BF16),
                 w_proj_b[l].astype(BF16), w_o[l].astype(BF16))
        x = _ffn(x, mod3, norm2_g[l].reshape(1, d), w_ffn_in[l].astype(BF16),
                 w_ffn_out[l].astype(BF16), final_norm_g.reshape(1, d), final=(l == depth - 1))
    return x
```

```python
import functools

import jax
import jax.numpy as jnp
import numpy as np
from jax import lax
from jax.experimental import pallas as pl
from jax.experimental.pallas import tpu as pltpu

HEAD_DIM = 64
A_Q_HEADS = 8
A_KV_HEADS = 2
A_GROUP = A_Q_HEADS // A_KV_HEADS
B_GROUPS = ((128, 1), (512, 4), (2048, 16))
B_HEADS_PER_GROUP = 4
B_HEADS = B_HEADS_PER_GROUP * len(B_GROUPS)
A_Q_W = A_Q_HEADS * HEAD_DIM
A_KV_W = A_KV_HEADS * HEAD_DIM
B_W = B_HEADS * HEAD_DIM
B_OUT_W = B_HEADS_PER_GROUP * HEAD_DIM
GRID_W = 64
AXIAL_THETA = 10000.0
PARTIAL_THETA = 500000.0
PARTIAL_ROT_DIM = HEAD_DIM // 4
EPS = 1e-6
NEG_INF = -1e30
SCALE = HEAD_DIM ** -0.5
LOG2E = 1.4426950408889634
LN2 = 0.6931471805599453

LANES = 128
MXU_DIM = 256
VMEM_LIMIT = 56 * 1024 * 1024

F32 = jnp.float32
BF16 = jnp.bfloat16

TM_QKV = 512
QKV_ROW_SPLITS = 1
TQ = 128
TK = 1024
TM_MIX = 512
TM_FFN = 512
BLK = 128
WIN = 256
VT_ROWS = HEAD_DIM + 16
GQA_UNROLL = 2
GQA_TILES = 4


def _params(*sem):
    return pltpu.CompilerParams(dimension_semantics=sem, vmem_limit_bytes=VMEM_LIMIT)


def _mod_kernel(c_ref, w_ref, b_ref, o_ref):
    c = c_ref[...]
    cond = c * jax.nn.sigmoid(c)
    c_hi = cond.astype(BF16)
    c_lo = (cond - c_hi.astype(F32)).astype(BF16)
    w = w_ref[...]
    w_hi = w.astype(BF16)
    w_lo = (w - w_hi.astype(F32)).astype(BF16)
    acc = jnp.dot(c_hi, w_hi, preferred_element_type=F32)
    acc += jnp.dot(c_hi, w_lo, preferred_element_type=F32)
    acc += jnp.dot(c_lo, w_hi, preferred_element_type=F32)
    o_ref[...] = acc + b_ref[...]


def _modulation(c_pad, w_ada, b_ada, layer):
    rows, d = c_pad.shape
    n = w_ada.shape[2]
    tn = 1024
    return pl.pallas_call(
        _mod_kernel,
        grid=(n // tn,),
        in_specs=[pl.BlockSpec((rows, d), lambda j: (0, 0)),
                  pl.BlockSpec((None, d, tn), lambda j: (layer, 0, j)),
                  pl.BlockSpec((1, tn), lambda j: (0, j))],
        out_specs=pl.BlockSpec((rows, tn), lambda j: (0, j)),
        out_shape=jax.ShapeDtypeStruct((rows, n), F32),
        compiler_params=_params("parallel"),
        name="mod",
    )(c_pad, w_ada, b_ada.reshape(1, n))


def _modulated_norm(x, gain, shift, scale):
    ms = jnp.mean(x * x, axis=-1, keepdims=True)
    return ((x * lax.rsqrt(ms + EPS)) * (gain * (1.0 + scale)) + shift).astype(BF16)


def _rope(y, cos, sin_lo, sin_hi, shift):
    return (y * cos + pltpu.roll(y, LANES - shift, 1) * sin_lo
            + pltpu.roll(y, shift, 1) * sin_hi)


def _qkv_kernel(x_ref, mod_ref, g1_ref, w_ref, qg_ref, kg_ref, bd_ref,
                ac_ref, asl_ref, ash_ref, pc_ref, psl_ref, psh_ref,
                qat_ref, ka_ref, vat_ref, grp0_ref, grp1_ref, grp2_ref, slab_ref):
    tm = x_ref.shape[0]
    th = tm // QKV_ROW_SPLITS
    for h in range(QKV_ROW_SPLITS):
        rows = slice(h * th, (h + 1) * th)
        u = _modulated_norm(x_ref[rows, :], g1_ref[...], mod_ref[0:1, :], mod_ref[1:2, :])

        def proj(lo, width):
            return jnp.dot(u, w_ref[:, lo:lo + width], preferred_element_type=F32)

        def head_norm_rope(wide, gain):
            out = []
            for lo in range(0, wide.shape[1], MXU_DIM):
                tile = wide[:, lo:lo + MXU_DIM]
                w = tile.shape[1]
                ss = jnp.dot((tile * tile).astype(BF16), bd_ref[:w, :w], preferred_element_type=F32)
                for c in range(w // LANES):
                    cols = slice(c * LANES, (c + 1) * LANES)
                    y = (tile[:, cols] * lax.rsqrt(ss[:, cols] * (1.0 / HEAD_DIM) + EPS)) * gain
                    out.append(_rope(y, ac_ref[rows, :], asl_ref[rows, :], ash_ref[rows, :], HEAD_DIM // 4))
            return out[0] if len(out) == 1 else jnp.concatenate(out, axis=1)

        def branch_b(g):
            out_ref, dil = (grp0_ref, grp1_ref, grp2_ref)[g], B_GROUPS[g][1]
            base = A_Q_W + 2 * A_KV_W
            halves = B_OUT_W // LANES
            for part in range(3):
                wide = proj(base + part * B_W + g * B_OUT_W, B_OUT_W)
                for c in range(halves):
                    chunk = wide[:, c * LANES:(c + 1) * LANES]
                    if part < 2:
                        chunk = _rope(chunk, pc_ref[rows, :], psl_ref[rows, :], psh_ref[rows, :],
                                      PARTIAL_ROT_DIM // 2)
                    slab = part * halves + c
                    if dil == 1:
                        out_ref[0, rows, slab * LANES:(slab + 1) * LANES] = chunk.astype(BF16)
                    else:
                        slab_ref[g - 1, slab, rows, :] = chunk
            if dil > 1:
                n = th // dil
                for r in range(dil):
                    for slab in range(3 * halves):
                        out_ref[r, h * n:(h + 1) * n, slab * LANES:(slab + 1) * LANES] = (
                            slab_ref[g - 1, slab, pl.ds(h * th + r, n, stride=dil), :].astype(BF16))

        qa = head_norm_rope(proj(0, A_Q_W), qg_ref[...])
        qat = qa.T.astype(BF16)
        for i in range(th // TQ):
            qat_ref[h * (th // TQ) + i] = qat[:, i * TQ:(i + 1) * TQ]

        kva = proj(A_Q_W, 2 * A_KV_W)
        ka_ref[rows, :] = head_norm_rope(kva[:, :A_KV_W], kg_ref[...]).astype(BF16)

        vat = kva[:, A_KV_W:].T.astype(BF16)
        pad_row = lax.broadcasted_iota(jnp.int32, (VT_ROWS - HEAD_DIM, th), 0)
        ones_row = jnp.where(pad_row == 0, 1.0, 0.0).astype(BF16)
        for hk in range(A_KV_HEADS):
            vat_ref[hk, 0:HEAD_DIM, rows] = vat[hk * HEAD_DIM:(hk + 1) * HEAD_DIM, :]
            vat_ref[hk, HEAD_DIM:VT_ROWS, rows] = ones_row

        for g in (2, 1, 0):
            branch_b(g)


def _qkv_project(x, mod3, g1, w_qkv, qg, kg, bd, tabs):
    b, s, d = x.shape
    tm = TM_QKV
    ncols = w_qkv.shape[1]
    row_tab = pl.BlockSpec((tm, LANES), lambda i, bb: (i, 0))
    const2 = lambda shape: pl.BlockSpec(shape, lambda i, bb: (0, 0))
    tok = lambda w: pl.BlockSpec((None, tm, w), lambda i, bb: (bb, i, 0))
    out_shapes = (
        jax.ShapeDtypeStruct((b, s // TQ, A_Q_W, TQ), BF16),
        jax.ShapeDtypeStruct((b, s, A_KV_W), BF16),
        jax.ShapeDtypeStruct((b, s // TK, A_KV_HEADS, VT_ROWS, TK), BF16),
    ) + tuple(jax.ShapeDtypeStruct((b, dil, s // dil, 3 * B_OUT_W), BF16) for _, dil in B_GROUPS)
    out_specs = (
        pl.BlockSpec((None, tm // TQ, A_Q_W, TQ), lambda i, bb: (bb, i, 0, 0)),
        tok(A_KV_W),
        pl.BlockSpec((None, None, A_KV_HEADS, VT_ROWS, tm),
                     lambda i, bb: (bb, i // (TK // tm), 0, 0, i % (TK // tm))),
    ) + tuple(pl.BlockSpec((None, dil, tm // dil, 3 * B_OUT_W), lambda i, bb: (bb, 0, i, 0))
              for _, dil in B_GROUPS)
    return pl.pallas_call(
        _qkv_kernel,
        grid=(s // tm, b),
        in_specs=[tok(d),
                  pl.BlockSpec((None, 6, d), lambda i, bb: (bb, 0, 0)),
                  const2((1, d)),
                  const2((d, ncols)),
                  const2((1, LANES)), const2((1, LANES)), const2((MXU_DIM, MXU_DIM)),
                  row_tab, row_tab, row_tab, row_tab, row_tab, row_tab],
        out_specs=out_specs,
        out_shape=out_shapes,
        scratch_shapes=[pltpu.VMEM((len(B_GROUPS) - 1, 3 * B_OUT_W // LANES, tm, LANES), F32)],
        compiler_params=_params("parallel", "parallel"),
        name="qkv",
    )(x, mod3, g1, w_qkv, qg, kg, bd, *tabs)


def _gqa_kernel(qt_ref, k_ref, vt_ref, o_ref, qx_ref, s_ref):
    nk = vt_ref.shape[0]
    ncol = A_GROUP * TQ
    units = [(t, hk) for t in range(GQA_TILES) for hk in range(A_KV_HEADS)]

    qx_ref[...] = jnp.zeros(qx_ref.shape, BF16)
    for u, (t, hk) in enumerate(units):
        for g in range(A_GROUP):
            h = hk * A_GROUP + g
            qx_ref[u, hk * HEAD_DIM:(hk + 1) * HEAD_DIM, g * TQ:(g + 1) * TQ] = (
                qt_ref[t, h * HEAD_DIM:(h + 1) * HEAD_DIM, :])

    def phase(score_unit, value_unit, m_prev):
        qx = None if score_unit is None else qx_ref[score_unit]

        def body(j, carry):
            mrun, acc = carry
            rows = pl.ds(pl.multiple_of(j * TK, TK), TK)
            if score_unit is not None:
                st = jnp.dot(k_ref[rows, :], qx, preferred_element_type=F32)
                s_ref[score_unit % 2, rows, :] = st
                mrun = jnp.maximum(mrun, jnp.max(st.reshape(TK // 8, 8, ncol), axis=0))
            if value_unit is not None:
                p = jnp.exp2((s_ref[value_unit % 2, rows, :] - m_prev).astype(BF16))
                acc = acc + jnp.dot(vt_ref[j, units[value_unit][1]], p, preferred_element_type=F32)
            return mrun, acc

        init = (jnp.full((8, ncol), NEG_INF, F32), jnp.zeros((VT_ROWS, ncol), F32))
        mrun, acc = lax.fori_loop(0, nk, body, init, unroll=GQA_UNROLL)
        return jnp.max(mrun, axis=0, keepdims=True), acc

    m = None
    for idx in range(len(units) + 1):
        score_unit = idx if idx < len(units) else None
        value_unit = idx - 1 if idx > 0 else None
        m, acc = phase(score_unit, value_unit, m)
        if value_unit is not None:
            t, hk = units[value_unit]
            out_t = acc[0:HEAD_DIM, :] / acc[HEAD_DIM:HEAD_DIM + 1, :]
            stacked = jnp.concatenate([out_t[:, g * TQ:(g + 1) * TQ] for g in range(A_GROUP)], axis=0)
            o_ref[t * TQ:(t + 1) * TQ, hk * A_GROUP * HEAD_DIM:(hk + 1) * A_GROUP * HEAD_DIM] = (
                stacked.T.astype(BF16))


def _global_attention(qat, ka, vat):
    b, nq = qat.shape[0], qat.shape[1]
    s = ka.shape[1]
    nk = vat.shape[1]
    ncol = A_GROUP * TQ
    return pl.pallas_call(
        _gqa_kernel,
        grid=(b, nq // GQA_TILES),
        in_specs=[pl.BlockSpec((None, GQA_TILES, A_Q_W, TQ), lambda bb, i: (bb, i, 0, 0)),
                  pl.BlockSpec((None, s, A_KV_W), lambda bb, i: (bb, 0, 0)),
                  pl.BlockSpec((None, nk, A_KV_HEADS, VT_ROWS, TK), lambda bb, i: (bb, 0, 0, 0, 0))],
        out_specs=pl.BlockSpec((None, GQA_TILES * TQ, A_Q_W), lambda bb, i: (bb, i, 0)),
        out_shape=jax.ShapeDtypeStruct((b, s, A_Q_W), BF16),
        scratch_shapes=[pltpu.VMEM((GQA_TILES * A_KV_HEADS, A_KV_W, ncol), BF16),
                        pltpu.VMEM((2, s, ncol), F32)],
        compiler_params=_params("parallel", "parallel"),
        name="gqa",
    )(qat, ka, vat)


def _swa_kernel(qkv_ref, o_ref, lse_ref, sc_ref, p_ref, top_ref, bias_ref, *, radius):
    dil, seq = qkv_ref.shape[0], qkv_ref.shape[1]
    nblk = seq // BLK
    halves = B_OUT_W // LANES
    lane = lax.broadcasted_iota(jnp.int32, (BLK, LANES), 1)
    first_head = lane < HEAD_DIM
    head_mask = [jnp.where(first_head, 1.0, 0.0).astype(BF16), jnp.where(first_head, 0.0, 1.0).astype(BF16)]
    ones_win = jnp.ones((WIN, LANES), BF16)

    rel = (lax.broadcasted_iota(jnp.int32, (BLK, WIN), 0)
           - lax.broadcasted_iota(jnp.int32, (BLK, WIN), 1))
    for n in range(bias_ref.shape[0]):
        bias_ref[n] = jnp.where(jnp.abs(rel + n * radius) <= radius, 0.0, NEG_INF)

    def geometry(i):
        r = i // nblk
        i0 = pl.multiple_of((i % nblk) * BLK, BLK)
        ws = pl.multiple_of(jnp.clip(i0 - radius, 0, seq - WIN), radius)
        return r, i0, ws

    def scores(i, slot):
        r, i0, ws = geometry(i)
        bias = bias_ref[(i0 - ws) // radius]
        for c in range(halves):
            q = qkv_ref[r, pl.ds(i0, BLK), c * LANES:(c + 1) * LANES]
            k = qkv_ref[r, pl.ds(ws, WIN), B_OUT_W + c * LANES:B_OUT_W + (c + 1) * LANES]
            for hh in range(2):
                sc_ref[slot, 2 * c + hh] = lax.dot_general(
                    q * head_mask[hh], k, (((1,), (1,)), ((), ())), preferred_element_type=F32) + bias

    def probabilities(slot):
        for c in range(halves):
            tops = []
            for hh in range(2):
                sc = sc_ref[slot, 2 * c + hh]
                m = jnp.max(sc, axis=-1, keepdims=True)
                p_ref[slot, 2 * c + hh] = jnp.exp2((sc - m).astype(BF16))
                tops.append(m)
            top_ref[slot, c] = jnp.where(first_head, tops[0], tops[1])

    def outputs(i, slot):
        r, i0, ws = geometry(i)
        for c in range(halves):
            v = qkv_ref[r, pl.ds(ws, WIN), 2 * B_OUT_W + c * LANES:2 * B_OUT_W + (c + 1) * LANES]
            v_ones = jnp.concatenate([v, ones_win], axis=1)
            outs = [jnp.dot(p_ref[slot, 2 * c + hh], v_ones, preferred_element_type=F32) for hh in range(2)]
            num = jnp.where(first_head, outs[0][:, :LANES], outs[1][:, :LANES])
            den = jnp.where(first_head, outs[0][:, LANES:], outs[1][:, LANES:])
            tokens = pl.ds(i0 * dil + r, BLK, stride=dil) if dil > 1 else pl.ds(i0, BLK)
            o_ref[c, tokens, :] = num / den
            lse_ref[c, tokens, :] = (top_ref[slot, c] + jnp.log2(den)) * LN2

    total = dil * nblk
    scores(0, 0)
    probabilities(0)
    scores(1, 1)

    def body(i, _):
        outputs(i - 2, i % 2)
        probabilities((i - 1) % 2)
        scores(i, i % 2)
        return 0

    lax.fori_loop(2, total, body, 0, unroll=2)
    outputs(total - 2, total % 2)
    probabilities((total - 1) % 2)
    outputs(total - 1, (total - 1) % 2)


def _banded_attention(qkv, group):
    window, dil = B_GROUPS[group]
    b, _, seq, width = qkv.shape
    radius = window // 2 // dil
    halves = B_OUT_W // LANES
    out_spec = pl.BlockSpec((None, halves, dil * seq, LANES), lambda bb: (bb, 0, 0, 0))
    out_shape = jax.ShapeDtypeStruct((b, halves, dil * seq, LANES), F32)
    return pl.pallas_call(
        functools.partial(_swa_kernel, radius=radius),
        grid=(b,),
        in_specs=[pl.BlockSpec((None, dil, seq, width), lambda bb: (bb, 0, 0, 0))],
        out_specs=(out_spec, out_spec),
        out_shape=(out_shape, out_shape),
        scratch_shapes=[pltpu.VMEM((2, B_HEADS_PER_GROUP, BLK, WIN), F32),
                        pltpu.VMEM((2, B_HEADS_PER_GROUP, BLK, WIN), BF16),
                        pltpu.VMEM((2, B_OUT_W // LANES, BLK, LANES), F32),
                        pltpu.VMEM(((WIN - BLK) // radius + 1, BLK, WIN), F32)],
        compiler_params=_params("parallel"),
        name=f"swa{group}",
    )(qkv)


def _mix_kernel(x_ref, mod_ref, g1_ref, attn_ref, o0_ref, o1_ref, o2_ref, l0_ref, l1_ref, l2_ref,
                wg_ref, bg_ref, wpa_ref, wpb_ref, wo_ref, out_ref):
    x = x_ref[...]
    d = x.shape[-1]
    halves = B_OUT_W // LANES
    ya = jnp.dot(attn_ref[...], wpa_ref[...], preferred_element_type=F32)
    u = _modulated_norm(x, g1_ref[...], mod_ref[0:1, :], mod_ref[1:2, :])

    def combine(c):
        lses = [l_ref[c] for l_ref in (l0_ref, l1_ref, l2_ref)]
        top = jnp.maximum(jnp.maximum(lses[0], lses[1]), lses[2])
        es = [jnp.exp(l - top) for l in lses]
        den = es[0] + es[1] + es[2]
        return sum((e / den) * o_ref[c] for e, o_ref in zip(es, (o0_ref, o1_ref, o2_ref))).astype(BF16)

    def gate_logits(j, parts=4):
        w = 2 * d // parts
        return jnp.dot(u, wg_ref[:, j * w:(j + 1) * w], preferred_element_type=F32) + bg_ref[:, j * w:(j + 1) * w]

    logits, comb = [], []
    for j in range(4):
        logits.append(gate_logits(j))
        if j < halves:
            comb.append(combine(j))
    yb = jnp.dot(jnp.concatenate(comb, axis=1), wpb_ref[...], preferred_element_type=F32)
    gate_a = jax.nn.sigmoid(jnp.concatenate(logits[:2], axis=1))
    gate_b = jax.nn.sigmoid(jnp.concatenate(logits[2:], axis=1))
    merged = (gate_a * ya + gate_b * yb).astype(BF16)
    mix = jnp.dot(merged, wo_ref[...], preferred_element_type=F32)
    out_ref[...] = x + mod_ref[2:3, :] * mix


def _mix(x, mod3, g1, attn, os_, lses, wg, bg, wpa, wpb, wo):
    b, s, d = x.shape
    tm = TM_MIX
    tok = lambda w: pl.BlockSpec((None, tm, w), lambda bb, i: (bb, i, 0))
    const2 = lambda shape: pl.BlockSpec(shape, lambda bb, i: (0, 0))
    sub = [pl.BlockSpec((None, B_OUT_W // LANES, tm, LANES), lambda bb, i: (bb, 0, i, 0))] * len(B_GROUPS)
    return pl.pallas_call(
        _mix_kernel,
        grid=(b, s // tm),
        in_specs=[tok(d),
                  pl.BlockSpec((None, 6, d), lambda bb, i: (bb, 0, 0)),
                  const2((1, d)),
                  tok(A_Q_W),
                  *sub, *sub,
                  const2(wg.shape), const2(bg.shape), const2(wpa.shape), const2(wpb.shape),
                  const2(wo.shape)],
        out_specs=tok(d),
        out_shape=jax.ShapeDtypeStruct((b, s, d), F32),
        compiler_params=_params("parallel", "parallel"),
        name="mix",
    )(x, mod3, g1, attn, *os_, *lses, wg, bg, wpa, wpb, wo)


def _ffn_kernel(x_ref, mod_ref, g2_ref, win_ref, wout_ref, fg_ref, out_ref, *, final, bounds):
    x = x_ref[...]
    d_ff = wout_ref.shape[0]
    u = _modulated_norm(x, g2_ref[...], mod_ref[3:4, :], mod_ref[4:5, :])
    acc = jnp.zeros(x.shape, F32)
    for lo, hi in zip(bounds[:-1], bounds[1:]):
        hg = jnp.dot(u, win_ref[:, lo:hi], preferred_element_type=F32)
        hu = jnp.dot(u, win_ref[:, d_ff + lo:d_ff + hi], preferred_element_type=F32)
        act = ((hg * jax.nn.sigmoid(hg)) * hu).astype(BF16)
        acc += jnp.dot(act, wout_ref[lo:hi, :], preferred_element_type=F32)
    y = x + mod_ref[5:6, :] * acc
    if final:
        ms = jnp.mean(y * y, axis=-1, keepdims=True)
        y = (y * lax.rsqrt(ms + EPS)) * fg_ref[...]
    out_ref[...] = y


def _ffn(x, mod3, g2, win, wout, fg, final):
    b, s, d = x.shape
    tm = TM_FFN
    d_ff = wout.shape[0]
    tiles = pl.cdiv(d_ff, MXU_DIM)
    bounds = (0, min(d_ff, pl.cdiv(tiles, 2) * MXU_DIM), d_ff)
    tok = pl.BlockSpec((None, tm, d), lambda bb, i: (bb, i, 0))
    resident = lambda shape: pl.BlockSpec(shape, lambda bb, i: (0, 0), pipeline_mode=pl.Buffered(1))
    return pl.pallas_call(
        functools.partial(_ffn_kernel, final=final, bounds=bounds),
        grid=(b, s // tm),
        in_specs=[tok,
                  pl.BlockSpec((None, 6, d), lambda bb, i: (bb, 0, 0)),
                  pl.BlockSpec((1, d), lambda bb, i: (0, 0)),
                  resident(win.shape), resident(wout.shape),
                  pl.BlockSpec((1, d), lambda bb, i: (0, 0))],
        out_specs=tok,
        out_shape=jax.ShapeDtypeStruct((b, s, d), F32),
        compiler_params=_params("parallel", "parallel"),
        name="ffn",
    )(x, mod3, g2, win, wout, fg)


def _inv_freq(dim, theta):
    return theta ** (-jnp.arange(0, dim, 2, dtype=F32) / dim)


def _rope_tables(s):
    in_head = np.arange(LANES) % HEAD_DIM
    half = HEAD_DIM // 2
    quarter = half // 2
    inv_a = _inv_freq(half, AXIAL_THETA)[in_head % quarter][None, :]
    ang_row = jnp.arange(s // GRID_W, dtype=jnp.int32).astype(F32)[:, None] * inv_a
    ang_col = jnp.arange(GRID_W, dtype=jnp.int32).astype(F32)[:, None] * inv_a
    by_row = (in_head < half)[None, None, :]
    first_a = ((in_head % half) < quarter)[None, None, :]

    def grid(fn):
        return jnp.where(by_row, fn(ang_row)[:, None, :], fn(ang_col)[None, :, :])

    a_cos = grid(jnp.cos).reshape(s, LANES)
    a_lo = jnp.where(first_a, -grid(jnp.sin), 0.0).reshape(s, LANES)
    a_hi = jnp.where(first_a, 0.0, grid(jnp.sin)).reshape(s, LANES)
    t = jnp.arange(s, dtype=jnp.int32)
    ap = t.astype(F32)[:, None] * _inv_freq(PARTIAL_ROT_DIM, PARTIAL_THETA)[None, :]
    rest = HEAD_DIM - PARTIAL_ROT_DIM
    zp = jnp.zeros_like(ap)
    p_cos = jnp.concatenate([jnp.cos(ap), jnp.cos(ap), jnp.ones((s, rest), F32)], axis=1)
    p_lo = jnp.concatenate([-jnp.sin(ap), zp, jnp.zeros((s, rest), F32)], axis=1)
    p_hi = jnp.concatenate([zp, jnp.sin(ap), jnp.zeros((s, rest), F32)], axis=1)
    two = lambda a: jnp.tile(a, (1, LANES // HEAD_DIM))
    return a_cos, a_lo, a_hi, two(p_cos), two(p_lo), two(p_hi)


def kernel(x, c, w_ada, b_ada, norm1_g, w_qkv, q_norm_a, k_norm_a, w_proj_a, w_proj_b, w_gate,
           b_gate, w_o, norm2_g, w_ffn_in, w_ffn_out, final_norm_g):
    b, s, d = x.shape
    depth = w_ada.shape[0]
    assert s % TM_QKV == 0 and s % (B_GROUPS[-1][1] * WIN) == 0 and TK % TM_QKV == 0 and s % TK == 0
    tabs = _rope_tables(s)
    heads_per_vreg = LANES // HEAD_DIM
    lane_head = np.arange(MXU_DIM) // HEAD_DIM
    bd = jnp.asarray(lane_head[:, None] == lane_head[None, :], BF16)
    b_q_lo = A_Q_W + 2 * A_KV_W
    col_scale = jnp.ones((w_qkv.shape[-1],), F32).at[b_q_lo:b_q_lo + B_W].set(SCALE * LOG2E)
    c_pad = jnp.pad(c, ((0, 16 - b), (0, 0)))
    for l in range(depth):
        mod3 = _modulation(c_pad, w_ada, b_ada[l], l)[:b].reshape(b, 6, d)
        qg = jnp.tile(q_norm_a[l] * (SCALE * LOG2E), heads_per_vreg).reshape(1, LANES)
        kg = jnp.tile(k_norm_a[l], heads_per_vreg).reshape(1, LANES)
        g1 = norm1_g[l].reshape(1, d)
        wq = (w_qkv[l] * col_scale[None, :]).astype(BF16)
        qat, ka, vat, *groups = _qkv_project(x, mod3, g1, wq, qg, kg, bd, tabs)
        attn = _global_attention(qat, ka, vat)
        branch = [_banded_attention(qkv_g, g) for g, qkv_g in enumerate(groups)]
        x = _mix(x, mod3, g1, attn, [o for o, _ in branch], [e for _, e in branch],
                 w_gate[l].astype(BF16), b_gate[l].reshape(1, -1), w_proj_a[l].astype(BF16),
                 w_proj_b[l].astype(BF16), w_o[l].astype(BF16))
        x = _ffn(x, mod3, norm2_g[l].reshape(1, d), w_ffn_in[l].astype(BF16),
                 w_ffn_out[l].astype(BF16), final_norm_g.reshape(1, d), final=(l == depth - 1))
    return x
```

```python
import functools

import jax
import jax.numpy as jnp
import numpy as np
from jax import lax
from jax.experimental import pallas as pl
from jax.experimental.pallas import tpu as pltpu

HEAD_DIM = 64
A_Q_HEADS = 8
A_KV_HEADS = 2
A_GROUP = A_Q_HEADS // A_KV_HEADS
B_GROUPS = ((128, 1), (512, 4), (2048, 16))
B_HEADS_PER_GROUP = 4
B_HEADS = B_HEADS_PER_GROUP * len(B_GROUPS)
A_Q_W = A_Q_HEADS * HEAD_DIM
A_KV_W = A_KV_HEADS * HEAD_DIM
B_W = B_HEADS * HEAD_DIM
B_OUT_W = B_HEADS_PER_GROUP * HEAD_DIM
GRID_W = 64
AXIAL_THETA = 10000.0
PARTIAL_THETA = 500000.0
PARTIAL_ROT_DIM = HEAD_DIM // 4
EPS = 1e-6
NEG_INF = -1e30
SCALE = HEAD_DIM ** -0.5
LOG2E = 1.4426950408889634
LN2 = 0.6931471805599453

LANES = 128
MXU_DIM = 256
VMEM_LIMIT = 56 * 1024 * 1024

F32 = jnp.float32
BF16 = jnp.bfloat16

TM_QKV = 512
QKV_ROW_SPLITS = 1
TQ = 128
TK = 1024
TM_MIX = 512
TM_FFN = 512
BLK = 128
WIN = 256
SWA_MAX_STORE_STRIDE = 4
VT_ROWS = HEAD_DIM + 16
GQA_UNROLL = 2
GQA_TILES = 4


def _params(*sem):
    return pltpu.CompilerParams(dimension_semantics=sem, vmem_limit_bytes=VMEM_LIMIT)


def _mod_kernel(c_ref, w_ref, b_ref, o_ref):
    c = c_ref[...]
    cond = c * jax.nn.sigmoid(c)
    c_hi = cond.astype(BF16)
    c_lo = (cond - c_hi.astype(F32)).astype(BF16)
    w = w_ref[...]
    w_hi = w.astype(BF16)
    w_lo = (w - w_hi.astype(F32)).astype(BF16)
    acc = jnp.dot(c_hi, w_hi, preferred_element_type=F32)
    acc += jnp.dot(c_hi, w_lo, preferred_element_type=F32)
    acc += jnp.dot(c_lo, w_hi, preferred_element_type=F32)
    o_ref[...] = acc + b_ref[...]


def _modulation(c_pad, w_ada, b_ada, layer):
    rows, d = c_pad.shape
    n = w_ada.shape[2]
    tn = 1024
    return pl.pallas_call(
        _mod_kernel,
        grid=(n // tn,),
        in_specs=[pl.BlockSpec((rows, d), lambda j: (0, 0)),
                  pl.BlockSpec((None, d, tn), lambda j: (layer, 0, j)),
                  pl.BlockSpec((1, tn), lambda j: (0, j))],
        out_specs=pl.BlockSpec((rows, tn), lambda j: (0, j)),
        out_shape=jax.ShapeDtypeStruct((rows, n), F32),
        compiler_params=_params("parallel"),
        name="mod",
    )(c_pad, w_ada, b_ada.reshape(1, n))


def _modulated_norm(x, gain, shift, scale):
    ms = jnp.mean(x * x, axis=-1, keepdims=True)
    return ((x * lax.rsqrt(ms + EPS)) * (gain * (1.0 + scale)) + shift).astype(BF16)


def _rope(y, cos, sin_lo, sin_hi, shift):
    return (y * cos + pltpu.roll(y, LANES - shift, 1) * sin_lo
            + pltpu.roll(y, shift, 1) * sin_hi)


def _qkv_kernel(x_ref, mod_ref, g1_ref, w_ref, qg_ref, kg_ref, bd_ref,
                ac_ref, asl_ref, ash_ref, pc_ref, psl_ref, psh_ref,
                qat_ref, ka_ref, vat_ref, grp0_ref, grp1_ref, grp2_ref, slab_ref):
    tm = x_ref.shape[0]
    th = tm // QKV_ROW_SPLITS
    for h in range(QKV_ROW_SPLITS):
        rows = slice(h * th, (h + 1) * th)
        u = _modulated_norm(x_ref[rows, :], g1_ref[...], mod_ref[0:1, :], mod_ref[1:2, :])

        def proj(lo, width):
            return jnp.dot(u, w_ref[:, lo:lo + width], preferred_element_type=F32)

        def head_norm_rope(wide, gain):
            out = []
            for lo in range(0, wide.shape[1], MXU_DIM):
                tile = wide[:, lo:lo + MXU_DIM]
                w = tile.shape[1]
                ss = jnp.dot((tile * tile).astype(BF16), bd_ref[:w, :w], preferred_element_type=F32)
                for c in range(w // LANES):
                    cols = slice(c * LANES, (c + 1) * LANES)
                    y = (tile[:, cols] * lax.rsqrt(ss[:, cols] * (1.0 / HEAD_DIM) + EPS)) * gain
                    out.append(_rope(y, ac_ref[rows, :], asl_ref[rows, :], ash_ref[rows, :], HEAD_DIM // 4))
            return out[0] if len(out) == 1 else jnp.concatenate(out, axis=1)

        def branch_b(g):
            out_ref, dil = (grp0_ref, grp1_ref, grp2_ref)[g], B_GROUPS[g][1]
            base = A_Q_W + 2 * A_KV_W
            halves = B_OUT_W // LANES
            for part in range(3):
                wide = proj(base + part * B_W + g * B_OUT_W, B_OUT_W)
                for c in range(halves):
                    chunk = wide[:, c * LANES:(c + 1) * LANES]
                    if part < 2:
                        chunk = _rope(chunk, pc_ref[rows, :], psl_ref[rows, :], psh_ref[rows, :],
                                      PARTIAL_ROT_DIM // 2)
                    slab = part * halves + c
                    if dil == 1:
                        out_ref[0, rows, slab * LANES:(slab + 1) * LANES] = chunk.astype(BF16)
                    else:
                        slab_ref[g - 1, slab, rows, :] = chunk
            if dil > 1:
                n = th // dil
                for r in range(dil):
                    for slab in range(3 * halves):
                        out_ref[r, h * n:(h + 1) * n, slab * LANES:(slab + 1) * LANES] = (
                            slab_ref[g - 1, slab, pl.ds(h * th + r, n, stride=dil), :].astype(BF16))

        qa = head_norm_rope(proj(0, A_Q_W), qg_ref[...])
        qat = qa.T.astype(BF16)
        for i in range(th // TQ):
            qat_ref[h * (th // TQ) + i] = qat[:, i * TQ:(i + 1) * TQ]

        kva = proj(A_Q_W, 2 * A_KV_W)
        ka_ref[rows, :] = head_norm_rope(kva[:, :A_KV_W], kg_ref[...]).astype(BF16)

        vat = kva[:, A_KV_W:].T.astype(BF16)
        pad_row = lax.broadcasted_iota(jnp.int32, (VT_ROWS - HEAD_DIM, th), 0)
        ones_row = jnp.where(pad_row == 0, 1.0, 0.0).astype(BF16)
        for hk in range(A_KV_HEADS):
            vat_ref[hk, 0:HEAD_DIM, rows] = vat[hk * HEAD_DIM:(hk + 1) * HEAD_DIM, :]
            vat_ref[hk, HEAD_DIM:VT_ROWS, rows] = ones_row

        for g in (2, 1, 0):
            branch_b(g)


def _qkv_project(x, mod3, g1, w_qkv, qg, kg, bd, tabs):
    b, s, d = x.shape
    tm = TM_QKV
    ncols = w_qkv.shape[1]
    row_tab = pl.BlockSpec((tm, LANES), lambda i, bb: (i, 0))
    const2 = lambda shape: pl.BlockSpec(shape, lambda i, bb: (0, 0))
    tok = lambda w: pl.BlockSpec((None, tm, w), lambda i, bb: (bb, i, 0))
    out_shapes = (
        jax.ShapeDtypeStruct((b, s // TQ, A_Q_W, TQ), BF16),
        jax.ShapeDtypeStruct((b, s, A_KV_W), BF16),
        jax.ShapeDtypeStruct((b, s // TK, A_KV_HEADS, VT_ROWS, TK), BF16),
    ) + tuple(jax.ShapeDtypeStruct((b, dil, s // dil, 3 * B_OUT_W), BF16) for _, dil in B_GROUPS)
    out_specs = (
        pl.BlockSpec((None, tm // TQ, A_Q_W, TQ), lambda i, bb: (bb, i, 0, 0)),
        tok(A_KV_W),
        pl.BlockSpec((None, None, A_KV_HEADS, VT_ROWS, tm),
                     lambda i, bb: (bb, i // (TK // tm), 0, 0, i % (TK // tm))),
    ) + tuple(pl.BlockSpec((None, dil, tm // dil, 3 * B_OUT_W), lambda i, bb: (bb, 0, i, 0))
              for _, dil in B_GROUPS)
    return pl.pallas_call(
        _qkv_kernel,
        grid=(s // tm, b),
        in_specs=[tok(d),
                  pl.BlockSpec((None, 6, d), lambda i, bb: (bb, 0, 0)),
                  const2((1, d)),
                  const2((d, ncols)),
                  const2((1, LANES)), const2((1, LANES)), const2((MXU_DIM, MXU_DIM)),
                  row_tab, row_tab, row_tab, row_tab, row_tab, row_tab],
        out_specs=out_specs,
        out_shape=out_shapes,
        scratch_shapes=[pltpu.VMEM((len(B_GROUPS) - 1, 3 * B_OUT_W // LANES, tm, LANES), F32)],
        compiler_params=_params("parallel", "parallel"),
        name="qkv",
    )(x, mod3, g1, w_qkv, qg, kg, bd, *tabs)


def _gqa_kernel(qt_ref, k_ref, vt_ref, o_ref, qx_ref, s_ref):
    nk = vt_ref.shape[0]
    ncol = A_GROUP * TQ
    units = [(t, hk) for t in range(GQA_TILES) for hk in range(A_KV_HEADS)]

    qx_ref[...] = jnp.zeros(qx_ref.shape, BF16)
    for u, (t, hk) in enumerate(units):
        for g in range(A_GROUP):
            h = hk * A_GROUP + g
            qx_ref[u, hk * HEAD_DIM:(hk + 1) * HEAD_DIM, g * TQ:(g + 1) * TQ] = (
                qt_ref[t, h * HEAD_DIM:(h + 1) * HEAD_DIM, :])

    def phase(score_unit, value_unit, m_prev):
        qx = None if score_unit is None else qx_ref[score_unit]

        def body(j, carry):
            mrun, acc = carry
            rows = pl.ds(pl.multiple_of(j * TK, TK), TK)
            if score_unit is not None:
                st = jnp.dot(k_ref[rows, :], qx, preferred_element_type=F32)
                s_ref[score_unit % 2, rows, :] = st
                mrun = jnp.maximum(mrun, jnp.max(st.reshape(TK // 8, 8, ncol), axis=0))
            if value_unit is not None:
                p = jnp.exp2((s_ref[value_unit % 2, rows, :] - m_prev).astype(BF16))
                acc = acc + jnp.dot(vt_ref[j, units[value_unit][1]], p, preferred_element_type=F32)
            return mrun, acc

        init = (jnp.full((8, ncol), NEG_INF, F32), jnp.zeros((VT_ROWS, ncol), F32))
        mrun, acc = lax.fori_loop(0, nk, body, init, unroll=GQA_UNROLL)
        return jnp.max(mrun, axis=0, keepdims=True), acc

    m = None
    for idx in range(len(units) + 1):
        score_unit = idx if idx < len(units) else None
        value_unit = idx - 1 if idx > 0 else None
        m, acc = phase(score_unit, value_unit, m)
        if value_unit is not None:
            t, hk = units[value_unit]
            out_t = acc[0:HEAD_DIM, :] / acc[HEAD_DIM:HEAD_DIM + 1, :]
            stacked = jnp.concatenate([out_t[:, g * TQ:(g + 1) * TQ] for g in range(A_GROUP)], axis=0)
            o_ref[t * TQ:(t + 1) * TQ, hk * A_GROUP * HEAD_DIM:(hk + 1) * A_GROUP * HEAD_DIM] = (
                stacked.T.astype(BF16))


def _global_attention(qat, ka, vat):
    b, nq = qat.shape[0], qat.shape[1]
    s = ka.shape[1]
    nk = vat.shape[1]
    ncol = A_GROUP * TQ
    return pl.pallas_call(
        _gqa_kernel,
        grid=(b, nq // GQA_TILES),
        in_specs=[pl.BlockSpec((None, GQA_TILES, A_Q_W, TQ), lambda bb, i: (bb, i, 0, 0)),
                  pl.BlockSpec((None, s, A_KV_W), lambda bb, i: (bb, 0, 0)),
                  pl.BlockSpec((None, nk, A_KV_HEADS, VT_ROWS, TK), lambda bb, i: (bb, 0, 0, 0, 0))],
        out_specs=pl.BlockSpec((None, GQA_TILES * TQ, A_Q_W), lambda bb, i: (bb, i, 0)),
        out_shape=jax.ShapeDtypeStruct((b, s, A_Q_W), BF16),
        scratch_shapes=[pltpu.VMEM((GQA_TILES * A_KV_HEADS, A_KV_W, ncol), BF16),
                        pltpu.VMEM((2, s, ncol), F32)],
        compiler_params=_params("parallel", "parallel"),
        name="gqa",
    )(qat, ka, vat)


def _swa_kernel(qkv_ref, o_ref, lse_ref, sc_ref, p_ref, top_ref, bias_ref, *, radius):
    dil, seq = qkv_ref.shape[0], qkv_ref.shape[1]
    nblk = seq // BLK
    halves = B_OUT_W // LANES
    lane = lax.broadcasted_iota(jnp.int32, (BLK, LANES), 1)
    first_head = lane < HEAD_DIM
    head_mask = [jnp.where(first_head, 1.0, 0.0).astype(BF16), jnp.where(first_head, 0.0, 1.0).astype(BF16)]
    ones_win = jnp.ones((WIN, LANES), BF16)

    rel = (lax.broadcasted_iota(jnp.int32, (BLK, WIN), 0)
           - lax.broadcasted_iota(jnp.int32, (BLK, WIN), 1))
    for n in range(bias_ref.shape[0]):
        bias_ref[n] = jnp.where(jnp.abs(rel + n * radius) <= radius, 0.0, NEG_INF)

    def geometry(i):
        r = i // nblk
        i0 = pl.multiple_of((i % nblk) * BLK, BLK)
        ws = pl.multiple_of(jnp.clip(i0 - radius, 0, seq - WIN), radius)
        return r, i0, ws

    def scores(i, slot):
        r, i0, ws = geometry(i)
        bias = bias_ref[(i0 - ws) // radius]
        for c in range(halves):
            q = qkv_ref[r, pl.ds(i0, BLK), c * LANES:(c + 1) * LANES]
            k = qkv_ref[r, pl.ds(ws, WIN), B_OUT_W + c * LANES:B_OUT_W + (c + 1) * LANES]
            for hh in range(2):
                sc_ref[slot, 2 * c + hh] = lax.dot_general(
                    q * head_mask[hh], k, (((1,), (1,)), ((), ())), preferred_element_type=F32) + bias

    def probabilities(slot):
        for c in range(halves):
            tops = []
            for hh in range(2):
                sc = sc_ref[slot, 2 * c + hh]
                m = jnp.max(sc, axis=-1, keepdims=True)
                p_ref[slot, 2 * c + hh] = jnp.exp2((sc - m).astype(BF16))
                tops.append(m)
            top_ref[slot, c] = jnp.where(first_head, tops[0], tops[1])

    def outputs(i, slot):
        r, i0, ws = geometry(i)
        for c in range(halves):
            v = qkv_ref[r, pl.ds(ws, WIN), 2 * B_OUT_W + c * LANES:2 * B_OUT_W + (c + 1) * LANES]
            v_ones = jnp.concatenate([v, ones_win], axis=1)
            outs = [jnp.dot(p_ref[slot, 2 * c + hh], v_ones, preferred_element_type=F32) for hh in range(2)]
            num = jnp.where(first_head, outs[0][:, :LANES], outs[1][:, :LANES])
            den = jnp.where(first_head, outs[0][:, LANES:], outs[1][:, LANES:])
            classes = o_ref.shape[1]
            step = dil // classes
            rows = pl.ds(i0 * step + r // classes, BLK, stride=step) if step > 1 else pl.ds(i0, BLK)
            o_ref[c, r % classes, rows, :] = num / den
            lse_ref[c, r % classes, rows, :] = (top_ref[slot, c] + jnp.log2(den)) * LN2

    total = dil * nblk
    scores(0, 0)
    probabilities(0)
    scores(1, 1)

    def body(i, _):
        outputs(i - 2, i % 2)
        probabilities((i - 1) % 2)
        scores(i, i % 2)
        return 0

    lax.fori_loop(2, total, body, 0, unroll=2)
    outputs(total - 2, total % 2)
    probabilities((total - 1) % 2)
    outputs(total - 1, (total - 1) % 2)


def _banded_attention(qkv, group):
    window, dil = B_GROUPS[group]
    b, _, seq, width = qkv.shape
    radius = window // 2 // dil
    halves = B_OUT_W // LANES
    classes = max(1, dil // SWA_MAX_STORE_STRIDE)
    out_spec = pl.BlockSpec((None, halves, classes, dil * seq // classes, LANES), lambda bb: (bb, 0, 0, 0, 0))
    out_shape = jax.ShapeDtypeStruct((b, halves, classes, dil * seq // classes, LANES), F32)
    return pl.pallas_call(
        functools.partial(_swa_kernel, radius=radius),
        grid=(b,),
        in_specs=[pl.BlockSpec((None, dil, seq, width), lambda bb: (bb, 0, 0, 0))],
        out_specs=(out_spec, out_spec),
        out_shape=(out_shape, out_shape),
        scratch_shapes=[pltpu.VMEM((2, B_HEADS_PER_GROUP, BLK, WIN), F32),
                        pltpu.VMEM((2, B_HEADS_PER_GROUP, BLK, WIN), BF16),
                        pltpu.VMEM((2, B_OUT_W // LANES, BLK, LANES), F32),
                        pltpu.VMEM(((WIN - BLK) // radius + 1, BLK, WIN), F32)],
        compiler_params=_params("parallel"),
        name=f"swa{group}",
    )(qkv)


def _mix_kernel(x_ref, mod_ref, g1_ref, attn_ref, o0_ref, o1_ref, o2_ref, l0_ref, l1_ref, l2_ref,
                wg_ref, bg_ref, wpa_ref, wpb_ref, wo_ref, out_ref, nat_ref):
    x = x_ref[...]
    d = x.shape[-1]
    halves = B_OUT_W // LANES
    ya = jnp.dot(attn_ref[...], wpa_ref[...], preferred_element_type=F32)
    u = _modulated_norm(x, g1_ref[...], mod_ref[0:1, :], mod_ref[1:2, :])

    def token_order(src, c, slot):
        classes, n = src.shape[1], src.shape[2]
        if classes == 1:
            return src[c, 0]
        for k in range(classes):
            nat_ref[slot, pl.ds(k, n, stride=classes), :] = src[c, k]
        return nat_ref[slot]

    def combine(c):
        groups = len(B_GROUPS)
        lses = [token_order(l_ref, c, (2 * c) * groups + g) for g, l_ref in enumerate((l0_ref, l1_ref, l2_ref))]
        outs = [token_order(o_ref, c, (2 * c + 1) * groups + g) for g, o_ref in enumerate((o0_ref, o1_ref, o2_ref))]
        top = jnp.maximum(jnp.maximum(lses[0], lses[1]), lses[2])
        es = [jnp.exp(l - top) for l in lses]
        den = es[0] + es[1] + es[2]
        return sum((e / den) * o for e, o in zip(es, outs)).astype(BF16)

    def gate_logits(j, parts=4):
        w = 2 * d // parts
        return jnp.dot(u, wg_ref[:, j * w:(j + 1) * w], preferred_element_type=F32) + bg_ref[:, j * w:(j + 1) * w]

    logits, comb = [], []
    for j in range(4):
        logits.append(gate_logits(j))
        if j < halves:
            comb.append(combine(j))
    yb = jnp.dot(jnp.concatenate(comb, axis=1), wpb_ref[...], preferred_element_type=F32)
    gate_a = jax.nn.sigmoid(jnp.concatenate(logits[:2], axis=1))
    gate_b = jax.nn.sigmoid(jnp.concatenate(logits[2:], axis=1))
    merged = (gate_a * ya + gate_b * yb).astype(BF16)
    mix = jnp.dot(merged, wo_ref[...], preferred_element_type=F32)
    out_ref[...] = x + mod_ref[2:3, :] * mix


def _mix(x, mod3, g1, attn, os_, lses, wg, bg, wpa, wpb, wo):
    b, s, d = x.shape
    tm = TM_MIX
    tok = lambda w: pl.BlockSpec((None, tm, w), lambda bb, i: (bb, i, 0))
    const2 = lambda shape: pl.BlockSpec(shape, lambda bb, i: (0, 0))
    halves = B_OUT_W // LANES
    sub = [pl.BlockSpec((None, halves, o.shape[2], tm // o.shape[2], LANES), lambda bb, i: (bb, 0, 0, i, 0))
           for o in os_]
    return pl.pallas_call(
        _mix_kernel,
        grid=(b, s // tm),
        in_specs=[tok(d),
                  pl.BlockSpec((None, 6, d), lambda bb, i: (bb, 0, 0)),
                  const2((1, d)),
                  tok(A_Q_W),
                  *sub, *sub,
                  const2(wg.shape), const2(bg.shape), const2(wpa.shape), const2(wpb.shape),
                  const2(wo.shape)],
        out_specs=tok(d),
        out_shape=jax.ShapeDtypeStruct((b, s, d), F32),
        scratch_shapes=[pltpu.VMEM((2 * halves * len(B_GROUPS), tm, LANES), F32)],
        compiler_params=_params("parallel", "parallel"),
        name="mix",
    )(x, mod3, g1, attn, *os_, *lses, wg, bg, wpa, wpb, wo)


def _ffn_kernel(x_ref, mod_ref, g2_ref, win_ref, wout_ref, fg_ref, out_ref, *, final, bounds):
    x = x_ref[...]
    d_ff = wout_ref.shape[0]
    u = _modulated_norm(x, g2_ref[...], mod_ref[3:4, :], mod_ref[4:5, :])
    acc = jnp.zeros(x.shape, F32)
    for lo, hi in zip(bounds[:-1], bounds[1:]):
        hg = jnp.dot(u, win_ref[:, lo:hi], preferred_element_type=F32)
        hu = jnp.dot(u, win_ref[:, d_ff + lo:d_ff + hi], preferred_element_type=F32)
        act = ((hg * jax.nn.sigmoid(hg)) * hu).astype(BF16)
        acc += jnp.dot(act, wout_ref[lo:hi, :], preferred_element_type=F32)
    y = x + mod_ref[5:6, :] * acc
    if final:
        ms = jnp.mean(y * y, axis=-1, keepdims=True)
        y = (y * lax.rsqrt(ms + EPS)) * fg_ref[...]
    out_ref[...] = y


def _ffn(x, mod3, g2, win, wout, fg, final):
    b, s, d = x.shape
    tm = TM_FFN
    d_ff = wout.shape[0]
    tiles = pl.cdiv(d_ff, MXU_DIM)
    bounds = (0, min(d_ff, pl.cdiv(tiles, 2) * MXU_DIM), d_ff)
    tok = pl.BlockSpec((None, tm, d), lambda bb, i: (bb, i, 0))
    resident = lambda shape: pl.BlockSpec(shape, lambda bb, i: (0, 0), pipeline_mode=pl.Buffered(1))
    return pl.pallas_call(
        functools.partial(_ffn_kernel, final=final, bounds=bounds),
        grid=(b, s // tm),
        in_specs=[tok,
                  pl.BlockSpec((None, 6, d), lambda bb, i: (bb, 0, 0)),
                  pl.BlockSpec((1, d), lambda bb, i: (0, 0)),
                  resident(win.shape), resident(wout.shape),
                  pl.BlockSpec((1, d), lambda bb, i: (0, 0))],
        out_specs=tok,
        out_shape=jax.ShapeDtypeStruct((b, s, d), F32),
        compiler_params=_params("parallel", "parallel"),
        name="ffn",
    )(x, mod3, g2, win, wout, fg)


def _inv_freq(dim, theta):
    return theta ** (-jnp.arange(0, dim, 2, dtype=F32) / dim)


def _rope_tables(s):
    in_head = np.arange(LANES) % HEAD_DIM
    half = HEAD_DIM // 2
    quarter = half // 2
    inv_a = _inv_freq(half, AXIAL_THETA)[in_head % quarter][None, :]
    ang_row = jnp.arange(s // GRID_W, dtype=jnp.int32).astype(F32)[:, None] * inv_a
    ang_col = jnp.arange(GRID_W, dtype=jnp.int32).astype(F32)[:, None] * inv_a
    by_row = (in_head < half)[None, None, :]
    first_a = ((in_head % half) < quarter)[None, None, :]

    small = lax.optimization_barrier(
        (jnp.cos(ang_row), jnp.cos(ang_col), jnp.sin(ang_row), jnp.sin(ang_col)))

    def grid(of_row, of_col):
        return jnp.where(by_row, of_row[:, None, :], of_col[None, :, :])

    a_cos = grid(small[0], small[1]).reshape(s, LANES)
    a_sin = grid(small[2], small[3])
    a_lo = jnp.where(first_a, -a_sin, 0.0).reshape(s, LANES)
    a_hi = jnp.where(first_a, 0.0, a_sin).reshape(s, LANES)
    t = jnp.arange(s, dtype=jnp.int32)
    ap = t.astype(F32)[:, None] * _inv_freq(PARTIAL_ROT_DIM, PARTIAL_THETA)[None, :]
    rest = HEAD_DIM - PARTIAL_ROT_DIM
    zp = jnp.zeros_like(ap)
    p_cos = jnp.concatenate([jnp.cos(ap), jnp.cos(ap), jnp.ones((s, rest), F32)], axis=1)
    p_lo = jnp.concatenate([-jnp.sin(ap), zp, jnp.zeros((s, rest), F32)], axis=1)
    p_hi = jnp.concatenate([zp, jnp.sin(ap), jnp.zeros((s, rest), F32)], axis=1)
    two = lambda a: jnp.tile(a, (1, LANES // HEAD_DIM))
    return a_cos, a_lo, a_hi, two(p_cos), two(p_lo), two(p_hi)


def kernel(x, c, w_ada, b_ada, norm1_g, w_qkv, q_norm_a, k_norm_a, w_proj_a, w_proj_b, w_gate,
           b_gate, w_o, norm2_g, w_ffn_in, w_ffn_out, final_norm_g):
    b, s, d = x.shape
    depth = w_ada.shape[0]
    assert s % TM_QKV == 0 and s % (B_GROUPS[-1][1] * WIN) == 0 and TK % TM_QKV == 0 and s % TK == 0
    tabs = _rope_tables(s)
    heads_per_vreg = LANES // HEAD_DIM
    lane_head = np.arange(MXU_DIM) // HEAD_DIM
    bd = jnp.asarray(lane_head[:, None] == lane_head[None, :], BF16)
    b_q_lo = A_Q_W + 2 * A_KV_W
    col_scale = jnp.ones((w_qkv.shape[-1],), F32).at[b_q_lo:b_q_lo + B_W].set(SCALE * LOG2E)
    c_pad = jnp.pad(c, ((0, 16 - b), (0, 0)))
    for l in range(depth):
        mod3 = _modulation(c_pad, w_ada, b_ada[l], l)[:b].reshape(b, 6, d)
        qg = jnp.tile(q_norm_a[l] * (SCALE * LOG2E), heads_per_vreg).reshape(1, LANES)
        kg = jnp.tile(k_norm_a[l], heads_per_vreg).reshape(1, LANES)
        g1 = norm1_g[l].reshape(1, d)
        wq = (w_qkv[l] * col_scale[None, :]).astype(BF16)
        qat, ka, vat, *groups = _qkv_project(x, mod3, g1, wq, qg, kg, bd, tabs)
        attn = _global_attention(qat, ka, vat)
        branch = [_banded_attention(qkv_g, g) for g, qkv_g in enumerate(groups)]
        x = _mix(x, mod3, g1, attn, [o for o, _ in branch], [e for _, e in branch],
                 w_gate[l].astype(BF16), b_gate[l].reshape(1, -1), w_proj_a[l].astype(BF16),
                 w_proj_b[l].astype(BF16), w_o[l].astype(BF16))
        x = _ffn(x, mod3, norm2_g[l].reshape(1, d), w_ffn_in[l].astype(BF16),
                 w_ffn_out[l].astype(BF16), final_norm_g.reshape(1, d), final=(l == depth - 1))
    return x
```

```python
import functools

import jax
import jax.numpy as jnp
import numpy as np
from jax import lax
from jax.experimental import pallas as pl
from jax.experimental.pallas import tpu as pltpu

HEAD_DIM = 64
A_Q_HEADS = 8
A_KV_HEADS = 2
A_GROUP = A_Q_HEADS // A_KV_HEADS
B_GROUPS = ((128, 1), (512, 4), (2048, 16))
B_HEADS_PER_GROUP = 4
B_HEADS = B_HEADS_PER_GROUP * len(B_GROUPS)
A_Q_W = A_Q_HEADS * HEAD_DIM
A_KV_W = A_KV_HEADS * HEAD_DIM
B_W = B_HEADS * HEAD_DIM
B_OUT_W = B_HEADS_PER_GROUP * HEAD_DIM
GRID_W = 64
AXIAL_THETA = 10000.0
PARTIAL_THETA = 500000.0
PARTIAL_ROT_DIM = HEAD_DIM // 4
EPS = 1e-6
NEG_INF = -1e30
SCALE = HEAD_DIM ** -0.5
LOG2E = 1.4426950408889634
LN2 = 0.6931471805599453

LANES = 128
MXU_DIM = 256
VMEM_LIMIT = 56 * 1024 * 1024

F32 = jnp.float32
BF16 = jnp.bfloat16

TM_QKV = 512
QKV_ROW_SPLITS = 1
TQ = 128
TK = 1024
TM_MIX = 512
TM_FFN = 512
BLK = 128
WIN = 256
SWA_MAX_STORE_STRIDE = 4
VT_ROWS = HEAD_DIM + 16
GQA_UNROLL = 2
GQA_TILES = 16


def _params(*sem):
    return pltpu.CompilerParams(dimension_semantics=sem, vmem_limit_bytes=VMEM_LIMIT)


def _mod_kernel(c_ref, w_ref, b_ref, o_ref):
    c = c_ref[...]
    cond = c * jax.nn.sigmoid(c)
    c_hi = cond.astype(BF16)
    c_lo = (cond - c_hi.astype(F32)).astype(BF16)
    w = w_ref[...]
    w_hi = w.astype(BF16)
    w_lo = (w - w_hi.astype(F32)).astype(BF16)
    acc = jnp.dot(c_hi, w_hi, preferred_element_type=F32)
    acc += jnp.dot(c_hi, w_lo, preferred_element_type=F32)
    acc += jnp.dot(c_lo, w_hi, preferred_element_type=F32)
    o_ref[...] = acc + b_ref[...]


def _modulation(c_pad, w_ada, b_ada, layer):
    rows, d = c_pad.shape
    n = w_ada.shape[2]
    tn = 1024
    return pl.pallas_call(
        _mod_kernel,
        grid=(n // tn,),
        in_specs=[pl.BlockSpec((rows, d), lambda j: (0, 0)),
                  pl.BlockSpec((None, d, tn), lambda j: (layer, 0, j)),
                  pl.BlockSpec((1, tn), lambda j: (0, j))],
        out_specs=pl.BlockSpec((rows, tn), lambda j: (0, j)),
        out_shape=jax.ShapeDtypeStruct((rows, n), F32),
        compiler_params=_params("parallel"),
        name="mod",
    )(c_pad, w_ada, b_ada.reshape(1, n))


def _modulated_norm(x, gain, shift, scale):
    ms = jnp.mean(x * x, axis=-1, keepdims=True)
    return ((x * lax.rsqrt(ms + EPS)) * (gain * (1.0 + scale)) + shift).astype(BF16)


def _rope(y, cos, sin_lo, sin_hi, shift):
    return (y * cos + pltpu.roll(y, LANES - shift, 1) * sin_lo
            + pltpu.roll(y, shift, 1) * sin_hi)


def _qkv_kernel(x_ref, mod_ref, g1_ref, w_ref, qg_ref, kg_ref, bd_ref,
                ac_ref, asl_ref, ash_ref, pc_ref, psl_ref, psh_ref,
                qat_ref, ka_ref, vat_ref, grp0_ref, grp1_ref, grp2_ref, slab_ref):
    tm = x_ref.shape[0]
    th = tm // QKV_ROW_SPLITS
    for h in range(QKV_ROW_SPLITS):
        rows = slice(h * th, (h + 1) * th)
        u = _modulated_norm(x_ref[rows, :], g1_ref[...], mod_ref[0:1, :], mod_ref[1:2, :])

        def proj(lo, width):
            return jnp.dot(u, w_ref[:, lo:lo + width], preferred_element_type=F32)

        def head_norm_rope(wide, gain):
            out = []
            for lo in range(0, wide.shape[1], MXU_DIM):
                tile = wide[:, lo:lo + MXU_DIM]
                w = tile.shape[1]
                ss = jnp.dot((tile * tile).astype(BF16), bd_ref[:w, :w], preferred_element_type=F32)
                for c in range(w // LANES):
                    cols = slice(c * LANES, (c + 1) * LANES)
                    y = (tile[:, cols] * lax.rsqrt(ss[:, cols] * (1.0 / HEAD_DIM) + EPS)) * gain
                    out.append(_rope(y, ac_ref[rows, :], asl_ref[rows, :], ash_ref[rows, :], HEAD_DIM // 4))
            return out[0] if len(out) == 1 else jnp.concatenate(out, axis=1)

        def branch_b(g):
            out_ref, dil = (grp0_ref, grp1_ref, grp2_ref)[g], B_GROUPS[g][1]
            base = A_Q_W + 2 * A_KV_W
            halves = B_OUT_W // LANES
            for part in range(3):
                wide = proj(base + part * B_W + g * B_OUT_W, B_OUT_W)
                for c in range(halves):
                    chunk = wide[:, c * LANES:(c + 1) * LANES]
                    if part < 2:
                        chunk = _rope(chunk, pc_ref[rows, :], psl_ref[rows, :], psh_ref[rows, :],
                                      PARTIAL_ROT_DIM // 2)
                    slab = part * halves + c
                    if dil == 1:
                        out_ref[0, rows, slab * LANES:(slab + 1) * LANES] = chunk.astype(BF16)
                    else:
                        slab_ref[g - 1, slab, rows, :] = chunk
            if dil > 1:
                n = th // dil
                for r in range(dil):
                    for slab in range(3 * halves):
                        out_ref[r, h * n:(h + 1) * n, slab * LANES:(slab + 1) * LANES] = (
                            slab_ref[g - 1, slab, pl.ds(h * th + r, n, stride=dil), :].astype(BF16))

        qa = head_norm_rope(proj(0, A_Q_W), qg_ref[...])
        qat = qa.T.astype(BF16)
        for i in range(th // TQ):
            qat_ref[h * (th // TQ) + i] = qat[:, i * TQ:(i + 1) * TQ]

        kva = proj(A_Q_W, 2 * A_KV_W)
        ka_ref[rows, :] = head_norm_rope(kva[:, :A_KV_W], kg_ref[...]).astype(BF16)

        vat = kva[:, A_KV_W:].T.astype(BF16)
        pad_row = lax.broadcasted_iota(jnp.int32, (VT_ROWS - HEAD_DIM, th), 0)
        ones_row = jnp.where(pad_row == 0, 1.0, 0.0).astype(BF16)
        for hk in range(A_KV_HEADS):
            vat_ref[hk, 0:HEAD_DIM, rows] = vat[hk * HEAD_DIM:(hk + 1) * HEAD_DIM, :]
            vat_ref[hk, HEAD_DIM:VT_ROWS, rows] = ones_row

        for g in (2, 1, 0):
            branch_b(g)


def _qkv_project(x, mod3, g1, w_qkv, qg, kg, bd, tabs):
    b, s, d = x.shape
    tm = TM_QKV
    ncols = w_qkv.shape[1]
    row_tab = pl.BlockSpec((tm, LANES), lambda i, bb: (i, 0))
    const2 = lambda shape: pl.BlockSpec(shape, lambda i, bb: (0, 0))
    tok = lambda w: pl.BlockSpec((None, tm, w), lambda i, bb: (bb, i, 0))
    out_shapes = (
        jax.ShapeDtypeStruct((b, s // TQ, A_Q_W, TQ), BF16),
        jax.ShapeDtypeStruct((b, s, A_KV_W), BF16),
        jax.ShapeDtypeStruct((b, s // TK, A_KV_HEADS, VT_ROWS, TK), BF16),
    ) + tuple(jax.ShapeDtypeStruct((b, dil, s // dil, 3 * B_OUT_W), BF16) for _, dil in B_GROUPS)
    out_specs = (
        pl.BlockSpec((None, tm // TQ, A_Q_W, TQ), lambda i, bb: (bb, i, 0, 0)),
        tok(A_KV_W),
        pl.BlockSpec((None, None, A_KV_HEADS, VT_ROWS, tm),
                     lambda i, bb: (bb, i // (TK // tm), 0, 0, i % (TK // tm))),
    ) + tuple(pl.BlockSpec((None, dil, tm // dil, 3 * B_OUT_W), lambda i, bb: (bb, 0, i, 0))
              for _, dil in B_GROUPS)
    return pl.pallas_call(
        _qkv_kernel,
        grid=(s // tm, b),
        in_specs=[tok(d),
                  pl.BlockSpec((None, 6, d), lambda i, bb: (bb, 0, 0)),
                  const2((1, d)),
                  const2((d, ncols)),
                  const2((1, LANES)), const2((1, LANES)), const2((MXU_DIM, MXU_DIM)),
                  row_tab, row_tab, row_tab, row_tab, row_tab, row_tab],
        out_specs=out_specs,
        out_shape=out_shapes,
        scratch_shapes=[pltpu.VMEM((len(B_GROUPS) - 1, 3 * B_OUT_W // LANES, tm, LANES), F32)],
        compiler_params=_params("parallel", "parallel"),
        name="qkv",
    )(x, mod3, g1, w_qkv, qg, kg, bd, *tabs)


def _gqa_kernel(qt_ref, k_ref, vt_ref, o_ref, qx_ref, s_ref):
    nk = vt_ref.shape[0]
    ncol = A_GROUP * TQ
    units = [(t, hk) for t in range(GQA_TILES) for hk in range(A_KV_HEADS)]

    qx_ref[...] = jnp.zeros(qx_ref.shape, BF16)
    for u, (t, hk) in enumerate(units):
        for g in range(A_GROUP):
            h = hk * A_GROUP + g
            qx_ref[u, hk * HEAD_DIM:(hk + 1) * HEAD_DIM, g * TQ:(g + 1) * TQ] = (
                qt_ref[t, h * HEAD_DIM:(h + 1) * HEAD_DIM, :])

    def phase(score_unit, value_unit, m_prev):
        qx = None if score_unit is None else qx_ref[score_unit]

        def body(j, carry):
            mrun, acc = carry
            rows = pl.ds(pl.multiple_of(j * TK, TK), TK)
            if score_unit is not None:
                st = jnp.dot(k_ref[rows, :], qx, preferred_element_type=F32)
                s_ref[score_unit % 2, rows, :] = st
                mrun = jnp.maximum(mrun, jnp.max(st.reshape(TK // 8, 8, ncol), axis=0))
            if value_unit is not None:
                p = jnp.exp2((s_ref[value_unit % 2, rows, :] - m_prev).astype(BF16))
                acc = acc + jnp.dot(vt_ref[j, units[value_unit][1]], p, preferred_element_type=F32)
            return mrun, acc

        init = (jnp.full((8, ncol), NEG_INF, F32), jnp.zeros((VT_ROWS, ncol), F32))
        mrun, acc = lax.fori_loop(0, nk, body, init, unroll=GQA_UNROLL)
        return jnp.max(mrun, axis=0, keepdims=True), acc

    m = None
    for idx in range(len(units) + 1):
        score_unit = idx if idx < len(units) else None
        value_unit = idx - 1 if idx > 0 else None
        m, acc = phase(score_unit, value_unit, m)
        if value_unit is not None:
            t, hk = units[value_unit]
            out_t = acc[0:HEAD_DIM, :] / acc[HEAD_DIM:HEAD_DIM + 1, :]
            stacked = jnp.concatenate([out_t[:, g * TQ:(g + 1) * TQ] for g in range(A_GROUP)], axis=0)
            o_ref[t * TQ:(t + 1) * TQ, hk * A_GROUP * HEAD_DIM:(hk + 1) * A_GROUP * HEAD_DIM] = (
                stacked.T.astype(BF16))


def _global_attention(qat, ka, vat):
    b, nq = qat.shape[0], qat.shape[1]
    s = ka.shape[1]
    nk = vat.shape[1]
    ncol = A_GROUP * TQ
    return pl.pallas_call(
        _gqa_kernel,
        grid=(b, nq // GQA_TILES),
        in_specs=[pl.BlockSpec((None, GQA_TILES, A_Q_W, TQ), lambda bb, i: (bb, i, 0, 0)),
                  pl.BlockSpec((None, s, A_KV_W), lambda bb, i: (bb, 0, 0)),
                  pl.BlockSpec((None, nk, A_KV_HEADS, VT_ROWS, TK), lambda bb, i: (bb, 0, 0, 0, 0))],
        out_specs=pl.BlockSpec((None, GQA_TILES * TQ, A_Q_W), lambda bb, i: (bb, i, 0)),
        out_shape=jax.ShapeDtypeStruct((b, s, A_Q_W), BF16),
        scratch_shapes=[pltpu.VMEM((GQA_TILES * A_KV_HEADS, A_KV_W, ncol), BF16),
                        pltpu.VMEM((2, s, ncol), F32)],
        compiler_params=_params("parallel", "parallel"),
        name="gqa",
    )(qat, ka, vat)


def _swa_kernel(qkv_ref, o_ref, lse_ref, sc_ref, p_ref, top_ref, bias_ref, *, radius):
    dil, seq = qkv_ref.shape[0], qkv_ref.shape[1]
    nblk = seq // BLK
    halves = B_OUT_W // LANES
    lane = lax.broadcasted_iota(jnp.int32, (BLK, LANES), 1)
    first_head = lane < HEAD_DIM
    head_mask = [jnp.where(first_head, 1.0, 0.0).astype(BF16), jnp.where(first_head, 0.0, 1.0).astype(BF16)]
    ones_win = jnp.ones((WIN, LANES), BF16)

    rel = (lax.broadcasted_iota(jnp.int32, (BLK, WIN), 0)
           - lax.broadcasted_iota(jnp.int32, (BLK, WIN), 1))
    for n in range(bias_ref.shape[0]):
        bias_ref[n] = jnp.where(jnp.abs(rel + n * radius) <= radius, 0.0, NEG_INF)

    def geometry(i):
        r = i // nblk
        i0 = pl.multiple_of((i % nblk) * BLK, BLK)
        ws = pl.multiple_of(jnp.clip(i0 - radius, 0, seq - WIN), radius)
        return r, i0, ws

    def scores(i, slot):
        r, i0, ws = geometry(i)
        bias = bias_ref[(i0 - ws) // radius]
        for c in range(halves):
            q = qkv_ref[r, pl.ds(i0, BLK), c * LANES:(c + 1) * LANES]
            k = qkv_ref[r, pl.ds(ws, WIN), B_OUT_W + c * LANES:B_OUT_W + (c + 1) * LANES]
            for hh in range(2):
                sc_ref[slot, 2 * c + hh] = lax.dot_general(
                    q * head_mask[hh], k, (((1,), (1,)), ((), ())), preferred_element_type=F32) + bias

    def probabilities(slot):
        for c in range(halves):
            tops = []
            for hh in range(2):
                sc = sc_ref[slot, 2 * c + hh]
                m = jnp.max(sc, axis=-1, keepdims=True)
                p_ref[slot, 2 * c + hh] = jnp.exp2((sc - m).astype(BF16))
                tops.append(m)
            top_ref[slot, c] = jnp.where(first_head, tops[0], tops[1])

    def outputs(i, slot):
        r, i0, ws = geometry(i)
        for c in range(halves):
            v = qkv_ref[r, pl.ds(ws, WIN), 2 * B_OUT_W + c * LANES:2 * B_OUT_W + (c + 1) * LANES]
            v_ones = jnp.concatenate([v, ones_win], axis=1)
            outs = [jnp.dot(p_ref[slot, 2 * c + hh], v_ones, preferred_element_type=F32) for hh in range(2)]
            num = jnp.where(first_head, outs[0][:, :LANES], outs[1][:, :LANES])
            den = jnp.where(first_head, outs[0][:, LANES:], outs[1][:, LANES:])
            classes = o_ref.shape[1]
            step = dil // classes
            rows = pl.ds(i0 * step + r // classes, BLK, stride=step) if step > 1 else pl.ds(i0, BLK)
            o_ref[c, r % classes, rows, :] = num / den
            lse_ref[c, r % classes, rows, :] = (top_ref[slot, c] + jnp.log2(den)) * LN2

    total = dil * nblk
    scores(0, 0)
    probabilities(0)
    scores(1, 1)

    def body(i, _):
        outputs(i - 2, i % 2)
        probabilities((i - 1) % 2)
        scores(i, i % 2)
        return 0

    lax.fori_loop(2, total, body, 0, unroll=2)
    outputs(total - 2, total % 2)
    probabilities((total - 1) % 2)
    outputs(total - 1, (total - 1) % 2)


def _banded_attention(qkv, group):
    window, dil = B_GROUPS[group]
    b, _, seq, width = qkv.shape
    radius = window // 2 // dil
    halves = B_OUT_W // LANES
    classes = max(1, dil // SWA_MAX_STORE_STRIDE)
    out_spec = pl.BlockSpec((None, halves, classes, dil * seq // classes, LANES), lambda bb: (bb, 0, 0, 0, 0))
    out_shape = jax.ShapeDtypeStruct((b, halves, classes, dil * seq // classes, LANES), F32)
    return pl.pallas_call(
        functools.partial(_swa_kernel, radius=radius),
        grid=(b,),
        in_specs=[pl.BlockSpec((None, dil, seq, width), lambda bb: (bb, 0, 0, 0))],
        out_specs=(out_spec, out_spec),
        out_shape=(out_shape, out_shape),
        scratch_shapes=[pltpu.VMEM((2, B_HEADS_PER_GROUP, BLK, WIN), F32),
                        pltpu.VMEM((2, B_HEADS_PER_GROUP, BLK, WIN), BF16),
                        pltpu.VMEM((2, B_OUT_W // LANES, BLK, LANES), F32),
                        pltpu.VMEM(((WIN - BLK) // radius + 1, BLK, WIN), F32)],
        compiler_params=_params("parallel"),
        name=f"swa{group}",
    )(qkv)


def _mix_kernel(x_ref, mod_ref, g1_ref, attn_ref, o0_ref, o1_ref, o2_ref, l0_ref, l1_ref, l2_ref,
                wg_ref, bg_ref, wpa_ref, wpb_ref, wo_ref, out_ref, nat_ref):
    x = x_ref[...]
    d = x.shape[-1]
    halves = B_OUT_W // LANES
    ya = jnp.dot(attn_ref[...], wpa_ref[...], preferred_element_type=F32)
    u = _modulated_norm(x, g1_ref[...], mod_ref[0:1, :], mod_ref[1:2, :])

    def token_order(src, c, slot):
        classes, n = src.shape[1], src.shape[2]
        if classes == 1:
            return src[c, 0]
        for k in range(classes):
            nat_ref[slot, pl.ds(k, n, stride=classes), :] = src[c, k]
        return nat_ref[slot]

    def combine(c):
        groups = len(B_GROUPS)
        lses = [token_order(l_ref, c, (2 * c) * groups + g) for g, l_ref in enumerate((l0_ref, l1_ref, l2_ref))]
        outs = [token_order(o_ref, c, (2 * c + 1) * groups + g) for g, o_ref in enumerate((o0_ref, o1_ref, o2_ref))]
        top = jnp.maximum(jnp.maximum(lses[0], lses[1]), lses[2])
        es = [jnp.exp(l - top) for l in lses]
        den = es[0] + es[1] + es[2]
        return sum((e / den) * o for e, o in zip(es, outs)).astype(BF16)

    def gate_logits(j, parts=4):
        w = 2 * d // parts
        return jnp.dot(u, wg_ref[:, j * w:(j + 1) * w], preferred_element_type=F32) + bg_ref[:, j * w:(j + 1) * w]

    logits, comb = [], []
    for j in range(4):
        logits.append(gate_logits(j))
        if j < halves:
            comb.append(combine(j))
    yb = jnp.dot(jnp.concatenate(comb, axis=1), wpb_ref[...], preferred_element_type=F32)
    gate_a = jax.nn.sigmoid(jnp.concatenate(logits[:2], axis=1))
    gate_b = jax.nn.sigmoid(jnp.concatenate(logits[2:], axis=1))
    merged = (gate_a * ya + gate_b * yb).astype(BF16)
    mix = jnp.dot(merged, wo_ref[...], preferred_element_type=F32)
    out_ref[...] = x + mod_ref[2:3, :] * mix


def _mix(x, mod3, g1, attn, os_, lses, wg, bg, wpa, wpb, wo):
    b, s, d = x.shape
    tm = TM_MIX
    tok = lambda w: pl.BlockSpec((None, tm, w), lambda bb, i: (bb, i, 0))
    const2 = lambda shape: pl.BlockSpec(shape, lambda bb, i: (0, 0))
    halves = B_OUT_W // LANES
    sub = [pl.BlockSpec((None, halves, o.shape[2], tm // o.shape[2], LANES), lambda bb, i: (bb, 0, 0, i, 0))
           for o in os_]
    return pl.pallas_call(
        _mix_kernel,
        grid=(b, s // tm),
        in_specs=[tok(d),
                  pl.BlockSpec((None, 6, d), lambda bb, i: (bb, 0, 0)),
                  const2((1, d)),
                  tok(A_Q_W),
                  *sub, *sub,
                  const2(wg.shape), const2(bg.shape), const2(wpa.shape), const2(wpb.shape),
                  const2(wo.shape)],
        out_specs=tok(d),
        out_shape=jax.ShapeDtypeStruct((b, s, d), F32),
        scratch_shapes=[pltpu.VMEM((2 * halves * len(B_GROUPS), tm, LANES), F32)],
        compiler_params=_params("parallel", "parallel"),
        name="mix",
    )(x, mod3, g1, attn, *os_, *lses, wg, bg, wpa, wpb, wo)


def _ffn_kernel(x_ref, mod_ref, g2_ref, win_ref, wout_ref, fg_ref, out_ref, *, final, bounds):
    x = x_ref[...]
    d_ff = wout_ref.shape[0]
    u = _modulated_norm(x, g2_ref[...], mod_ref[3:4, :], mod_ref[4:5, :])
    acc = jnp.zeros(x.shape, F32)
    for lo, hi in zip(bounds[:-1], bounds[1:]):
        hg = jnp.dot(u, win_ref[:, lo:hi], preferred_element_type=F32)
        hu = jnp.dot(u, win_ref[:, d_ff + lo:d_ff + hi], preferred_element_type=F32)
        act = ((hg * jax.nn.sigmoid(hg)) * hu).astype(BF16)
        acc += jnp.dot(act, wout_ref[lo:hi, :], preferred_element_type=F32)
    y = x + mod_ref[5:6, :] * acc
    if final:
        ms = jnp.mean(y * y, axis=-1, keepdims=True)
        y = (y * lax.rsqrt(ms + EPS)) * fg_ref[...]
    out_ref[...] = y


def _ffn(x, mod3, g2, win, wout, fg, final):
    b, s, d = x.shape
    tm = TM_FFN
    d_ff = wout.shape[0]
    tiles = pl.cdiv(d_ff, MXU_DIM)
    bounds = (0, min(d_ff, pl.cdiv(tiles, 2) * MXU_DIM), d_ff)
    tok = pl.BlockSpec((None, tm, d), lambda bb, i: (bb, i, 0))
    resident = lambda shape: pl.BlockSpec(shape, lambda bb, i: (0, 0), pipeline_mode=pl.Buffered(1))
    return pl.pallas_call(
        functools.partial(_ffn_kernel, final=final, bounds=bounds),
        grid=(b, s // tm),
        in_specs=[tok,
                  pl.BlockSpec((None, 6, d), lambda bb, i: (bb, 0, 0)),
                  pl.BlockSpec((1, d), lambda bb, i: (0, 0)),
                  resident(win.shape), resident(wout.shape),
                  pl.BlockSpec((1, d), lambda bb, i: (0, 0))],
        out_specs=tok,
        out_shape=jax.ShapeDtypeStruct((b, s, d), F32),
        compiler_params=_params("parallel", "parallel"),
        name="ffn",
    )(x, mod3, g2, win, wout, fg)


def _inv_freq(dim, theta):
    return theta ** (-jnp.arange(0, dim, 2, dtype=F32) / dim)


def _rope_tables(s):
    in_head = np.arange(LANES) % HEAD_DIM
    half = HEAD_DIM // 2
    quarter = half // 2
    inv_a = (AXIAL_THETA ** (-jnp.asarray(2 * (in_head % quarter), F32) / half))[None, :]
    ang_row = jnp.arange(s // GRID_W, dtype=jnp.int32).astype(F32)[:, None] * inv_a
    ang_col = jnp.arange(GRID_W, dtype=jnp.int32).astype(F32)[:, None] * inv_a
    by_row = (in_head < half)[None, None, :]
    first_a = ((in_head % half) < quarter)[None, None, :]

    small = lax.optimization_barrier(
        (jnp.cos(ang_row), jnp.cos(ang_col), jnp.sin(ang_row), jnp.sin(ang_col)))

    def grid(of_row, of_col):
        return jnp.where(by_row, of_row[:, None, :], of_col[None, :, :])

    a_cos = grid(small[0], small[1]).reshape(s, LANES)
    a_sin = grid(small[2], small[3])
    a_lo = jnp.where(first_a, -a_sin, 0.0).reshape(s, LANES)
    a_hi = jnp.where(first_a, 0.0, a_sin).reshape(s, LANES)
    t = jnp.arange(s, dtype=jnp.int32)
    ap = t.astype(F32)[:, None] * _inv_freq(PARTIAL_ROT_DIM, PARTIAL_THETA)[None, :]
    rest = HEAD_DIM - PARTIAL_ROT_DIM
    zp = jnp.zeros_like(ap)
    p_cos = jnp.concatenate([jnp.cos(ap), jnp.cos(ap), jnp.ones((s, rest), F32)], axis=1)
    p_lo = jnp.concatenate([-jnp.sin(ap), zp, jnp.zeros((s, rest), F32)], axis=1)
    p_hi = jnp.concatenate([zp, jnp.sin(ap), jnp.zeros((s, rest), F32)], axis=1)
    two = lambda a: jnp.tile(a, (1, LANES // HEAD_DIM))
    return a_cos, a_lo, a_hi, two(p_cos), two(p_lo), two(p_hi)


def kernel(x, c, w_ada, b_ada, norm1_g, w_qkv, q_norm_a, k_norm_a, w_proj_a, w_proj_b, w_gate,
           b_gate, w_o, norm2_g, w_ffn_in, w_ffn_out, final_norm_g):
    b, s, d = x.shape
    depth = w_ada.shape[0]
    assert s % TM_QKV == 0 and s % (B_GROUPS[-1][1] * WIN) == 0 and TK % TM_QKV == 0 and s % TK == 0
    tabs = _rope_tables(s)
    heads_per_vreg = LANES // HEAD_DIM
    lane_head = np.arange(MXU_DIM) // HEAD_DIM
    bd = jnp.asarray(lane_head[:, None] == lane_head[None, :], BF16)
    b_q_lo = A_Q_W + 2 * A_KV_W
    col_scale = jnp.ones((w_qkv.shape[-1],), F32).at[b_q_lo:b_q_lo + B_W].set(SCALE * LOG2E)
    c_pad = jnp.pad(c, ((0, 16 - b), (0, 0)))
    for l in range(depth):
        mod3 = _modulation(c_pad, w_ada, b_ada[l], l)[:b].reshape(b, 6, d)
        qg = jnp.tile(q_norm_a[l] * (SCALE * LOG2E), heads_per_vreg).reshape(1, LANES)
        kg = jnp.tile(k_norm_a[l], heads_per_vreg).reshape(1, LANES)
        g1 = norm1_g[l].reshape(1, d)
        wq = (w_qkv[l] * col_scale[None, :]).astype(BF16)
        qat, ka, vat, *groups = _qkv_project(x, mod3, g1, wq, qg, kg, bd, tabs)
        attn = _global_attention(qat, ka, vat)
        branch = [_banded_attention(qkv_g, g) for g, qkv_g in enumerate(groups)]
        x = _mix(x, mod3, g1, attn, [o for o, _ in branch], [e for _, e in branch],
                 w_gate[l].astype(BF16), b_gate[l].reshape(1, -1), w_proj_a[l].astype(BF16),
                 w_proj_b[l].astype(BF16), w_o[l].astype(BF16))
        x = _ffn(x, mod3, norm2_g[l].reshape(1, d), w_ffn_in[l].astype(BF16),
                 w_ffn_out[l].astype(BF16), final_norm_g.reshape(1, d), final=(l == depth - 1))
    return x
```

```python
import functools

import jax
import jax.numpy as jnp
import numpy as np
from jax import lax
from jax.experimental import pallas as pl
from jax.experimental.pallas import tpu as pltpu

HEAD_DIM = 64
A_Q_HEADS = 8
A_KV_HEADS = 2
A_GROUP = A_Q_HEADS // A_KV_HEADS
B_GROUPS = ((128, 1), (512, 4), (2048, 16))
B_HEADS_PER_GROUP = 4
B_HEADS = B_HEADS_PER_GROUP * len(B_GROUPS)
A_Q_W = A_Q_HEADS * HEAD_DIM
A_KV_W = A_KV_HEADS * HEAD_DIM
B_W = B_HEADS * HEAD_DIM
B_OUT_W = B_HEADS_PER_GROUP * HEAD_DIM
GRID_W = 64
AXIAL_THETA = 10000.0
PARTIAL_THETA = 500000.0
PARTIAL_ROT_DIM = HEAD_DIM // 4
EPS = 1e-6
NEG_INF = -1e30
SCALE = HEAD_DIM ** -0.5
LOG2E = 1.4426950408889634
LN2 = 0.6931471805599453

LANES = 128
MXU_DIM = 256
VMEM_LIMIT = 56 * 1024 * 1024

F32 = jnp.float32
BF16 = jnp.bfloat16

TM_QKV = 512
QKV_ROW_SPLITS = 1
TQ = 128
TK = 1024
TM_MIX = 512
TM_FFN = 512
BLK = 128
WIN = 256
SWA_MAX_STORE_STRIDE = 4
VT_ROWS = HEAD_DIM + 16
GQA_UNROLL = 2
GQA_TILES = 8


def _params(*sem):
    return pltpu.CompilerParams(dimension_semantics=sem, vmem_limit_bytes=VMEM_LIMIT)


def _mod_kernel(c_ref, w_ref, b_ref, o_ref):
    c = c_ref[...]
    cond = c * jax.nn.sigmoid(c)
    c_hi = cond.astype(BF16)
    c_lo = (cond - c_hi.astype(F32)).astype(BF16)
    w = w_ref[...]
    w_hi = w.astype(BF16)
    w_lo = (w - w_hi.astype(F32)).astype(BF16)
    acc = jnp.dot(c_hi, w_hi, preferred_element_type=F32)
    acc += jnp.dot(c_hi, w_lo, preferred_element_type=F32)
    acc += jnp.dot(c_lo, w_hi, preferred_element_type=F32)
    o_ref[...] = acc + b_ref[...]


def _modulation(c_pad, w_ada, b_ada, layer):
    rows, d = c_pad.shape
    n = w_ada.shape[2]
    tn = 1024
    return pl.pallas_call(
        _mod_kernel,
        grid=(n // tn,),
        in_specs=[pl.BlockSpec((rows, d), lambda j: (0, 0)),
                  pl.BlockSpec((None, d, tn), lambda j: (layer, 0, j)),
                  pl.BlockSpec((1, tn), lambda j: (0, j))],
        out_specs=pl.BlockSpec((rows, tn), lambda j: (0, j)),
        out_shape=jax.ShapeDtypeStruct((rows, n), F32),
        compiler_params=_params("parallel"),
        name="mod",
    )(c_pad, w_ada, b_ada.reshape(1, n))


def _modulated_norm(x, gain, shift, scale):
    ms = jnp.mean(x * x, axis=-1, keepdims=True)
    return ((x * lax.rsqrt(ms + EPS)) * (gain * (1.0 + scale)) + shift).astype(BF16)


def _rope(y, cos, sin_lo, sin_hi, shift):
    return (y * cos + pltpu.roll(y, LANES - shift, 1) * sin_lo
            + pltpu.roll(y, shift, 1) * sin_hi)


def _qkv_kernel(x_ref, mod_ref, g1_ref, w_ref, qg_ref, kg_ref, bd_ref,
                ac_ref, asl_ref, ash_ref, pc_ref, psl_ref, psh_ref,
                qat_ref, ka_ref, vat_ref, grp0_ref, grp1_ref, grp2_ref, slab_ref):
    tm = x_ref.shape[0]
    th = tm // QKV_ROW_SPLITS
    for h in range(QKV_ROW_SPLITS):
        rows = slice(h * th, (h + 1) * th)
        u = _modulated_norm(x_ref[rows, :], g1_ref[...], mod_ref[0:1, :], mod_ref[1:2, :])

        def proj(lo, width):
            return jnp.dot(u, w_ref[:, lo:lo + width], preferred_element_type=F32)

        def head_norm_rope(wide, gain):
            out = []
            for lo in range(0, wide.shape[1], MXU_DIM):
                tile = wide[:, lo:lo + MXU_DIM]
                w = tile.shape[1]
                ss = jnp.dot((tile * tile).astype(BF16), bd_ref[:w, :w], preferred_element_type=F32)
                for c in range(w // LANES):
                    cols = slice(c * LANES, (c + 1) * LANES)
                    y = (tile[:, cols] * lax.rsqrt(ss[:, cols] * (1.0 / HEAD_DIM) + EPS)) * gain
                    out.append(_rope(y, ac_ref[rows, :], asl_ref[rows, :], ash_ref[rows, :], HEAD_DIM // 4))
            return out[0] if len(out) == 1 else jnp.concatenate(out, axis=1)

        def branch_b(g):
            out_ref, dil = (grp0_ref, grp1_ref, grp2_ref)[g], B_GROUPS[g][1]
            base = A_Q_W + 2 * A_KV_W
            halves = B_OUT_W // LANES
            for part in range(3):
                wide = proj(base + part * B_W + g * B_OUT_W, B_OUT_W)
                for c in range(halves):
                    chunk = wide[:, c * LANES:(c + 1) * LANES]
                    if part < 2:
                        chunk = _rope(chunk, pc_ref[rows, :], psl_ref[rows, :], psh_ref[rows, :],
                                      PARTIAL_ROT_DIM // 2)
                    slab = part * halves + c
                    if dil == 1:
                        out_ref[0, rows, slab * LANES:(slab + 1) * LANES] = chunk.astype(BF16)
                    else:
                        slab_ref[g - 1, slab, rows, :] = chunk
            if dil > 1:
                n = th // dil
                for r in range(dil):
                    for slab in range(3 * halves):
                        out_ref[r, h * n:(h + 1) * n, slab * LANES:(slab + 1) * LANES] = (
                            slab_ref[g - 1, slab, pl.ds(h * th + r, n, stride=dil), :].astype(BF16))

        qa = head_norm_rope(proj(0, A_Q_W), qg_ref[...])
        qat = qa.T.astype(BF16)
        for i in range(th // TQ):
            qat_ref[h * (th // TQ) + i] = qat[:, i * TQ:(i + 1) * TQ]

        kva = proj(A_Q_W, 2 * A_KV_W)
        ka_ref[rows, :] = head_norm_rope(kva[:, :A_KV_W], kg_ref[...]).astype(BF16)

        vat = kva[:, A_KV_W:].T.astype(BF16)
        pad_row = lax.broadcasted_iota(jnp.int32, (VT_ROWS - HEAD_DIM, th), 0)
        ones_row = jnp.where(pad_row == 0, 1.0, 0.0).astype(BF16)
        for hk in range(A_KV_HEADS):
            vat_ref[hk, 0:HEAD_DIM, rows] = vat[hk * HEAD_DIM:(hk + 1) * HEAD_DIM, :]
            vat_ref[hk, HEAD_DIM:VT_ROWS, rows] = ones_row

        for g in (2, 1, 0):
            branch_b(g)


def _qkv_project(x, mod3, g1, w_qkv, qg, kg, bd, tabs):
    b, s, d = x.shape
    tm = TM_QKV
    ncols = w_qkv.shape[1]
    row_tab = pl.BlockSpec((tm, LANES), lambda i, bb: (i, 0))
    const2 = lambda shape: pl.BlockSpec(shape, lambda i, bb: (0, 0))
    tok = lambda w: pl.BlockSpec((None, tm, w), lambda i, bb: (bb, i, 0))
    out_shapes = (
        jax.ShapeDtypeStruct((b, s // TQ, A_Q_W, TQ), BF16),
        jax.ShapeDtypeStruct((b, s, A_KV_W), BF16),
        jax.ShapeDtypeStruct((b, s // TK, A_KV_HEADS, VT_ROWS, TK), BF16),
    ) + tuple(jax.ShapeDtypeStruct((b, dil, s // dil, 3 * B_OUT_W), BF16) for _, dil in B_GROUPS)
    out_specs = (
        pl.BlockSpec((None, tm // TQ, A_Q_W, TQ), lambda i, bb: (bb, i, 0, 0)),
        tok(A_KV_W),
        pl.BlockSpec((None, None, A_KV_HEADS, VT_ROWS, tm),
                     lambda i, bb: (bb, i // (TK // tm), 0, 0, i % (TK // tm))),
    ) + tuple(pl.BlockSpec((None, dil, tm // dil, 3 * B_OUT_W), lambda i, bb: (bb, 0, i, 0))
              for _, dil in B_GROUPS)
    return pl.pallas_call(
        _qkv_kernel,
        grid=(s // tm, b),
        in_specs=[tok(d),
                  pl.BlockSpec((None, 6, d), lambda i, bb: (bb, 0, 0)),
                  const2((1, d)),
                  const2((d, ncols)),
                  const2((1, LANES)), const2((1, LANES)), const2((MXU_DIM, MXU_DIM)),
                  row_tab, row_tab, row_tab, row_tab, row_tab, row_tab],
        out_specs=out_specs,
        out_shape=out_shapes,
        scratch_shapes=[pltpu.VMEM((len(B_GROUPS) - 1, 3 * B_OUT_W // LANES, tm, LANES), F32)],
        compiler_params=_params("parallel", "parallel"),
        name="qkv",
    )(x, mod3, g1, w_qkv, qg, kg, bd, *tabs)


def _gqa_kernel(qt_ref, k_ref, vt_ref, o_ref, qx_ref, s_ref):
    nk = vt_ref.shape[0]
    ncol = A_GROUP * TQ
    units = [(t, hk) for t in range(GQA_TILES) for hk in range(A_KV_HEADS)]

    qx_ref[...] = jnp.zeros(qx_ref.shape, BF16)
    for u, (t, hk) in enumerate(units):
        for g in range(A_GROUP):
            h = hk * A_GROUP + g
            qx_ref[u, hk * HEAD_DIM:(hk + 1) * HEAD_DIM, g * TQ:(g + 1) * TQ] = (
                qt_ref[t, h * HEAD_DIM:(h + 1) * HEAD_DIM, :])

    def phase(score_unit, value_unit, m_prev):
        qx = None if score_unit is None else qx_ref[score_unit]

        def body(j, carry):
            mrun, acc = carry
            rows = pl.ds(pl.multiple_of(j * TK, TK), TK)
            if score_unit is not None:
                st = jnp.dot(k_ref[rows, :], qx, preferred_element_type=F32)
                s_ref[score_unit % 2, rows, :] = st
                mrun = jnp.maximum(mrun, jnp.max(st.reshape(TK // 8, 8, ncol), axis=0))
            if value_unit is not None:
                p = jnp.exp2((s_ref[value_unit % 2, rows, :] - m_prev).astype(BF16))
                acc = acc + jnp.dot(vt_ref[j, units[value_unit][1]], p, preferred_element_type=F32)
            return mrun, acc

        init = (jnp.full((8, ncol), NEG_INF, F32), jnp.zeros((VT_ROWS, ncol), F32))
        mrun, acc = lax.fori_loop(0, nk, body, init, unroll=GQA_UNROLL)
        return jnp.max(mrun, axis=0, keepdims=True), acc

    m = None
    for idx in range(len(units) + 1):
        score_unit = idx if idx < len(units) else None
        value_unit = idx - 1 if idx > 0 else None
        m, acc = phase(score_unit, value_unit, m)
        if value_unit is not None:
            t, hk = units[value_unit]
            out_t = acc[0:HEAD_DIM, :] / acc[HEAD_DIM:HEAD_DIM + 1, :]
            stacked = jnp.concatenate([out_t[:, g * TQ:(g + 1) * TQ] for g in range(A_GROUP)], axis=0)
            o_ref[t * TQ:(t + 1) * TQ, hk * A_GROUP * HEAD_DIM:(hk + 1) * A_GROUP * HEAD_DIM] = (
                stacked.T.astype(BF16))


def _global_attention(qat, ka, vat):
    b, nq = qat.shape[0], qat.shape[1]
    s = ka.shape[1]
    nk = vat.shape[1]
    ncol = A_GROUP * TQ
    return pl.pallas_call(
        _gqa_kernel,
        grid=(b, nq // GQA_TILES),
        in_specs=[pl.BlockSpec((None, GQA_TILES, A_Q_W, TQ), lambda bb, i: (bb, i, 0, 0)),
                  pl.BlockSpec((None, s, A_KV_W), lambda bb, i: (bb, 0, 0)),
                  pl.BlockSpec((None, nk, A_KV_HEADS, VT_ROWS, TK), lambda bb, i: (bb, 0, 0, 0, 0))],
        out_specs=pl.BlockSpec((None, GQA_TILES * TQ, A_Q_W), lambda bb, i: (bb, i, 0)),
        out_shape=jax.ShapeDtypeStruct((b, s, A_Q_W), BF16),
        scratch_shapes=[pltpu.VMEM((GQA_TILES * A_KV_HEADS, A_KV_W, ncol), BF16),
                        pltpu.VMEM((2, s, ncol), F32)],
        compiler_params=_params("parallel", "parallel"),
        name="gqa",
    )(qat, ka, vat)


def _swa_kernel(qkv_ref, o_ref, lse_ref, sc_ref, p_ref, top_ref, bias_ref, *, radius):
    dil, seq = qkv_ref.shape[0], qkv_ref.shape[1]
    nblk = seq // BLK
    halves = B_OUT_W // LANES
    lane = lax.broadcasted_iota(jnp.int32, (BLK, LANES), 1)
    first_head = lane < HEAD_DIM
    head_mask = [jnp.where(first_head, 1.0, 0.0).astype(BF16), jnp.where(first_head, 0.0, 1.0).astype(BF16)]
    ones_win = jnp.ones((WIN, LANES), BF16)

    rel = (lax.broadcasted_iota(jnp.int32, (BLK, WIN), 0)
           - lax.broadcasted_iota(jnp.int32, (BLK, WIN), 1))
    for n in range(bias_ref.shape[0]):
        bias_ref[n] = jnp.where(jnp.abs(rel + n * radius) <= radius, 0.0, NEG_INF)

    def geometry(i):
        r = i // nblk
        i0 = pl.multiple_of((i % nblk) * BLK, BLK)
        ws = pl.multiple_of(jnp.clip(i0 - radius, 0, seq - WIN), radius)
        return r, i0, ws

    def scores(i, slot):
        r, i0, ws = geometry(i)
        bias = bias_ref[(i0 - ws) // radius]
        for c in range(halves):
            q = qkv_ref[r, pl.ds(i0, BLK), c * LANES:(c + 1) * LANES]
            k = qkv_ref[r, pl.ds(ws, WIN), B_OUT_W + c * LANES:B_OUT_W + (c + 1) * LANES]
            for hh in range(2):
                sc_ref[slot, 2 * c + hh] = lax.dot_general(
                    q * head_mask[hh], k, (((1,), (1,)), ((), ())), preferred_element_type=F32) + bias

    def probabilities(slot):
        for c in range(halves):
            tops = []
            for hh in range(2):
                sc = sc_ref[slot, 2 * c + hh]
                m = jnp.max(sc, axis=-1, keepdims=True)
                p_ref[slot, 2 * c + hh] = jnp.exp2((sc - m).astype(BF16))
                tops.append(m)
            top_ref[slot, c] = jnp.where(first_head, tops[0], tops[1])

    def outputs(i, slot):
        r, i0, ws = geometry(i)
        for c in range(halves):
            v = qkv_ref[r, pl.ds(ws, WIN), 2 * B_OUT_W + c * LANES:2 * B_OUT_W + (c + 1) * LANES]
            v_ones = jnp.concatenate([v, ones_win], axis=1)
            outs = [jnp.dot(p_ref[slot, 2 * c + hh], v_ones, preferred_element_type=F32) for hh in range(2)]
            num = jnp.where(first_head, outs[0][:, :LANES], outs[1][:, :LANES])
            den = jnp.where(first_head, outs[0][:, LANES:], outs[1][:, LANES:])
            classes = o_ref.shape[1]
            step = dil // classes
            rows = pl.ds(i0 * step + r // classes, BLK, stride=step) if step > 1 else pl.ds(i0, BLK)
            o_ref[c, r % classes, rows, :] = num / den
            lse_ref[c, r % classes, rows, :] = (top_ref[slot, c] + jnp.log2(den)) * LN2

    total = dil * nblk
    scores(0, 0)
    probabilities(0)
    scores(1, 1)

    def body(i, _):
        outputs(i - 2, i % 2)
        probabilities((i - 1) % 2)
        scores(i, i % 2)
        return 0

    lax.fori_loop(2, total, body, 0, unroll=2)
    outputs(total - 2, total % 2)
    probabilities((total - 1) % 2)
    outputs(total - 1, (total - 1) % 2)


def _banded_attention(qkv, group):
    window, dil = B_GROUPS[group]
    b, _, seq, width = qkv.shape
    radius = window // 2 // dil
    halves = B_OUT_W // LANES
    classes = max(1, dil // SWA_MAX_STORE_STRIDE)
    out_spec = pl.BlockSpec((None, halves, classes, dil * seq // classes, LANES), lambda bb: (bb, 0, 0, 0, 0))
    out_shape = jax.ShapeDtypeStruct((b, halves, classes, dil * seq // classes, LANES), F32)
    return pl.pallas_call(
        functools.partial(_swa_kernel, radius=radius),
        grid=(b,),
        in_specs=[pl.BlockSpec((None, dil, seq, width), lambda bb: (bb, 0, 0, 0))],
        out_specs=(out_spec, out_spec),
        out_shape=(out_shape, out_shape),
        scratch_shapes=[pltpu.VMEM((2, B_HEADS_PER_GROUP, BLK, WIN), F32),
                        pltpu.VMEM((2, B_HEADS_PER_GROUP, BLK, WIN), BF16),
                        pltpu.VMEM((2, B_OUT_W // LANES, BLK, LANES), F32),
                        pltpu.VMEM(((WIN - BLK) // radius + 1, BLK, WIN), F32)],
        compiler_params=_params("parallel"),
        name=f"swa{group}",
    )(qkv)


def _mix_kernel(x_ref, mod_ref, g1_ref, attn_ref, o0_ref, o1_ref, o2_ref, l0_ref, l1_ref, l2_ref,
                wg_ref, bg_ref, wpa_ref, wpb_ref, wo_ref, out_ref, nat_ref):
    x = x_ref[...]
    d = x.shape[-1]
    halves = B_OUT_W // LANES
    ya = jnp.dot(attn_ref[...], wpa_ref[...], preferred_element_type=F32)
    u = _modulated_norm(x, g1_ref[...], mod_ref[0:1, :], mod_ref[1:2, :])

    def token_order(src, c, slot):
        classes, n = src.shape[1], src.shape[2]
        if classes == 1:
            return src[c, 0]
        for k in range(classes):
            nat_ref[slot, pl.ds(k, n, stride=classes), :] = src[c, k]
        return nat_ref[slot]

    def combine(c):
        groups = len(B_GROUPS)
        lses = [token_order(l_ref, c, (2 * c) * groups + g) for g, l_ref in enumerate((l0_ref, l1_ref, l2_ref))]
        outs = [token_order(o_ref, c, (2 * c + 1) * groups + g) for g, o_ref in enumerate((o0_ref, o1_ref, o2_ref))]
        top = jnp.maximum(jnp.maximum(lses[0], lses[1]), lses[2])
        es = [jnp.exp(l - top) for l in lses]
        den = es[0] + es[1] + es[2]
        return sum((e / den) * o for e, o in zip(es, outs)).astype(BF16)

    def gate_logits(j, parts=4):
        w = 2 * d // parts
        return jnp.dot(u, wg_ref[:, j * w:(j + 1) * w], preferred_element_type=F32) + bg_ref[:, j * w:(j + 1) * w]

    logits, comb = [], []
    for j in range(4):
        logits.append(gate_logits(j))
        if j < halves:
            comb.append(combine(j))
    yb = jnp.dot(jnp.concatenate(comb, axis=1), wpb_ref[...], preferred_element_type=F32)
    gate_a = jax.nn.sigmoid(jnp.concatenate(logits[:2], axis=1))
    gate_b = jax.nn.sigmoid(jnp.concatenate(logits[2:], axis=1))
    merged = (gate_a * ya + gate_b * yb).astype(BF16)
    mix = jnp.dot(merged, wo_ref[...], preferred_element_type=F32)
    out_ref[...] = x + mod_ref[2:3, :] * mix


def _mix(x, mod3, g1, attn, os_, lses, wg, bg, wpa, wpb, wo):
    b, s, d = x.shape
    tm = TM_MIX
    tok = lambda w: pl.BlockSpec((None, tm, w), lambda bb, i: (bb, i, 0))
    const2 = lambda shape: pl.BlockSpec(shape, lambda bb, i: (0, 0))
    halves = B_OUT_W // LANES
    sub = [pl.BlockSpec((None, halves, o.shape[2], tm // o.shape[2], LANES), lambda bb, i: (bb, 0, 0, i, 0))
           for o in os_]
    return pl.pallas_call(
        _mix_kernel,
        grid=(b, s // tm),
        in_specs=[tok(d),
                  pl.BlockSpec((None, 6, d), lambda bb, i: (bb, 0, 0)),
                  const2((1, d)),
                  tok(A_Q_W),
                  *sub, *sub,
                  const2(wg.shape), const2(bg.shape), const2(wpa.shape), const2(wpb.shape),
                  const2(wo.shape)],
        out_specs=tok(d),
        out_shape=jax.ShapeDtypeStruct((b, s, d), F32),
        scratch_shapes=[pltpu.VMEM((2 * halves * len(B_GROUPS), tm, LANES), F32)],
        compiler_params=_params("parallel", "parallel"),
        name="mix",
    )(x, mod3, g1, attn, *os_, *lses, wg, bg, wpa, wpb, wo)


def _ffn_kernel(x_ref, mod_ref, g2_ref, win_ref, wout_ref, fg_ref, out_ref, *, final, bounds):
    x = x_ref[...]
    d_ff = wout_ref.shape[0]
    u = _modulated_norm(x, g2_ref[...], mod_ref[3:4, :], mod_ref[4:5, :])
    acc = jnp.zeros(x.shape, F32)
    for lo, hi in zip(bounds[:-1], bounds[1:]):
        hg = jnp.dot(u, win_ref[:, lo:hi], preferred_element_type=F32)
        hu = jnp.dot(u, win_ref[:, d_ff + lo:d_ff + hi], preferred_element_type=F32)
        act = ((hg * jax.nn.sigmoid(hg)) * hu).astype(BF16)
        acc += jnp.dot(act, wout_ref[lo:hi, :], preferred_element_type=F32)
    y = x + mod_ref[5:6, :] * acc
    if final:
        ms = jnp.mean(y * y, axis=-1, keepdims=True)
        y = (y * lax.rsqrt(ms + EPS)) * fg_ref[...]
    out_ref[...] = y


def _ffn(x, mod3, g2, win, wout, fg, final):
    b, s, d = x.shape
    tm = TM_FFN
    d_ff = wout.shape[0]
    tiles = pl.cdiv(d_ff, MXU_DIM)
    bounds = (0, min(d_ff, pl.cdiv(tiles, 2) * MXU_DIM), d_ff)
    tok = pl.BlockSpec((None, tm, d), lambda bb, i: (bb, i, 0))
    resident = lambda shape: pl.BlockSpec(shape, lambda bb, i: (0, 0), pipeline_mode=pl.Buffered(1))
    return pl.pallas_call(
        functools.partial(_ffn_kernel, final=final, bounds=bounds),
        grid=(b, s // tm),
        in_specs=[tok,
                  pl.BlockSpec((None, 6, d), lambda bb, i: (bb, 0, 0)),
                  pl.BlockSpec((1, d), lambda bb, i: (0, 0)),
                  resident(win.shape), resident(wout.shape),
                  pl.BlockSpec((1, d), lambda bb, i: (0, 0))],
        out_specs=tok,
        out_shape=jax.ShapeDtypeStruct((b, s, d), F32),
        compiler_params=_params("parallel", "parallel"),
        name="ffn",
    )(x, mod3, g2, win, wout, fg)


def _inv_freq(dim, theta):
    return theta ** (-jnp.arange(0, dim, 2, dtype=F32) / dim)


def _rope_tables(s):
    in_head = np.arange(LANES) % HEAD_DIM
    half = HEAD_DIM // 2
    quarter = half // 2
    inv_a = (AXIAL_THETA ** (-jnp.asarray(2 * (in_head % quarter), F32) / half))[None, :]
    ang_row = jnp.arange(s // GRID_W, dtype=jnp.int32).astype(F32)[:, None] * inv_a
    ang_col = jnp.arange(GRID_W, dtype=jnp.int32).astype(F32)[:, None] * inv_a
    by_row = (in_head < half)[None, None, :]
    first_a = ((in_head % half) < quarter)[None, None, :]

    small = lax.optimization_barrier(
        (jnp.cos(ang_row), jnp.cos(ang_col), jnp.sin(ang_row), jnp.sin(ang_col)))

    def grid(of_row, of_col):
        return jnp.where(by_row, of_row[:, None, :], of_col[None, :, :])

    a_cos = grid(small[0], small[1]).reshape(s, LANES)
    a_sin = grid(small[2], small[3])
    a_lo = jnp.where(first_a, -a_sin, 0.0).reshape(s, LANES)
    a_hi = jnp.where(first_a, 0.0, a_sin).reshape(s, LANES)
    t = jnp.arange(s, dtype=jnp.int32)
    ap = t.astype(F32)[:, None] * _inv_freq(PARTIAL_ROT_DIM, PARTIAL_THETA)[None, :]
    rest = HEAD_DIM - PARTIAL_ROT_DIM
    zp = jnp.zeros_like(ap)
    p_cos = jnp.concatenate([jnp.cos(ap), jnp.cos(ap), jnp.ones((s, rest), F32)], axis=1)
    p_lo = jnp.concatenate([-jnp.sin(ap), zp, jnp.zeros((s, rest), F32)], axis=1)
    p_hi = jnp.concatenate([zp, jnp.sin(ap), jnp.zeros((s, rest), F32)], axis=1)
    two = lambda a: jnp.tile(a, (1, LANES // HEAD_DIM))
    return a_cos, a_lo, a_hi, two(p_cos), two(p_lo), two(p_hi)


def kernel(x, c, w_ada, b_ada, norm1_g, w_qkv, q_norm_a, k_norm_a, w_proj_a, w_proj_b, w_gate,
           b_gate, w_o, norm2_g, w_ffn_in, w_ffn_out, final_norm_g):
    b, s, d = x.shape
    depth = w_ada.shape[0]
    assert s % TM_QKV == 0 and s % (B_GROUPS[-1][1] * WIN) == 0 and TK % TM_QKV == 0 and s % TK == 0
    tabs = _rope_tables(s)
    heads_per_vreg = LANES // HEAD_DIM
    lane_head = np.arange(MXU_DIM) // HEAD_DIM
    bd = jnp.asarray(lane_head[:, None] == lane_head[None, :], BF16)
    b_q_lo = A_Q_W + 2 * A_KV_W
    col_scale = jnp.ones((w_qkv.shape[-1],), F32).at[b_q_lo:b_q_lo + B_W].set(SCALE * LOG2E)
    c_pad = jnp.pad(c, ((0, 16 - b), (0, 0)))
    for l in range(depth):
        mod3 = _modulation(c_pad, w_ada, b_ada[l], l)[:b].reshape(b, 6, d)
        qg = jnp.tile(q_norm_a[l] * (SCALE * LOG2E), heads_per_vreg).reshape(1, LANES)
        kg = jnp.tile(k_norm_a[l], heads_per_vreg).reshape(1, LANES)
        g1 = norm1_g[l].reshape(1, d)
        wq = (w_qkv[l] * col_scale[None, :]).astype(BF16)
        qat, ka, vat, *groups = _qkv_project(x, mod3, g1, wq, qg, kg, bd, tabs)
        attn = _global_attention(qat, ka, vat)
        branch = [_banded_attention(qkv_g, g) for g, qkv_g in enumerate(groups)]
        x = _mix(x, mod3, g1, attn, [o for o, _ in branch], [e for _, e in branch],
                 w_gate[l].astype(BF16), b_gate[l].reshape(1, -1), w_proj_a[l].astype(BF16),
                 w_proj_b[l].astype(BF16), w_o[l].astype(BF16))
        x = _ffn(x, mod3, norm2_g[l].reshape(1, d), w_ffn_in[l].astype(BF16),
                 w_ffn_out[l].astype(BF16), final_norm_g.reshape(1, d), final=(l == depth - 1))
    return x
```

```python
import functools

import jax
import jax.numpy as jnp
import numpy as np
from jax import lax
from jax.experimental import pallas as pl
from jax.experimental.pallas import tpu as pltpu

HEAD_DIM = 64
A_Q_HEADS = 8
A_KV_HEADS = 2
A_GROUP = A_Q_HEADS // A_KV_HEADS
B_GROUPS = ((128, 1), (512, 4), (2048, 16))
B_HEADS_PER_GROUP = 4
B_HEADS = B_HEADS_PER_GROUP * len(B_GROUPS)
A_Q_W = A_Q_HEADS * HEAD_DIM
A_KV_W = A_KV_HEADS * HEAD_DIM
B_W = B_HEADS * HEAD_DIM
B_OUT_W = B_HEADS_PER_GROUP * HEAD_DIM
GRID_W = 64
AXIAL_THETA = 10000.0
PARTIAL_THETA = 500000.0
PARTIAL_ROT_DIM = HEAD_DIM // 4
EPS = 1e-6
NEG_INF = -1e30
SCALE = HEAD_DIM ** -0.5
LOG2E = 1.4426950408889634
LN2 = 0.6931471805599453

LANES = 128
MXU_DIM = 256
VMEM_LIMIT = 56 * 1024 * 1024

F32 = jnp.float32
BF16 = jnp.bfloat16

TM_QKV = 1024
QKV_ROW_SPLITS = 1
TQ = 128
TK = 1024
TM_MIX = 1024
TM_FFN = 1024
FFN_CHUNKS = 3
BLK = 128
WIN = 256
SWA_MAX_STORE_STRIDE = 4
VT_ROWS = HEAD_DIM + 16
GQA_UNROLL = 2
GQA_TILES = 8


def _params(*sem):
    return pltpu.CompilerParams(dimension_semantics=sem, vmem_limit_bytes=VMEM_LIMIT)


def _mod_kernel(c_ref, w_ref, b_ref, o_ref):
    c = c_ref[...]
    cond = c * jax.nn.sigmoid(c)
    c_hi = cond.astype(BF16)
    c_lo = (cond - c_hi.astype(F32)).astype(BF16)
    w = w_ref[...]
    w_hi = w.astype(BF16)
    w_lo = (w - w_hi.astype(F32)).astype(BF16)
    acc = jnp.dot(c_hi, w_hi, preferred_element_type=F32)
    acc += jnp.dot(c_hi, w_lo, preferred_element_type=F32)
    acc += jnp.dot(c_lo, w_hi, preferred_element_type=F32)
    o_ref[...] = acc + b_ref[...]


def _modulation(c_pad, w_ada, b_ada, layer):
    rows, d = c_pad.shape
    n = w_ada.shape[2]
    tn = 1024
    return pl.pallas_call(
        _mod_kernel,
        grid=(n // tn,),
        in_specs=[pl.BlockSpec((rows, d), lambda j: (0, 0)),
                  pl.BlockSpec((None, d, tn), lambda j: (layer, 0, j)),
                  pl.BlockSpec((1, tn), lambda j: (0, j))],
        out_specs=pl.BlockSpec((rows, tn), lambda j: (0, j)),
        out_shape=jax.ShapeDtypeStruct((rows, n), F32),
        compiler_params=_params("parallel"),
        name="mod",
    )(c_pad, w_ada, b_ada.reshape(1, n))


def _modulated_norm(x, gain, shift, scale):
    ms = jnp.mean(x * x, axis=-1, keepdims=True)
    return ((x * lax.rsqrt(ms + EPS)) * (gain * (1.0 + scale)) + shift).astype(BF16)


def _rope(y, cos, sin_lo, sin_hi, shift):
    return (y * cos + pltpu.roll(y, LANES - shift, 1) * sin_lo
            + pltpu.roll(y, shift, 1) * sin_hi)


def _qkv_kernel(x_ref, mod_ref, g1_ref, w_ref, qg_ref, kg_ref, bd_ref,
                ac_ref, asl_ref, ash_ref, pc_ref, psl_ref, psh_ref,
                qat_ref, ka_ref, vat_ref, grp0_ref, grp1_ref, grp2_ref, slab_ref):
    tm = x_ref.shape[0]
    th = tm // QKV_ROW_SPLITS
    for h in range(QKV_ROW_SPLITS):
        rows = slice(h * th, (h + 1) * th)
        u = _modulated_norm(x_ref[rows, :], g1_ref[...], mod_ref[0:1, :], mod_ref[1:2, :])

        def proj(lo, width):
            return jnp.dot(u, w_ref[:, lo:lo + width], preferred_element_type=F32)

        def head_norm_rope(wide, gain):
            out = []
            for lo in range(0, wide.shape[1], MXU_DIM):
                tile = wide[:, lo:lo + MXU_DIM]
                w = tile.shape[1]
                ss = jnp.dot((tile * tile).astype(BF16), bd_ref[:w, :w], preferred_element_type=F32)
                for c in range(w // LANES):
                    cols = slice(c * LANES, (c + 1) * LANES)
                    y = (tile[:, cols] * lax.rsqrt(ss[:, cols] * (1.0 / HEAD_DIM) + EPS)) * gain
                    out.append(_rope(y, ac_ref[rows, :], asl_ref[rows, :], ash_ref[rows, :], HEAD_DIM // 4))
            return out[0] if len(out) == 1 else jnp.concatenate(out, axis=1)

        def branch_b(g):
            out_ref, dil = (grp0_ref, grp1_ref, grp2_ref)[g], B_GROUPS[g][1]
            base = A_Q_W + 2 * A_KV_W
            halves = B_OUT_W // LANES
            for part in range(3):
                wide = proj(base + part * B_W + g * B_OUT_W, B_OUT_W)
                for c in range(halves):
                    chunk = wide[:, c * LANES:(c + 1) * LANES]
                    if part < 2:
                        chunk = _rope(chunk, pc_ref[rows, :], psl_ref[rows, :], psh_ref[rows, :],
                                      PARTIAL_ROT_DIM // 2)
                    slab = part * halves + c
                    if dil == 1:
                        out_ref[0, rows, slab * LANES:(slab + 1) * LANES] = chunk.astype(BF16)
                    else:
                        slab_ref[g - 1, slab, rows, :] = chunk
            if dil > 1:
                n = th // dil
                for r in range(dil):
                    for slab in range(3 * halves):
                        out_ref[r, h * n:(h + 1) * n, slab * LANES:(slab + 1) * LANES] = (
                            slab_ref[g - 1, slab, pl.ds(h * th + r, n, stride=dil), :].astype(BF16))

        qa = head_norm_rope(proj(0, A_Q_W), qg_ref[...])
        qat = qa.T.astype(BF16)
        for i in range(th // TQ):
            qat_ref[h * (th // TQ) + i] = qat[:, i * TQ:(i + 1) * TQ]

        kva = proj(A_Q_W, 2 * A_KV_W)
        ka_ref[rows, :] = head_norm_rope(kva[:, :A_KV_W], kg_ref[...]).astype(BF16)

        vat = kva[:, A_KV_W:].T.astype(BF16)
        pad_row = lax.broadcasted_iota(jnp.int32, (VT_ROWS - HEAD_DIM, th), 0)
        ones_row = jnp.where(pad_row == 0, 1.0, 0.0).astype(BF16)
        for hk in range(A_KV_HEADS):
            vat_ref[hk, 0:HEAD_DIM, rows] = vat[hk * HEAD_DIM:(hk + 1) * HEAD_DIM, :]
            vat_ref[hk, HEAD_DIM:VT_ROWS, rows] = ones_row

        for g in (2, 1, 0):
            branch_b(g)


def _qkv_project(x, mod3, g1, w_qkv, qg, kg, bd, tabs):
    b, s, d = x.shape
    tm = TM_QKV
    ncols = w_qkv.shape[1]
    row_tab = pl.BlockSpec((tm, LANES), lambda i, bb: (i, 0))
    const2 = lambda shape: pl.BlockSpec(shape, lambda i, bb: (0, 0))
    tok = lambda w: pl.BlockSpec((None, tm, w), lambda i, bb: (bb, i, 0))
    out_shapes = (
        jax.ShapeDtypeStruct((b, s // TQ, A_Q_W, TQ), BF16),
        jax.ShapeDtypeStruct((b, s, A_KV_W), BF16),
        jax.ShapeDtypeStruct((b, s // TK, A_KV_HEADS, VT_ROWS, TK), BF16),
    ) + tuple(jax.ShapeDtypeStruct((b, dil, s // dil, 3 * B_OUT_W), BF16) for _, dil in B_GROUPS)
    out_specs = (
        pl.BlockSpec((None, tm // TQ, A_Q_W, TQ), lambda i, bb: (bb, i, 0, 0)),
        tok(A_KV_W),
        pl.BlockSpec((None, None, A_KV_HEADS, VT_ROWS, tm),
                     lambda i, bb: (bb, i // (TK // tm), 0, 0, i % (TK // tm))),
    ) + tuple(pl.BlockSpec((None, dil, tm // dil, 3 * B_OUT_W), lambda i, bb: (bb, 0, i, 0))
              for _, dil in B_GROUPS)
    return pl.pallas_call(
        _qkv_kernel,
        grid=(s // tm, b),
        in_specs=[tok(d),
                  pl.BlockSpec((None, 6, d), lambda i, bb: (bb, 0, 0)),
                  const2((1, d)),
                  const2((d, ncols)),
                  const2((1, LANES)), const2((1, LANES)), const2((MXU_DIM, MXU_DIM)),
                  row_tab, row_tab, row_tab, row_tab, row_tab, row_tab],
        out_specs=out_specs,
        out_shape=out_shapes,
        scratch_shapes=[pltpu.VMEM((len(B_GROUPS) - 1, 3 * B_OUT_W // LANES, tm, LANES), F32)],
        compiler_params=_params("parallel", "parallel"),
        name="qkv",
    )(x, mod3, g1, w_qkv, qg, kg, bd, *tabs)


def _gqa_kernel(qt_ref, k_ref, vt_ref, o_ref, qx_ref, s_ref):
    nk = vt_ref.shape[0]
    ncol = A_GROUP * TQ
    units = [(t, hk) for t in range(GQA_TILES) for hk in range(A_KV_HEADS)]

    qx_ref[...] = jnp.zeros(qx_ref.shape, BF16)
    for u, (t, hk) in enumerate(units):
        for g in range(A_GROUP):
            h = hk * A_GROUP + g
            qx_ref[u, hk * HEAD_DIM:(hk + 1) * HEAD_DIM, g * TQ:(g + 1) * TQ] = (
                qt_ref[t, h * HEAD_DIM:(h + 1) * HEAD_DIM, :])

    def phase(score_unit, value_unit, m_prev):
        qx = None if score_unit is None else qx_ref[score_unit]

        def body(j, carry):
            mrun, acc = carry
            rows = pl.ds(pl.multiple_of(j * TK, TK), TK)
            if score_unit is not None:
                st = jnp.dot(k_ref[rows, :], qx, preferred_element_type=F32)
                s_ref[score_unit % 2, rows, :] = st
                mrun = jnp.maximum(mrun, jnp.max(st.reshape(TK // 8, 8, ncol), axis=0))
            if value_unit is not None:
                p = jnp.exp2((s_ref[value_unit % 2, rows, :] - m_prev).astype(BF16))
                acc = acc + jnp.dot(vt_ref[j, units[value_unit][1]], p, preferred_element_type=F32)
            return mrun, acc

        init = (jnp.full((8, ncol), NEG_INF, F32), jnp.zeros((VT_ROWS, ncol), F32))
        mrun, acc = lax.fori_loop(0, nk, body, init, unroll=GQA_UNROLL)
        return jnp.max(mrun, axis=0, keepdims=True), acc

    m = None
    for idx in range(len(units) + 1):
        score_unit = idx if idx < len(units) else None
        value_unit = idx - 1 if idx > 0 else None
        m, acc = phase(score_unit, value_unit, m)
        if value_unit is not None:
            t, hk = units[value_unit]
            out_t = acc[0:HEAD_DIM, :] / acc[HEAD_DIM:HEAD_DIM + 1, :]
            stacked = jnp.concatenate([out_t[:, g * TQ:(g + 1) * TQ] for g in range(A_GROUP)], axis=0)
            o_ref[t * TQ:(t + 1) * TQ, hk * A_GROUP * HEAD_DIM:(hk + 1) * A_GROUP * HEAD_DIM] = (
                stacked.T.astype(BF16))


def _global_attention(qat, ka, vat):
    b, nq = qat.shape[0], qat.shape[1]
    s = ka.shape[1]
    nk = vat.shape[1]
    ncol = A_GROUP * TQ
    return pl.pallas_call(
        _gqa_kernel,
        grid=(b, nq // GQA_TILES),
        in_specs=[pl.BlockSpec((None, GQA_TILES, A_Q_W, TQ), lambda bb, i: (bb, i, 0, 0)),
                  pl.BlockSpec((None, s, A_KV_W), lambda bb, i: (bb, 0, 0)),
                  pl.BlockSpec((None, nk, A_KV_HEADS, VT_ROWS, TK), lambda bb, i: (bb, 0, 0, 0, 0))],
        out_specs=pl.BlockSpec((None, GQA_TILES * TQ, A_Q_W), lambda bb, i: (bb, i, 0)),
        out_shape=jax.ShapeDtypeStruct((b, s, A_Q_W), BF16),
        scratch_shapes=[pltpu.VMEM((GQA_TILES * A_KV_HEADS, A_KV_W, ncol), BF16),
                        pltpu.VMEM((2, s, ncol), F32)],
        compiler_params=_params("parallel", "parallel"),
        name="gqa",
    )(qat, ka, vat)


def _swa_kernel(qkv_ref, o_ref, lse_ref, sc_ref, p_ref, top_ref, bias_ref, *, radius):
    dil, seq = qkv_ref.shape[0], qkv_ref.shape[1]
    nblk = seq // BLK
    halves = B_OUT_W // LANES
    lane = lax.broadcasted_iota(jnp.int32, (BLK, LANES), 1)
    first_head = lane < HEAD_DIM
    head_mask = [jnp.where(first_head, 1.0, 0.0).astype(BF16), jnp.where(first_head, 0.0, 1.0).astype(BF16)]
    ones_win = jnp.ones((WIN, LANES), BF16)

    rel = (lax.broadcasted_iota(jnp.int32, (BLK, WIN), 0)
           - lax.broadcasted_iota(jnp.int32, (BLK, WIN), 1))
    for n in range(bias_ref.shape[0]):
        bias_ref[n] = jnp.where(jnp.abs(rel + n * radius) <= radius, 0.0, NEG_INF)

    def geometry(i):
        r = i // nblk
        i0 = pl.multiple_of((i % nblk) * BLK, BLK)
        ws = pl.multiple_of(jnp.clip(i0 - radius, 0, seq - WIN), radius)
        return r, i0, ws

    def scores(i, slot):
        r, i0, ws = geometry(i)
        bias = bias_ref[(i0 - ws) // radius]
        for c in range(halves):
            q = qkv_ref[r, pl.ds(i0, BLK), c * LANES:(c + 1) * LANES]
            k = qkv_ref[r, pl.ds(ws, WIN), B_OUT_W + c * LANES:B_OUT_W + (c + 1) * LANES]
            for hh in range(2):
                sc_ref[slot, 2 * c + hh] = lax.dot_general(
                    q * head_mask[hh], k, (((1,), (1,)), ((), ())), preferred_element_type=F32) + bias

    def probabilities(slot):
        for c in range(halves):
            tops = []
            for hh in range(2):
                sc = sc_ref[slot, 2 * c + hh]
                m = jnp.max(sc, axis=-1, keepdims=True)
                p_ref[slot, 2 * c + hh] = jnp.exp2((sc - m).astype(BF16))
                tops.append(m)
            top_ref[slot, c] = jnp.where(first_head, tops[0], tops[1])

    def outputs(i, slot):
        r, i0, ws = geometry(i)
        for c in range(halves):
            v = qkv_ref[r, pl.ds(ws, WIN), 2 * B_OUT_W + c * LANES:2 * B_OUT_W + (c + 1) * LANES]
            v_ones = jnp.concatenate([v, ones_win], axis=1)
            outs = [jnp.dot(p_ref[slot, 2 * c + hh], v_ones, preferred_element_type=F32) for hh in range(2)]
            num = jnp.where(first_head, outs[0][:, :LANES], outs[1][:, :LANES])
            den = jnp.where(first_head, outs[0][:, LANES:], outs[1][:, LANES:])
            classes = o_ref.shape[1]
            step = dil // classes
            rows = pl.ds(i0 * step + r // classes, BLK, stride=step) if step > 1 else pl.ds(i0, BLK)
            o_ref[c, r % classes, rows, :] = num / den
            lse_ref[c, r % classes, rows, :] = (top_ref[slot, c] + jnp.log2(den)) * LN2

    total = dil * nblk
    scores(0, 0)
    probabilities(0)
    scores(1, 1)

    def body(i, _):
        outputs(i - 2, i % 2)
        probabilities((i - 1) % 2)
        scores(i, i % 2)
        return 0

    lax.fori_loop(2, total, body, 0, unroll=2)
    outputs(total - 2, total % 2)
    probabilities((total - 1) % 2)
    outputs(total - 1, (total - 1) % 2)


def _banded_attention(qkv, group):
    window, dil = B_GROUPS[group]
    b, _, seq, width = qkv.shape
    radius = window // 2 // dil
    halves = B_OUT_W // LANES
    classes = max(1, dil // SWA_MAX_STORE_STRIDE)
    out_spec = pl.BlockSpec((None, halves, classes, dil * seq // classes, LANES), lambda bb: (bb, 0, 0, 0, 0))
    out_shape = jax.ShapeDtypeStruct((b, halves, classes, dil * seq // classes, LANES), F32)
    return pl.pallas_call(
        functools.partial(_swa_kernel, radius=radius),
        grid=(b,),
        in_specs=[pl.BlockSpec((None, dil, seq, width), lambda bb: (bb, 0, 0, 0))],
        out_specs=(out_spec, out_spec),
        out_shape=(out_shape, out_shape),
        scratch_shapes=[pltpu.VMEM((2, B_HEADS_PER_GROUP, BLK, WIN), F32),
                        pltpu.VMEM((2, B_HEADS_PER_GROUP, BLK, WIN), BF16),
                        pltpu.VMEM((2, B_OUT_W // LANES, BLK, LANES), F32),
                        pltpu.VMEM(((WIN - BLK) // radius + 1, BLK, WIN), F32)],
        compiler_params=_params("parallel"),
        name=f"swa{group}",
    )(qkv)


def _mix_kernel(x_ref, mod_ref, g1_ref, attn_ref, o0_ref, o1_ref, o2_ref, l0_ref, l1_ref, l2_ref,
                wg_ref, bg_ref, wpa_ref, wpb_ref, wo_ref, out_ref, nat_ref):
    x = x_ref[...]
    d = x.shape[-1]
    halves = B_OUT_W // LANES
    ya = jnp.dot(attn_ref[...], wpa_ref[...], preferred_element_type=F32)
    u = _modulated_norm(x, g1_ref[...], mod_ref[0:1, :], mod_ref[1:2, :])

    def token_order(src, c, slot):
        classes, n = src.shape[1], src.shape[2]
        if classes == 1:
            return src[c, 0]
        for k in range(classes):
            nat_ref[slot, pl.ds(k, n, stride=classes), :] = src[c, k]
        return nat_ref[slot]

    def combine(c):
        groups = len(B_GROUPS)
        lses = [token_order(l_ref, c, (2 * c) * groups + g) for g, l_ref in enumerate((l0_ref, l1_ref, l2_ref))]
        outs = [token_order(o_ref, c, (2 * c + 1) * groups + g) for g, o_ref in enumerate((o0_ref, o1_ref, o2_ref))]
        top = jnp.maximum(jnp.maximum(lses[0], lses[1]), lses[2])
        es = [jnp.exp(l - top) for l in lses]
        den = es[0] + es[1] + es[2]
        return sum((e / den) * o for e, o in zip(es, outs)).astype(BF16)

    def gate_logits(j, parts=4):
        w = 2 * d // parts
        return jnp.dot(u, wg_ref[:, j * w:(j + 1) * w], preferred_element_type=F32) + bg_ref[:, j * w:(j + 1) * w]

    logits, comb = [], []
    for j in range(4):
        logits.append(gate_logits(j))
        if j < halves:
            comb.append(combine(j))
    yb = jnp.dot(jnp.concatenate(comb, axis=1), wpb_ref[...], preferred_element_type=F32)
    gate_a = jax.nn.sigmoid(jnp.concatenate(logits[:2], axis=1))
    gate_b = jax.nn.sigmoid(jnp.concatenate(logits[2:], axis=1))
    merged = (gate_a * ya + gate_b * yb).astype(BF16)
    mix = jnp.dot(merged, wo_ref[...], preferred_element_type=F32)
    out_ref[...] = x + mod_ref[2:3, :] * mix


def _mix(x, mod3, g1, attn, os_, lses, wg, bg, wpa, wpb, wo):
    b, s, d = x.shape
    tm = TM_MIX
    tok = lambda w: pl.BlockSpec((None, tm, w), lambda bb, i: (bb, i, 0))
    const2 = lambda shape: pl.BlockSpec(shape, lambda bb, i: (0, 0))
    halves = B_OUT_W // LANES
    sub = [pl.BlockSpec((None, halves, o.shape[2], tm // o.shape[2], LANES), lambda bb, i: (bb, 0, 0, i, 0))
           for o in os_]
    return pl.pallas_call(
        _mix_kernel,
        grid=(b, s // tm),
        in_specs=[tok(d),
                  pl.BlockSpec((None, 6, d), lambda bb, i: (bb, 0, 0)),
                  const2((1, d)),
                  tok(A_Q_W),
                  *sub, *sub,
                  const2(wg.shape), const2(bg.shape), const2(wpa.shape), const2(wpb.shape),
                  const2(wo.shape)],
        out_specs=tok(d),
        out_shape=jax.ShapeDtypeStruct((b, s, d), F32),
        scratch_shapes=[pltpu.VMEM((2 * halves * len(B_GROUPS), tm, LANES), F32)],
        compiler_params=_params("parallel", "parallel"),
        name="mix",
    )(x, mod3, g1, attn, *os_, *lses, wg, bg, wpa, wpb, wo)


def _ffn_kernel(x_ref, mod_ref, g2_ref, win_ref, wout_ref, fg_ref, out_ref, *, final, bounds):
    x = x_ref[...]
    d_ff = wout_ref.shape[0]
    u = _modulated_norm(x, g2_ref[...], mod_ref[3:4, :], mod_ref[4:5, :])
    acc = jnp.zeros(x.shape, F32)
    for lo, hi in zip(bounds[:-1], bounds[1:]):
        hg = jnp.dot(u, win_ref[:, lo:hi], preferred_element_type=F32)
        hu = jnp.dot(u, win_ref[:, d_ff + lo:d_ff + hi], preferred_element_type=F32)
        act = ((hg * jax.nn.sigmoid(hg)) * hu).astype(BF16)
        acc += jnp.dot(act, wout_ref[lo:hi, :], preferred_element_type=F32)
    y = x + mod_ref[5:6, :] * acc
    if final:
        ms = jnp.mean(y * y, axis=-1, keepdims=True)
        y = (y * lax.rsqrt(ms + EPS)) * fg_ref[...]
    out_ref[...] = y


def _ffn(x, mod3, g2, win, wout, fg, final):
    b, s, d = x.shape
    tm = TM_FFN
    d_ff = wout.shape[0]
    tiles = pl.cdiv(d_ff, MXU_DIM)
    per_chunk = pl.cdiv(tiles, FFN_CHUNKS) * MXU_DIM
    bounds = tuple(min(d_ff, i * per_chunk) for i in range(FFN_CHUNKS + 1))
    tok = pl.BlockSpec((None, tm, d), lambda bb, i: (bb, i, 0))
    resident = lambda shape: pl.BlockSpec(shape, lambda bb, i: (0, 0), pipeline_mode=pl.Buffered(1))
    return pl.pallas_call(
        functools.partial(_ffn_kernel, final=final, bounds=bounds),
        grid=(b, s // tm),
        in_specs=[tok,
                  pl.BlockSpec((None, 6, d), lambda bb, i: (bb, 0, 0)),
                  pl.BlockSpec((1, d), lambda bb, i: (0, 0)),
                  resident(win.shape), resident(wout.shape),
                  pl.BlockSpec((1, d), lambda bb, i: (0, 0))],
        out_specs=tok,
        out_shape=jax.ShapeDtypeStruct((b, s, d), F32),
        compiler_params=_params("parallel", "parallel"),
        name="ffn",
    )(x, mod3, g2, win, wout, fg)


def _inv_freq(dim, theta):
    return theta ** (-jnp.arange(0, dim, 2, dtype=F32) / dim)


def _rope_tables(s):
    in_head = np.arange(LANES) % HEAD_DIM
    half = HEAD_DIM // 2
    quarter = half // 2
    inv_a = (AXIAL_THETA ** (-jnp.asarray(2 * (in_head % quarter), F32) / half))[None, :]
    ang_row = jnp.arange(s // GRID_W, dtype=jnp.int32).astype(F32)[:, None] * inv_a
    ang_col = jnp.arange(GRID_W, dtype=jnp.int32).astype(F32)[:, None] * inv_a
    by_row = (in_head < half)[None, None, :]
    first_a = ((in_head % half) < quarter)[None, None, :]

    small = lax.optimization_barrier(
        (jnp.cos(ang_row), jnp.cos(ang_col), jnp.sin(ang_row), jnp.sin(ang_col)))

    def grid(of_row, of_col):
        return jnp.where(by_row, of_row[:, None, :], of_col[None, :, :])

    a_cos = grid(small[0], small[1]).reshape(s, LANES)
    a_sin = grid(small[2], small[3])
    a_lo = jnp.where(first_a, -a_sin, 0.0).reshape(s, LANES)
    a_hi = jnp.where(first_a, 0.0, a_sin).reshape(s, LANES)
    t = jnp.arange(s, dtype=jnp.int32)
    ap = t.astype(F32)[:, None] * _inv_freq(PARTIAL_ROT_DIM, PARTIAL_THETA)[None, :]
    rest = HEAD_DIM - PARTIAL_ROT_DIM
    zp = jnp.zeros_like(ap)
    p_cos = jnp.concatenate([jnp.cos(ap), jnp.cos(ap), jnp.ones((s, rest), F32)], axis=1)
    p_lo = jnp.concatenate([-jnp.sin(ap), zp, jnp.zeros((s, rest), F32)], axis=1)
    p_hi = jnp.concatenate([zp, jnp.sin(ap), jnp.zeros((s, rest), F32)], axis=1)
    two = lambda a: jnp.tile(a, (1, LANES // HEAD_DIM))
    return a_cos, a_lo, a_hi, two(p_cos), two(p_lo), two(p_hi)


def kernel(x, c, w_ada, b_ada, norm1_g, w_qkv, q_norm_a, k_norm_a, w_proj_a, w_proj_b, w_gate,
           b_gate, w_o, norm2_g, w_ffn_in, w_ffn_out, final_norm_g):
    b, s, d = x.shape
    depth = w_ada.shape[0]
    assert s % TM_QKV == 0 and s % (B_GROUPS[-1][1] * WIN) == 0 and TK % TM_QKV == 0 and s % TK == 0
    tabs = _rope_tables(s)
    heads_per_vreg = LANES // HEAD_DIM
    lane_head = np.arange(MXU_DIM) // HEAD_DIM
    bd = jnp.asarray(lane_head[:, None] == lane_head[None, :], BF16)
    b_q_lo = A_Q_W + 2 * A_KV_W
    col_scale = jnp.ones((w_qkv.shape[-1],), F32).at[b_q_lo:b_q_lo + B_W].set(SCALE * LOG2E)
    c_pad = jnp.pad(c, ((0, 16 - b), (0, 0)))
    for l in range(depth):
        mod3 = _modulation(c_pad, w_ada, b_ada[l], l)[:b].reshape(b, 6, d)
        qg = jnp.tile(q_norm_a[l] * (SCALE * LOG2E), heads_per_vreg).reshape(1, LANES)
        kg = jnp.tile(k_norm_a[l], heads_per_vreg).reshape(1, LANES)
        g1 = norm1_g[l].reshape(1, d)
        wq = (w_qkv[l] * col_scale[None, :]).astype(BF16)
        qat, ka, vat, *groups = _qkv_project(x, mod3, g1, wq, qg, kg, bd, tabs)
        attn = _global_attention(qat, ka, vat)
        branch = [_banded_attention(qkv_g, g) for g, qkv_g in enumerate(groups)]
        x = _mix(x, mod3, g1, attn, [o for o, _ in branch], [e for _, e in branch],
                 w_gate[l].astype(BF16), b_gate[l].reshape(1, -1), w_proj_a[l].astype(BF16),
                 w_proj_b[l].astype(BF16), w_o[l].astype(BF16))
        x = _ffn(x, mod3, norm2_g[l].reshape(1, d), w_ffn_in[l].astype(BF16),
                 w_ffn_out[l].astype(BF16), final_norm_g.reshape(1, d), final=(l == depth - 1))
    return x
```

```python
import functools

import jax
import jax.numpy as jnp
import numpy as np
from jax import lax
from jax.experimental import pallas as pl
from jax.experimental.pallas import tpu as pltpu

HEAD_DIM = 64
A_Q_HEADS = 8
A_KV_HEADS = 2
A_GROUP = A_Q_HEADS // A_KV_HEADS
B_GROUPS = ((128, 1), (512, 4), (2048, 16))
B_HEADS_PER_GROUP = 4
B_HEADS = B_HEADS_PER_GROUP * len(B_GROUPS)
A_Q_W = A_Q_HEADS * HEAD_DIM
A_KV_W = A_KV_HEADS * HEAD_DIM
B_W = B_HEADS * HEAD_DIM
B_OUT_W = B_HEADS_PER_GROUP * HEAD_DIM
GRID_W = 64
AXIAL_THETA = 10000.0
PARTIAL_THETA = 500000.0
PARTIAL_ROT_DIM = HEAD_DIM // 4
EPS = 1e-6
NEG_INF = -1e30
SCALE = HEAD_DIM ** -0.5
LOG2E = 1.4426950408889634
LN2 = 0.6931471805599453

LANES = 128
MXU_DIM = 256
VMEM_LIMIT = 56 * 1024 * 1024

F32 = jnp.float32
BF16 = jnp.bfloat16

TM_QKV = 1024
QKV_ROW_SPLITS = 1
TQ = 128
TK = 1024
TM_MIX = 1024
MIX_ROW_SPLITS = 1
TM_FFN = 1024
FFN_CHUNKS = 3
FFN_ROW_SPLITS = 2
BLK = 128
WIN = 256
SWA_MAX_STORE_STRIDE = 4
VT_ROWS = HEAD_DIM + 16
GQA_UNROLL = 2
GQA_TILES = 8


def _params(*sem):
    return pltpu.CompilerParams(dimension_semantics=sem, vmem_limit_bytes=VMEM_LIMIT)


def _mod_kernel(c_ref, w_ref, b_ref, o_ref):
    c = c_ref[...]
    cond = c * jax.nn.sigmoid(c)
    c_hi = cond.astype(BF16)
    c_lo = (cond - c_hi.astype(F32)).astype(BF16)
    w = w_ref[...]
    w_hi = w.astype(BF16)
    w_lo = (w - w_hi.astype(F32)).astype(BF16)
    acc = jnp.dot(c_hi, w_hi, preferred_element_type=F32)
    acc += jnp.dot(c_hi, w_lo, preferred_element_type=F32)
    acc += jnp.dot(c_lo, w_hi, preferred_element_type=F32)
    o_ref[...] = acc + b_ref[...]


def _modulation(c_pad, w_ada, b_ada, layer):
    rows, d = c_pad.shape
    n = w_ada.shape[2]
    tn = 1024
    return pl.pallas_call(
        _mod_kernel,
        grid=(n // tn,),
        in_specs=[pl.BlockSpec((rows, d), lambda j: (0, 0)),
                  pl.BlockSpec((None, d, tn), lambda j: (layer, 0, j)),
                  pl.BlockSpec((1, tn), lambda j: (0, j))],
        out_specs=pl.BlockSpec((rows, tn), lambda j: (0, j)),
        out_shape=jax.ShapeDtypeStruct((rows, n), F32),
        compiler_params=_params("parallel"),
        name="mod",
    )(c_pad, w_ada, b_ada.reshape(1, n))


def _modulated_norm(x, gain, shift, scale):
    ms = jnp.mean(x * x, axis=-1, keepdims=True)
    return ((x * lax.rsqrt(ms + EPS)) * (gain * (1.0 + scale)) + shift).astype(BF16)


def _interleave(programs):
    done = object()
    live = []
    for prog in programs:
        live.append(prog)
        live = [p for p in live if next(p, done) is not done]
    while live:
        live = [p for p in live if next(p, done) is not done]


def _rope(y, cos, sin_lo, sin_hi, shift):
    return (y * cos + pltpu.roll(y, LANES - shift, 1) * sin_lo
            + pltpu.roll(y, shift, 1) * sin_hi)


def _qkv_kernel(x_ref, mod_ref, g1_ref, w_ref, qg_ref, kg_ref, bd_ref,
                ac_ref, asl_ref, ash_ref, pc_ref, psl_ref, psh_ref,
                qat_ref, ka_ref, vat_ref, grp0_ref, grp1_ref, grp2_ref, slab_ref):
    tm = x_ref.shape[0]
    th = tm // QKV_ROW_SPLITS
    for h in range(QKV_ROW_SPLITS):
        rows = slice(h * th, (h + 1) * th)
        u = _modulated_norm(x_ref[rows, :], g1_ref[...], mod_ref[0:1, :], mod_ref[1:2, :])

        def proj(lo, width):
            return jnp.dot(u, w_ref[:, lo:lo + width], preferred_element_type=F32)

        def head_norm_rope(wide, gain):
            out = []
            for lo in range(0, wide.shape[1], MXU_DIM):
                tile = wide[:, lo:lo + MXU_DIM]
                w = tile.shape[1]
                ss = jnp.dot((tile * tile).astype(BF16), bd_ref[:w, :w], preferred_element_type=F32)
                for c in range(w // LANES):
                    cols = slice(c * LANES, (c + 1) * LANES)
                    y = (tile[:, cols] * lax.rsqrt(ss[:, cols] * (1.0 / HEAD_DIM) + EPS)) * gain
                    out.append(_rope(y, ac_ref[rows, :], asl_ref[rows, :], ash_ref[rows, :], HEAD_DIM // 4))
            return out[0] if len(out) == 1 else jnp.concatenate(out, axis=1)

        def branch_b(g, between):
            out_ref, dil = (grp0_ref, grp1_ref, grp2_ref)[g], B_GROUPS[g][1]
            base = A_Q_W + 2 * A_KV_W
            halves = B_OUT_W // LANES
            for part in range(3):
                wide = proj(base + part * B_W + g * B_OUT_W, B_OUT_W)
                for c in range(halves):
                    chunk = wide[:, c * LANES:(c + 1) * LANES]
                    if part < 2:
                        chunk = _rope(chunk, pc_ref[rows, :], psl_ref[rows, :], psh_ref[rows, :],
                                      PARTIAL_ROT_DIM // 2)
                    slab = part * halves + c
                    if dil == 1:
                        out_ref[0, rows, slab * LANES:(slab + 1) * LANES] = chunk.astype(BF16)
                    else:
                        slab_ref[g - 1, slab, rows, :] = chunk
                between()
            if dil > 1:
                n = th // dil
                for r in range(dil):
                    for slab in range(3 * halves):
                        out_ref[r, h * n:(h + 1) * n, slab * LANES:(slab + 1) * LANES] = (
                            slab_ref[g - 1, slab, pl.ds(h * th + r, n, stride=dil), :].astype(BF16))

        raw_q = proj(0, A_Q_W)
        raw_kv = proj(A_Q_W, 2 * A_KV_W)
        q_tiles = []

        def store_q():
            qat = jnp.concatenate(q_tiles, axis=1).T.astype(BF16)
            for i in range(th // TQ):
                qat_ref[h * (th // TQ) + i] = qat[:, i * TQ:(i + 1) * TQ]

        def store_k():
            ka_ref[rows, :] = head_norm_rope(raw_kv[:, :A_KV_W], kg_ref[...]).astype(BF16)

        def store_v():
            vat = raw_kv[:, A_KV_W:].T.astype(BF16)
            pad_row = lax.broadcasted_iota(jnp.int32, (VT_ROWS - HEAD_DIM, th), 0)
            ones_row = jnp.where(pad_row == 0, 1.0, 0.0).astype(BF16)
            for hk in range(A_KV_HEADS):
                vat_ref[hk, 0:HEAD_DIM, rows] = vat[hk * HEAD_DIM:(hk + 1) * HEAD_DIM, :]
                vat_ref[hk, HEAD_DIM:VT_ROWS, rows] = ones_row

        steps = [functools.partial(lambda lo: q_tiles.append(head_norm_rope(raw_q[:, lo:lo + MXU_DIM], qg_ref[...])), lo)
                 for lo in range(0, A_Q_W, MXU_DIM)] + [store_q, store_k, store_v]

        def between():
            if steps:
                steps.pop(0)()

        for g in (2, 1, 0):
            branch_b(g, between)
        while steps:
            between()


def _qkv_project(x, mod3, g1, w_qkv, qg, kg, bd, tabs):
    b, s, d = x.shape
    tm = TM_QKV
    ncols = w_qkv.shape[1]
    row_tab = pl.BlockSpec((tm, LANES), lambda i, bb: (i, 0))
    const2 = lambda shape: pl.BlockSpec(shape, lambda i, bb: (0, 0))
    tok = lambda w: pl.BlockSpec((None, tm, w), lambda i, bb: (bb, i, 0))
    out_shapes = (
        jax.ShapeDtypeStruct((b, s // TQ, A_Q_W, TQ), BF16),
        jax.ShapeDtypeStruct((b, s, A_KV_W), BF16),
        jax.ShapeDtypeStruct((b, s // TK, A_KV_HEADS, VT_ROWS, TK), BF16),
    ) + tuple(jax.ShapeDtypeStruct((b, dil, s // dil, 3 * B_OUT_W), BF16) for _, dil in B_GROUPS)
    out_specs = (
        pl.BlockSpec((None, tm // TQ, A_Q_W, TQ), lambda i, bb: (bb, i, 0, 0)),
        tok(A_KV_W),
        pl.BlockSpec((None, None, A_KV_HEADS, VT_ROWS, tm),
                     lambda i, bb: (bb, i // (TK // tm), 0, 0, i % (TK // tm))),
    ) + tuple(pl.BlockSpec((None, dil, tm // dil, 3 * B_OUT_W), lambda i, bb: (bb, 0, i, 0))
              for _, dil in B_GROUPS)
    return pl.pallas_call(
        _qkv_kernel,
        grid=(s // tm, b),
        in_specs=[tok(d),
                  pl.BlockSpec((None, 6, d), lambda i, bb: (bb, 0, 0)),
                  const2((1, d)),
                  const2((d, ncols)),
                  const2((1, LANES)), const2((1, LANES)), const2((MXU_DIM, MXU_DIM)),
                  row_tab, row_tab, row_tab, row_tab, row_tab, row_tab],
        out_specs=out_specs,
        out_shape=out_shapes,
        scratch_shapes=[pltpu.VMEM((len(B_GROUPS) - 1, 3 * B_OUT_W // LANES, tm, LANES), F32)],
        compiler_params=_params("parallel", "parallel"),
        name="qkv",
    )(x, mod3, g1, w_qkv, qg, kg, bd, *tabs)


def _gqa_kernel(qt_ref, k_ref, vt_ref, o_ref, qx_ref, s_ref):
    nk = vt_ref.shape[0]
    ncol = A_GROUP * TQ
    units = [(t, hk) for t in range(GQA_TILES) for hk in range(A_KV_HEADS)]

    qx_ref[...] = jnp.zeros(qx_ref.shape, BF16)
    for u, (t, hk) in enumerate(units):
        for g in range(A_GROUP):
            h = hk * A_GROUP + g
            qx_ref[u, hk * HEAD_DIM:(hk + 1) * HEAD_DIM, g * TQ:(g + 1) * TQ] = (
                qt_ref[t, h * HEAD_DIM:(h + 1) * HEAD_DIM, :])

    def phase(score_unit, value_unit, m_prev):
        qx = None if score_unit is None else qx_ref[score_unit]

        def body(j, carry):
            mrun, acc = carry
            rows = pl.ds(pl.multiple_of(j * TK, TK), TK)
            if score_unit is not None:
                st = jnp.dot(k_ref[rows, :], qx, preferred_element_type=F32)
                s_ref[score_unit % 2, rows, :] = st
                mrun = jnp.maximum(mrun, jnp.max(st.reshape(TK // 8, 8, ncol), axis=0))
            if value_unit is not None:
                p = jnp.exp2((s_ref[value_unit % 2, rows, :] - m_prev).astype(BF16))
                acc = acc + jnp.dot(vt_ref[j, units[value_unit][1]], p, preferred_element_type=F32)
            return mrun, acc

        init = (jnp.full((8, ncol), NEG_INF, F32), jnp.zeros((VT_ROWS, ncol), F32))
        mrun, acc = lax.fori_loop(0, nk, body, init, unroll=GQA_UNROLL)
        return jnp.max(mrun, axis=0, keepdims=True), acc

    m = None
    for idx in range(len(units) + 1):
        score_unit = idx if idx < len(units) else None
        value_unit = idx - 1 if idx > 0 else None
        m, acc = phase(score_unit, value_unit, m)
        if value_unit is not None:
            t, hk = units[value_unit]
            out_t = acc[0:HEAD_DIM, :] / acc[HEAD_DIM:HEAD_DIM + 1, :]
            stacked = jnp.concatenate([out_t[:, g * TQ:(g + 1) * TQ] for g in range(A_GROUP)], axis=0)
            o_ref[t * TQ:(t + 1) * TQ, hk * A_GROUP * HEAD_DIM:(hk + 1) * A_GROUP * HEAD_DIM] = (
                stacked.T.astype(BF16))


def _global_attention(qat, ka, vat):
    b, nq = qat.shape[0], qat.shape[1]
    s = ka.shape[1]
    nk = vat.shape[1]
    ncol = A_GROUP * TQ
    return pl.pallas_call(
        _gqa_kernel,
        grid=(b, nq // GQA_TILES),
        in_specs=[pl.BlockSpec((None, GQA_TILES, A_Q_W, TQ), lambda bb, i: (bb, i, 0, 0)),
                  pl.BlockSpec((None, s, A_KV_W), lambda bb, i: (bb, 0, 0)),
                  pl.BlockSpec((None, nk, A_KV_HEADS, VT_ROWS, TK), lambda bb, i: (bb, 0, 0, 0, 0))],
        out_specs=pl.BlockSpec((None, GQA_TILES * TQ, A_Q_W), lambda bb, i: (bb, i, 0)),
        out_shape=jax.ShapeDtypeStruct((b, s, A_Q_W), BF16),
        scratch_shapes=[pltpu.VMEM((GQA_TILES * A_KV_HEADS, A_KV_W, ncol), BF16),
                        pltpu.VMEM((2, s, ncol), F32)],
        compiler_params=_params("parallel", "parallel"),
        name="gqa",
    )(qat, ka, vat)


def _swa_kernel(qkv_ref, o_ref, lse_ref, sc_ref, p_ref, top_ref, bias_ref, *, radius):
    dil, seq = qkv_ref.shape[0], qkv_ref.shape[1]
    nblk = seq // BLK
    halves = B_OUT_W // LANES
    lane = lax.broadcasted_iota(jnp.int32, (BLK, LANES), 1)
    first_head = lane < HEAD_DIM
    head_mask = [jnp.where(first_head, 1.0, 0.0).astype(BF16), jnp.where(first_head, 0.0, 1.0).astype(BF16)]
    ones_win = jnp.ones((WIN, LANES), BF16)

    rel = (lax.broadcasted_iota(jnp.int32, (BLK, WIN), 0)
           - lax.broadcasted_iota(jnp.int32, (BLK, WIN), 1))
    for n in range(bias_ref.shape[0]):
        bias_ref[n] = jnp.where(jnp.abs(rel + n * radius) <= radius, 0.0, NEG_INF)

    def geometry(i):
        r = i // nblk
        i0 = pl.multiple_of((i % nblk) * BLK, BLK)
        ws = pl.multiple_of(jnp.clip(i0 - radius, 0, seq - WIN), radius)
        return r, i0, ws

    def scores(i, slot):
        r, i0, ws = geometry(i)
        bias = bias_ref[(i0 - ws) // radius]
        for c in range(halves):
            q = qkv_ref[r, pl.ds(i0, BLK), c * LANES:(c + 1) * LANES]
            k = qkv_ref[r, pl.ds(ws, WIN), B_OUT_W + c * LANES:B_OUT_W + (c + 1) * LANES]
            for hh in range(2):
                sc_ref[slot, 2 * c + hh] = lax.dot_general(
                    q * head_mask[hh], k, (((1,), (1,)), ((), ())), preferred_element_type=F32) + bias

    def probabilities(slot):
        for c in range(halves):
            tops = []
            for hh in range(2):
                sc = sc_ref[slot, 2 * c + hh]
                m = jnp.max(sc, axis=-1, keepdims=True)
                p_ref[slot, 2 * c + hh] = jnp.exp2((sc - m).astype(BF16))
                tops.append(m)
            top_ref[slot, c] = jnp.where(first_head, tops[0], tops[1])

    def outputs(i, slot):
        r, i0, ws = geometry(i)
        for c in range(halves):
            v = qkv_ref[r, pl.ds(ws, WIN), 2 * B_OUT_W + c * LANES:2 * B_OUT_W + (c + 1) * LANES]
            v_ones = jnp.concatenate([v, ones_win], axis=1)
            outs = [jnp.dot(p_ref[slot, 2 * c + hh], v_ones, preferred_element_type=F32) for hh in range(2)]
            num = jnp.where(first_head, outs[0][:, :LANES], outs[1][:, :LANES])
            den = jnp.where(first_head, outs[0][:, LANES:], outs[1][:, LANES:])
            classes = o_ref.shape[1]
            step = dil // classes
            rows = pl.ds(i0 * step + r // classes, BLK, stride=step) if step > 1 else pl.ds(i0, BLK)
            o_ref[c, r % classes, rows, :] = num / den
            lse_ref[c, r % classes, rows, :] = (top_ref[slot, c] + jnp.log2(den)) * LN2

    total = dil * nblk
    scores(0, 0)
    probabilities(0)
    scores(1, 1)

    def body(i, _):
        outputs(i - 2, i % 2)
        probabilities((i - 1) % 2)
        scores(i, i % 2)
        return 0

    lax.fori_loop(2, total, body, 0, unroll=2)
    outputs(total - 2, total % 2)
    probabilities((total - 1) % 2)
    outputs(total - 1, (total - 1) % 2)


def _banded_attention(qkv, group):
    window, dil = B_GROUPS[group]
    b, _, seq, width = qkv.shape
    radius = window // 2 // dil
    halves = B_OUT_W // LANES
    classes = max(1, dil // SWA_MAX_STORE_STRIDE)
    out_spec = pl.BlockSpec((None, halves, classes, dil * seq // classes, LANES), lambda bb: (bb, 0, 0, 0, 0))
    out_shape = jax.ShapeDtypeStruct((b, halves, classes, dil * seq // classes, LANES), F32)
    return pl.pallas_call(
        functools.partial(_swa_kernel, radius=radius),
        grid=(b,),
        in_specs=[pl.BlockSpec((None, dil, seq, width), lambda bb: (bb, 0, 0, 0))],
        out_specs=(out_spec, out_spec),
        out_shape=(out_shape, out_shape),
        scratch_shapes=[pltpu.VMEM((2, B_HEADS_PER_GROUP, BLK, WIN), F32),
                        pltpu.VMEM((2, B_HEADS_PER_GROUP, BLK, WIN), BF16),
                        pltpu.VMEM((2, B_OUT_W // LANES, BLK, LANES), F32),
                        pltpu.VMEM(((WIN - BLK) // radius + 1, BLK, WIN), F32)],
        compiler_params=_params("parallel"),
        name=f"swa{group}",
    )(qkv)


def _mix_kernel(x_ref, mod_ref, g1_ref, attn_ref, o0_ref, o1_ref, o2_ref, l0_ref, l1_ref, l2_ref,
                wg_ref, bg_ref, wpa_ref, wpb_ref, wo_ref, out_ref, nat_ref):
    d = x_ref.shape[-1]
    halves = B_OUT_W // LANES
    th = x_ref.shape[0] // MIX_ROW_SPLITS

    def slab(h):
        rows = slice(h * th, (h + 1) * th)
        x = x_ref[rows, :]
        ya = jnp.dot(attn_ref[rows, :], wpa_ref[...], preferred_element_type=F32)
        yield
        u = _modulated_norm(x, g1_ref[...], mod_ref[0:1, :], mod_ref[1:2, :])

        def token_order(src, c, slot):
            classes = src.shape[1]
            if classes == 1:
                return src[c, 0, rows, :]
            n = th // classes
            for k in range(classes):
                nat_ref[slot, pl.ds(h * th + k, n, stride=classes), :] = src[c, k, h * n:(h + 1) * n, :]
            return nat_ref[slot, rows, :]

        def combine(c):
            groups = len(B_GROUPS)
            lses = [token_order(l_ref, c, (2 * c) * groups + g) for g, l_ref in enumerate((l0_ref, l1_ref, l2_ref))]
            outs = [token_order(o_ref, c, (2 * c + 1) * groups + g)
                    for g, o_ref in enumerate((o0_ref, o1_ref, o2_ref))]
            top = jnp.maximum(jnp.maximum(lses[0], lses[1]), lses[2])
            es = [jnp.exp(l - top) for l in lses]
            den = es[0] + es[1] + es[2]
            return sum((e / den) * o for e, o in zip(es, outs)).astype(BF16)

        def gate_logits(j, parts=4):
            w = 2 * d // parts
            cols = slice(j * w, (j + 1) * w)
            return jnp.dot(u, wg_ref[:, cols], preferred_element_type=F32) + bg_ref[:, cols]

        logits, comb = [], []
        for j in range(4):
            logits.append(gate_logits(j))
            yield
            if j < halves:
                comb.append(combine(j))
                yield
        yb = jnp.dot(jnp.concatenate(comb, axis=1), wpb_ref[...], preferred_element_type=F32)
        yield
        gate_a = jax.nn.sigmoid(jnp.concatenate(logits[:2], axis=1))
        gate_b = jax.nn.sigmoid(jnp.concatenate(logits[2:], axis=1))
        merged = (gate_a * ya + gate_b * yb).astype(BF16)
        yield
        mix = jnp.dot(merged, wo_ref[...], preferred_element_type=F32)
        out_ref[rows, :] = x + mod_ref[2:3, :] * mix

    _interleave([slab(h) for h in range(MIX_ROW_SPLITS)])


def _mix(x, mod3, g1, attn, os_, lses, wg, bg, wpa, wpb, wo):
    b, s, d = x.shape
    tm = TM_MIX
    tok = lambda w: pl.BlockSpec((None, tm, w), lambda bb, i: (bb, i, 0))
    const2 = lambda shape: pl.BlockSpec(shape, lambda bb, i: (0, 0))
    halves = B_OUT_W // LANES
    sub = [pl.BlockSpec((None, halves, o.shape[2], tm // o.shape[2], LANES), lambda bb, i: (bb, 0, 0, i, 0))
           for o in os_]
    return pl.pallas_call(
        _mix_kernel,
        grid=(b, s // tm),
        in_specs=[tok(d),
                  pl.BlockSpec((None, 6, d), lambda bb, i: (bb, 0, 0)),
                  const2((1, d)),
                  tok(A_Q_W),
                  *sub, *sub,
                  const2(wg.shape), const2(bg.shape), const2(wpa.shape), const2(wpb.shape),
                  const2(wo.shape)],
        out_specs=tok(d),
        out_shape=jax.ShapeDtypeStruct((b, s, d), F32),
        scratch_shapes=[pltpu.VMEM((2 * halves * len(B_GROUPS), tm, LANES), F32)],
        compiler_params=_params("parallel", "parallel"),
        name="mix",
    )(x, mod3, g1, attn, *os_, *lses, wg, bg, wpa, wpb, wo)


def _ffn_kernel(x_ref, mod_ref, g2_ref, win_ref, wout_ref, fg_ref, out_ref, *, final, bounds):
    d_ff = wout_ref.shape[0]
    th = x_ref.shape[0] // FFN_ROW_SPLITS

    def slab(h):
        rows = slice(h * th, (h + 1) * th)
        x = x_ref[rows, :]
        u = _modulated_norm(x, g2_ref[...], mod_ref[3:4, :], mod_ref[4:5, :])
        yield
        acc = jnp.zeros(x.shape, F32)
        for lo, hi in zip(bounds[:-1], bounds[1:]):
            hg = jnp.dot(u, win_ref[:, lo:hi], preferred_element_type=F32)
            yield
            hu = jnp.dot(u, win_ref[:, d_ff + lo:d_ff + hi], preferred_element_type=F32)
            yield
            act = ((hg * jax.nn.sigmoid(hg)) * hu).astype(BF16)
            acc += jnp.dot(act, wout_ref[lo:hi, :], preferred_element_type=F32)
            yield
        y = x + mod_ref[5:6, :] * acc
        if final:
            ms = jnp.mean(y * y, axis=-1, keepdims=True)
            y = (y * lax.rsqrt(ms + EPS)) * fg_ref[...]
        out_ref[rows, :] = y

    _interleave([slab(h) for h in range(FFN_ROW_SPLITS)])


def _ffn(x, mod3, g2, win, wout, fg, final):
    b, s, d = x.shape
    tm = TM_FFN
    d_ff = wout.shape[0]
    tiles = pl.cdiv(d_ff, MXU_DIM)
    per_chunk = pl.cdiv(tiles, FFN_CHUNKS) * MXU_DIM
    bounds = tuple(min(d_ff, i * per_chunk) for i in range(FFN_CHUNKS + 1))
    tok = pl.BlockSpec((None, tm, d), lambda bb, i: (bb, i, 0))
    resident = lambda shape: pl.BlockSpec(shape, lambda bb, i: (0, 0), pipeline_mode=pl.Buffered(1))
    return pl.pallas_call(
        functools.partial(_ffn_kernel, final=final, bounds=bounds),
        grid=(b, s // tm),
        in_specs=[tok,
                  pl.BlockSpec((None, 6, d), lambda bb, i: (bb, 0, 0)),
                  pl.BlockSpec((1, d), lambda bb, i: (0, 0)),
                  resident(win.shape), resident(wout.shape),
                  pl.BlockSpec((1, d), lambda bb, i: (0, 0))],
        out_specs=tok,
        out_shape=jax.ShapeDtypeStruct((b, s, d), F32),
        compiler_params=_params("parallel", "parallel"),
        name="ffn",
    )(x, mod3, g2, win, wout, fg)


def _inv_freq(dim, theta):
    return theta ** (-jnp.arange(0, dim, 2, dtype=F32) / dim)


def _rope_tables(s):
    in_head = np.arange(LANES) % HEAD_DIM
    half = HEAD_DIM // 2
    quarter = half // 2
    inv_a = (AXIAL_THETA ** (-jnp.asarray(2 * (in_head % quarter), F32) / half))[None, :]
    ang_row = jnp.arange(s // GRID_W, dtype=jnp.int32).astype(F32)[:, None] * inv_a
    ang_col = jnp.arange(GRID_W, dtype=jnp.int32).astype(F32)[:, None] * inv_a
    by_row = (in_head < half)[None, None, :]
    first_a = ((in_head % half) < quarter)[None, None, :]

    small = lax.optimization_barrier(
        (jnp.cos(ang_row), jnp.cos(ang_col), jnp.sin(ang_row), jnp.sin(ang_col)))

    def grid(of_row, of_col):
        return jnp.where(by_row, of_row[:, None, :], of_col[None, :, :])

    a_cos = grid(small[0], small[1]).reshape(s, LANES)
    a_sin = grid(small[2], small[3])
    a_lo = jnp.where(first_a, -a_sin, 0.0).reshape(s, LANES)
    a_hi = jnp.where(first_a, 0.0, a_sin).reshape(s, LANES)
    t = jnp.arange(s, dtype=jnp.int32)
    ap = t.astype(F32)[:, None] * _inv_freq(PARTIAL_ROT_DIM, PARTIAL_THETA)[None, :]
    rest = HEAD_DIM - PARTIAL_ROT_DIM
    zp = jnp.zeros_like(ap)
    p_cos = jnp.concatenate([jnp.cos(ap), jnp.cos(ap), jnp.ones((s, rest), F32)], axis=1)
    p_lo = jnp.concatenate([-jnp.sin(ap), zp, jnp.zeros((s, rest), F32)], axis=1)
    p_hi = jnp.concatenate([zp, jnp.sin(ap), jnp.zeros((s, rest), F32)], axis=1)
    two = lambda a: jnp.tile(a, (1, LANES // HEAD_DIM))
    return a_cos, a_lo, a_hi, two(p_cos), two(p_lo), two(p_hi)


def kernel(x, c, w_ada, b_ada, norm1_g, w_qkv, q_norm_a, k_norm_a, w_proj_a, w_proj_b, w_gate,
           b_gate, w_o, norm2_g, w_ffn_in, w_ffn_out, final_norm_g):
    b, s, d = x.shape
    depth = w_ada.shape[0]
    assert s % TM_QKV == 0 and s % (B_GROUPS[-1][1] * WIN) == 0 and TK % TM_QKV == 0 and s % TK == 0
    tabs = _rope_tables(s)
    heads_per_vreg = LANES // HEAD_DIM
    lane_head = np.arange(MXU_DIM) // HEAD_DIM
    bd = jnp.asarray(lane_head[:, None] == lane_head[None, :], BF16)
    b_q_lo = A_Q_W + 2 * A_KV_W
    col_scale = jnp.ones((w_qkv.shape[-1],), F32).at[b_q_lo:b_q_lo + B_W].set(SCALE * LOG2E)
    c_pad = jnp.pad(c, ((0, 16 - b), (0, 0)))
    for l in range(depth):
        mod3 = _modulation(c_pad, w_ada, b_ada[l], l)[:b].reshape(b, 6, d)
        qg = jnp.tile(q_norm_a[l] * (SCALE * LOG2E), heads_per_vreg).reshape(1, LANES)
        kg = jnp.tile(k_norm_a[l], heads_per_vreg).reshape(1, LANES)
        g1 = norm1_g[l].reshape(1, d)
        wq = (w_qkv[l] * col_scale[None, :]).astype(BF16)
        qat, ka, vat, *groups = _qkv_project(x, mod3, g1, wq, qg, kg, bd, tabs)
        attn = _global_attention(qat, ka, vat)
        branch = [_banded_attention(qkv_g, g) for g, qkv_g in enumerate(groups)]
        x = _mix(x, mod3, g1, attn, [o for o, _ in branch], [e for _, e in branch],
                 w_gate[l].astype(BF16), b_gate[l].reshape(1, -1), w_proj_a[l].astype(BF16),
                 w_proj_b[l].astype(BF16), w_o[l].astype(BF16))
        x = _ffn(x, mod3, norm2_g[l].reshape(1, d), w_ffn_in[l].astype(BF16),
                 w_ffn_out[l].astype(BF16), final_norm_g.reshape(1, d), final=(l == depth - 1))
    return x
```

```python
import functools

import jax
import jax.numpy as jnp
import numpy as np
from jax import lax
from jax.experimental import pallas as pl
from jax.experimental.pallas import tpu as pltpu

HEAD_DIM = 64
A_Q_HEADS = 8
A_KV_HEADS = 2
A_GROUP = A_Q_HEADS // A_KV_HEADS
B_GROUPS = ((128, 1), (512, 4), (2048, 16))
B_HEADS_PER_GROUP = 4
B_HEADS = B_HEADS_PER_GROUP * len(B_GROUPS)
A_Q_W = A_Q_HEADS * HEAD_DIM
A_KV_W = A_KV_HEADS * HEAD_DIM
B_W = B_HEADS * HEAD_DIM
B_OUT_W = B_HEADS_PER_GROUP * HEAD_DIM
GRID_W = 64
AXIAL_THETA = 10000.0
PARTIAL_THETA = 500000.0
PARTIAL_ROT_DIM = HEAD_DIM // 4
EPS = 1e-6
NEG_INF = -1e30
SCALE = HEAD_DIM ** -0.5
LOG2E = 1.4426950408889634
LN2 = 0.6931471805599453

LANES = 128
MXU_DIM = 256
VMEM_LIMIT = 56 * 1024 * 1024

F32 = jnp.float32
BF16 = jnp.bfloat16

TM_QKV = 1024
QKV_ROW_SPLITS = 1
TQ = 128
TK = 1024
TM_MIX = 1024
MIX_ROW_SPLITS = 1
TM_FFN = 1024
FFN_CHUNKS = 3
FFN_ROW_SPLITS = 2
BLK = 128
WIN = 256
SWA_MAX_STORE_STRIDE = 4
VT_ROWS = HEAD_DIM + 16
GQA_UNROLL = 2
GQA_TILES = 8


def _params(*sem):
    return pltpu.CompilerParams(dimension_semantics=sem, vmem_limit_bytes=VMEM_LIMIT)


def _mod_kernel(c_ref, w_ref, b_ref, o_ref):
    c = c_ref[...]
    cond = c * jax.nn.sigmoid(c)
    c_hi = cond.astype(BF16)
    c_lo = (cond - c_hi.astype(F32)).astype(BF16)
    w = w_ref[...]
    w_hi = w.astype(BF16)
    w_lo = (w - w_hi.astype(F32)).astype(BF16)
    acc = jnp.dot(c_hi, w_hi, preferred_element_type=F32)
    acc += jnp.dot(c_hi, w_lo, preferred_element_type=F32)
    acc += jnp.dot(c_lo, w_hi, preferred_element_type=F32)
    o_ref[...] = acc + b_ref[...]


def _modulation(c_pad, w_ada, b_ada, layer):
    rows, d = c_pad.shape
    n = w_ada.shape[2]
    tn = 1024
    return pl.pallas_call(
        _mod_kernel,
        grid=(n // tn,),
        in_specs=[pl.BlockSpec((rows, d), lambda j: (0, 0)),
                  pl.BlockSpec((None, d, tn), lambda j: (layer, 0, j)),
                  pl.BlockSpec((1, tn), lambda j: (0, j))],
        out_specs=pl.BlockSpec((rows, tn), lambda j: (0, j)),
        out_shape=jax.ShapeDtypeStruct((rows, n), F32),
        compiler_params=_params("parallel"),
        name="mod",
    )(c_pad, w_ada, b_ada.reshape(1, n))


def _modulated_norm(x, gain, shift, scale):
    ms = jnp.mean(x * x, axis=-1, keepdims=True)
    return ((x * lax.rsqrt(ms + EPS)) * (gain * (1.0 + scale)) + shift).astype(BF16)


def _interleave(programs):
    done = object()
    live = []
    for prog in programs:
        live.append(prog)
        live = [p for p in live if next(p, done) is not done]
    while live:
        live = [p for p in live if next(p, done) is not done]


def _rope(y, cos, sin_lo, sin_hi, shift):
    return (y * cos + pltpu.roll(y, LANES - shift, 1) * sin_lo
            + pltpu.roll(y, shift, 1) * sin_hi)


def _qkv_kernel(x_ref, mod_ref, g1_ref, w_ref, qg_ref, kg_ref, bd_ref,
                ac_ref, asl_ref, ash_ref, pc_ref, psl_ref, psh_ref,
                qat_ref, ka_ref, vat_ref, grp0_ref, grp1_ref, grp2_ref, slab_ref):
    tm = x_ref.shape[0]
    th = tm // QKV_ROW_SPLITS
    for h in range(QKV_ROW_SPLITS):
        rows = slice(h * th, (h + 1) * th)
        u = _modulated_norm(x_ref[rows, :], g1_ref[...], mod_ref[0:1, :], mod_ref[1:2, :])

        def proj(lo, width):
            return jnp.dot(u, w_ref[:, lo:lo + width], preferred_element_type=F32)

        def head_norm_rope(wide, gain):
            out = []
            for lo in range(0, wide.shape[1], MXU_DIM):
                tile = wide[:, lo:lo + MXU_DIM]
                w = tile.shape[1]
                ss = jnp.dot((tile * tile).astype(BF16), bd_ref[:w, :w], preferred_element_type=F32)
                for c in range(w // LANES):
                    cols = slice(c * LANES, (c + 1) * LANES)
                    y = (tile[:, cols] * lax.rsqrt(ss[:, cols] * (1.0 / HEAD_DIM) + EPS)) * gain
                    out.append(_rope(y, ac_ref[rows, :], asl_ref[rows, :], ash_ref[rows, :], HEAD_DIM // 4))
            return out[0] if len(out) == 1 else jnp.concatenate(out, axis=1)

        def branch_b(g, between):
            out_ref, dil = (grp0_ref, grp1_ref, grp2_ref)[g], B_GROUPS[g][1]
            base = A_Q_W + 2 * A_KV_W
            halves = B_OUT_W // LANES
            for part in range(3):
                wide = proj(base + part * B_W + g * B_OUT_W, B_OUT_W)
                for c in range(halves):
                    chunk = wide[:, c * LANES:(c + 1) * LANES]
                    if part < 2:
                        chunk = _rope(chunk, pc_ref[rows, :], psl_ref[rows, :], psh_ref[rows, :],
                                      PARTIAL_ROT_DIM // 2)
                    slab = part * halves + c
                    if dil == 1:
                        out_ref[0, rows, slab * LANES:(slab + 1) * LANES] = chunk.astype(BF16)
                    else:
                        slab_ref[g - 1, slab, rows, :] = chunk
                between()
            if dil > 1:
                n = th // dil
                for r in range(dil):
                    for slab in range(3 * halves):
                        out_ref[r, h * n:(h + 1) * n, slab * LANES:(slab + 1) * LANES] = (
                            slab_ref[g - 1, slab, pl.ds(h * th + r, n, stride=dil), :].astype(BF16))

        raw_q = proj(0, A_Q_W)
        raw_kv = proj(A_Q_W, 2 * A_KV_W)
        q_tiles = []

        def store_q():
            qat = jnp.concatenate(q_tiles, axis=1).T.astype(BF16)
            for i in range(th // TQ):
                qat_ref[h * (th // TQ) + i] = qat[:, i * TQ:(i + 1) * TQ]

        def store_k():
            ka_ref[rows, :] = head_norm_rope(raw_kv[:, :A_KV_W], kg_ref[...]).astype(BF16)

        def store_v():
            vat = raw_kv[:, A_KV_W:].T.astype(BF16)
            pad_row = lax.broadcasted_iota(jnp.int32, (VT_ROWS - HEAD_DIM, th), 0)
            ones_row = jnp.where(pad_row == 0, 1.0, 0.0).astype(BF16)
            for hk in range(A_KV_HEADS):
                vat_ref[hk, 0:HEAD_DIM, rows] = vat[hk * HEAD_DIM:(hk + 1) * HEAD_DIM, :]
                vat_ref[hk, HEAD_DIM:VT_ROWS, rows] = ones_row

        steps = [functools.partial(lambda lo: q_tiles.append(head_norm_rope(raw_q[:, lo:lo + MXU_DIM], qg_ref[...])), lo)
                 for lo in range(0, A_Q_W, MXU_DIM)] + [store_q, store_k, store_v]

        def between():
            if steps:
                steps.pop(0)()

        for g in (2, 1, 0):
            branch_b(g, between)
        while steps:
            between()


def _qkv_project(x, mod3, g1, w_qkv, qg, kg, bd, tabs):
    b, s, d = x.shape
    tm = TM_QKV
    ncols = w_qkv.shape[1]
    row_tab = pl.BlockSpec((tm, LANES), lambda i, bb: (i, 0))
    const2 = lambda shape: pl.BlockSpec(shape, lambda i, bb: (0, 0))
    tok = lambda w: pl.BlockSpec((None, tm, w), lambda i, bb: (bb, i, 0))
    out_shapes = (
        jax.ShapeDtypeStruct((b, s // TQ, A_Q_W, TQ), BF16),
        jax.ShapeDtypeStruct((b, s, A_KV_W), BF16),
        jax.ShapeDtypeStruct((b, s // TK, A_KV_HEADS, VT_ROWS, TK), BF16),
    ) + tuple(jax.ShapeDtypeStruct((b, dil, s // dil, 3 * B_OUT_W), BF16) for _, dil in B_GROUPS)
    out_specs = (
        pl.BlockSpec((None, tm // TQ, A_Q_W, TQ), lambda i, bb: (bb, i, 0, 0)),
        tok(A_KV_W),
        pl.BlockSpec((None, None, A_KV_HEADS, VT_ROWS, tm),
                     lambda i, bb: (bb, i // (TK // tm), 0, 0, i % (TK // tm))),
    ) + tuple(pl.BlockSpec((None, dil, tm // dil, 3 * B_OUT_W), lambda i, bb: (bb, 0, i, 0))
              for _, dil in B_GROUPS)
    return pl.pallas_call(
        _qkv_kernel,
        grid=(s // tm, b),
        in_specs=[tok(d),
                  pl.BlockSpec((None, 6, d), lambda i, bb: (bb, 0, 0)),
                  const2((1, d)),
                  const2((d, ncols)),
                  const2((1, LANES)), const2((1, LANES)), const2((MXU_DIM, MXU_DIM)),
                  row_tab, row_tab, row_tab, row_tab, row_tab, row_tab],
        out_specs=out_specs,
        out_shape=out_shapes,
        scratch_shapes=[pltpu.VMEM((len(B_GROUPS) - 1, 3 * B_OUT_W // LANES, tm, LANES), F32)],
        compiler_params=_params("parallel", "parallel"),
        name="qkv",
    )(x, mod3, g1, w_qkv, qg, kg, bd, *tabs)


def _gqa_kernel(qt_ref, k_ref, vt_ref, o_ref, qx_ref, s_ref):
    nk = vt_ref.shape[0]
    ncol = A_GROUP * TQ
    units = [(t, hk) for t in range(GQA_TILES) for hk in range(A_KV_HEADS)]

    qx_ref[...] = jnp.zeros(qx_ref.shape, BF16)
    for u, (t, hk) in enumerate(units):
        for g in range(A_GROUP):
            h = hk * A_GROUP + g
            qx_ref[u, hk * HEAD_DIM:(hk + 1) * HEAD_DIM, g * TQ:(g + 1) * TQ] = (
                qt_ref[t, h * HEAD_DIM:(h + 1) * HEAD_DIM, :])

    def phase(score_unit, value_unit, m_prev):
        qx = None if score_unit is None else qx_ref[score_unit]

        def body(j, carry):
            mrun, acc = carry
            rows = pl.ds(pl.multiple_of(j * TK, TK), TK)
            if score_unit is not None:
                st = jnp.dot(k_ref[rows, :], qx, preferred_element_type=F32)
                s_ref[score_unit % 2, rows, :] = st
                mrun = jnp.maximum(mrun, jnp.max(st.reshape(TK // 8, 8, ncol), axis=0))
            if value_unit is not None:
                p = jnp.exp2((s_ref[value_unit % 2, rows, :] - m_prev).astype(BF16))
                acc = acc + jnp.dot(vt_ref[j, units[value_unit][1]], p, preferred_element_type=F32)
            return mrun, acc

        init = (jnp.full((8, ncol), NEG_INF, F32), jnp.zeros((VT_ROWS, ncol), F32))
        mrun, acc = lax.fori_loop(0, nk, body, init, unroll=GQA_UNROLL)
        return jnp.max(mrun, axis=0, keepdims=True), acc

    m = None
    for idx in range(len(units) + 1):
        score_unit = idx if idx < len(units) else None
        value_unit = idx - 1 if idx > 0 else None
        m, acc = phase(score_unit, value_unit, m)
        if value_unit is not None:
            t, hk = units[value_unit]
            out_t = acc[0:HEAD_DIM, :] / acc[HEAD_DIM:HEAD_DIM + 1, :]
            stacked = jnp.concatenate([out_t[:, g * TQ:(g + 1) * TQ] for g in range(A_GROUP)], axis=0)
            o_ref[t * TQ:(t + 1) * TQ, hk * A_GROUP * HEAD_DIM:(hk + 1) * A_GROUP * HEAD_DIM] = (
                stacked.T.astype(BF16))


def _global_attention(qat, ka, vat):
    b, nq = qat.shape[0], qat.shape[1]
    s = ka.shape[1]
    nk = vat.shape[1]
    ncol = A_GROUP * TQ
    return pl.pallas_call(
        _gqa_kernel,
        grid=(b, nq // GQA_TILES),
        in_specs=[pl.BlockSpec((None, GQA_TILES, A_Q_W, TQ), lambda bb, i: (bb, i, 0, 0)),
                  pl.BlockSpec((None, s, A_KV_W), lambda bb, i: (bb, 0, 0)),
                  pl.BlockSpec((None, nk, A_KV_HEADS, VT_ROWS, TK), lambda bb, i: (bb, 0, 0, 0, 0))],
        out_specs=pl.BlockSpec((None, GQA_TILES * TQ, A_Q_W), lambda bb, i: (bb, i, 0)),
        out_shape=jax.ShapeDtypeStruct((b, s, A_Q_W), BF16),
        scratch_shapes=[pltpu.VMEM((GQA_TILES * A_KV_HEADS, A_KV_W, ncol), BF16),
                        pltpu.VMEM((2, s, ncol), F32)],
        compiler_params=_params("parallel", "parallel"),
        name="gqa",
    )(qat, ka, vat)


def _swa_kernel(qkv_ref, o_ref, lse_ref, sc_ref, p_ref, top_ref, bias_ref, *, radius):
    dil, seq = qkv_ref.shape[0], qkv_ref.shape[1]
    nblk = seq // BLK
    halves = B_OUT_W // LANES
    lane = lax.broadcasted_iota(jnp.int32, (BLK, LANES), 1)
    first_head = lane < HEAD_DIM
    head_mask = [jnp.where(first_head, 1.0, 0.0).astype(BF16), jnp.where(first_head, 0.0, 1.0).astype(BF16)]
    ones_win = jnp.ones((WIN, LANES), BF16)

    rel = (lax.broadcasted_iota(jnp.int32, (BLK, WIN), 0)
           - lax.broadcasted_iota(jnp.int32, (BLK, WIN), 1))
    for n in range(bias_ref.shape[0]):
        bias_ref[n] = jnp.where(jnp.abs(rel + n * radius) <= radius, 0.0, NEG_INF)

    def geometry(i):
        r = i // nblk
        i0 = pl.multiple_of((i % nblk) * BLK, BLK)
        ws = pl.multiple_of(jnp.clip(i0 - radius, 0, seq - WIN), radius)
        return r, i0, ws

    def scores(i, slot):
        r, i0, ws = geometry(i)
        bias = bias_ref[(i0 - ws) // radius]
        for c in range(halves):
            q = qkv_ref[r, pl.ds(i0, BLK), c * LANES:(c + 1) * LANES]
            k = qkv_ref[r, pl.ds(ws, WIN), B_OUT_W + c * LANES:B_OUT_W + (c + 1) * LANES]
            for hh in range(2):
                sc_ref[slot, 2 * c + hh] = lax.dot_general(
                    q * head_mask[hh], k, (((1,), (1,)), ((), ())), preferred_element_type=F32) + bias

    def probabilities(slot):
        for c in range(halves):
            tops = []
            for hh in range(2):
                sc = sc_ref[slot, 2 * c + hh]
                m = jnp.max(sc, axis=-1, keepdims=True)
                p_ref[slot, 2 * c + hh] = jnp.exp2((sc - m).astype(BF16))
                tops.append(m)
            top_ref[slot, c] = jnp.where(first_head, tops[0], tops[1])

    def outputs(i, slot):
        r, i0, ws = geometry(i)
        for c in range(halves):
            v = qkv_ref[r, pl.ds(ws, WIN), 2 * B_OUT_W + c * LANES:2 * B_OUT_W + (c + 1) * LANES]
            v_ones = jnp.concatenate([v, ones_win], axis=1)
            outs = [jnp.dot(p_ref[slot, 2 * c + hh], v_ones, preferred_element_type=F32) for hh in range(2)]
            num = jnp.where(first_head, outs[0][:, :LANES], outs[1][:, :LANES])
            den = jnp.where(first_head, outs[0][:, LANES:], outs[1][:, LANES:])
            classes = o_ref.shape[1]
            step = dil // classes
            rows = pl.ds(i0 * step + r // classes, BLK, stride=step) if step > 1 else pl.ds(i0, BLK)
            o_ref[c, r % classes, rows, :] = num / den
            lse_ref[c, r % classes, rows, :] = (top_ref[slot, c] + jnp.log2(den)) * LN2

    total = dil * nblk
    scores(0, 0)
    probabilities(0)
    scores(1, 1)

    def body(i, _):
        outputs(i - 2, i % 2)
        probabilities((i - 1) % 2)
        scores(i, i % 2)
        return 0

    lax.fori_loop(2, total, body, 0, unroll=2)
    outputs(total - 2, total % 2)
    probabilities((total - 1) % 2)
    outputs(total - 1, (total - 1) % 2)


def _banded_attention(qkv, group):
    window, dil = B_GROUPS[group]
    b, _, seq, width = qkv.shape
    radius = window // 2 // dil
    halves = B_OUT_W // LANES
    classes = max(1, dil // SWA_MAX_STORE_STRIDE)
    out_spec = pl.BlockSpec((None, halves, classes, dil * seq // classes, LANES), lambda bb: (bb, 0, 0, 0, 0))
    out_shape = jax.ShapeDtypeStruct((b, halves, classes, dil * seq // classes, LANES), F32)
    return pl.pallas_call(
        functools.partial(_swa_kernel, radius=radius),
        grid=(b,),
        in_specs=[pl.BlockSpec((None, dil, seq, width), lambda bb: (bb, 0, 0, 0))],
        out_specs=(out_spec, out_spec),
        out_shape=(out_shape, out_shape),
        scratch_shapes=[pltpu.VMEM((2, B_HEADS_PER_GROUP, BLK, WIN), F32),
                        pltpu.VMEM((2, B_HEADS_PER_GROUP, BLK, WIN), BF16),
                        pltpu.VMEM((2, B_OUT_W // LANES, BLK, LANES), F32),
                        pltpu.VMEM(((WIN - BLK) // radius + 1, BLK, WIN), F32)],
        compiler_params=_params("parallel"),
        name=f"swa{group}",
    )(qkv)


def _mix_kernel(x_ref, mod_ref, g1_ref, attn_ref, o0_ref, o1_ref, o2_ref, l0_ref, l1_ref, l2_ref,
                wg_ref, bg_ref, wpa_ref, wpb_ref, wo_ref, out_ref, nat_ref):
    d = x_ref.shape[-1]
    halves = B_OUT_W // LANES
    th = x_ref.shape[0] // MIX_ROW_SPLITS

    def slab(h):
        rows = slice(h * th, (h + 1) * th)
        x = x_ref[rows, :]
        ya = jnp.dot(attn_ref[rows, :], wpa_ref[...], preferred_element_type=F32)
        yield
        u = _modulated_norm(x, g1_ref[...], mod_ref[0:1, :], mod_ref[1:2, :])

        def token_order(src, c, slot):
            classes = src.shape[1]
            if classes == 1:
                return src[c, 0, rows, :]
            n = th // classes
            for k in range(classes):
                nat_ref[slot, pl.ds(h * th + k, n, stride=classes), :] = src[c, k, h * n:(h + 1) * n, :]
            return nat_ref[slot, rows, :]

        def combine(c):
            groups = len(B_GROUPS)
            lses = [token_order(l_ref, c, (2 * c) * groups + g) for g, l_ref in enumerate((l0_ref, l1_ref, l2_ref))]
            outs = [token_order(o_ref, c, (2 * c + 1) * groups + g)
                    for g, o_ref in enumerate((o0_ref, o1_ref, o2_ref))]
            top = jnp.maximum(jnp.maximum(lses[0], lses[1]), lses[2])
            es = [jnp.exp(l - top) for l in lses]
            den = es[0] + es[1] + es[2]
            return sum((e / den) * o for e, o in zip(es, outs)).astype(BF16)

        def twice_gates(j, parts=4):
            w = 2 * d // parts
            cols = slice(j * w, (j + 1) * w)
            return jnp.tanh(jnp.dot(u, wg_ref[:, cols], preferred_element_type=F32) + bg_ref[:, cols]) + 1.0

        gate, comb = [], []
        for j in range(4):
            gate.append(twice_gates(j))
            yield
            if j < halves:
                comb.append(combine(j))
                yield
            if j == 1:
                gated_a = jnp.concatenate(gate[:2], axis=1) * ya
        yb = jnp.dot(jnp.concatenate(comb, axis=1), wpb_ref[...], preferred_element_type=F32)
        yield
        merged = (gated_a + jnp.concatenate(gate[2:], axis=1) * yb).astype(BF16)
        yield
        mix = jnp.dot(merged, wo_ref[...], preferred_element_type=F32)
        out_ref[rows, :] = x + mod_ref[2:3, :] * mix

    _interleave([slab(h) for h in range(MIX_ROW_SPLITS)])


def _mix(x, mod3, g1, attn, os_, lses, wg, bg, wpa, wpb, wo):
    b, s, d = x.shape
    tm = TM_MIX
    tok = lambda w: pl.BlockSpec((None, tm, w), lambda bb, i: (bb, i, 0))
    const2 = lambda shape: pl.BlockSpec(shape, lambda bb, i: (0, 0))
    halves = B_OUT_W // LANES
    sub = [pl.BlockSpec((None, halves, o.shape[2], tm // o.shape[2], LANES), lambda bb, i: (bb, 0, 0, i, 0))
           for o in os_]
    return pl.pallas_call(
        _mix_kernel,
        grid=(b, s // tm),
        in_specs=[tok(d),
                  pl.BlockSpec((None, 6, d), lambda bb, i: (bb, 0, 0)),
                  const2((1, d)),
                  tok(A_Q_W),
                  *sub, *sub,
                  const2(wg.shape), const2(bg.shape), const2(wpa.shape), const2(wpb.shape),
                  const2(wo.shape)],
        out_specs=tok(d),
        out_shape=jax.ShapeDtypeStruct((b, s, d), F32),
        scratch_shapes=[pltpu.VMEM((2 * halves * len(B_GROUPS), tm, LANES), F32)],
        compiler_params=_params("parallel", "parallel"),
        name="mix",
    )(x, mod3, g1, attn, *os_, *lses, wg, bg, wpa, wpb, wo)


def _ffn_kernel(x_ref, mod_ref, g2_ref, win_ref, wout_ref, fg_ref, out_ref, *, final, bounds):
    d_ff = wout_ref.shape[0]
    th = x_ref.shape[0] // FFN_ROW_SPLITS

    def slab(h):
        rows = slice(h * th, (h + 1) * th)
        x = x_ref[rows, :]
        u = _modulated_norm(x, g2_ref[...], mod_ref[3:4, :], mod_ref[4:5, :])
        yield
        acc = jnp.zeros(x.shape, F32)
        for lo, hi in zip(bounds[:-1], bounds[1:]):
            hg = jnp.dot(u, win_ref[:, lo:hi], preferred_element_type=F32)
            yield
            hu = jnp.dot(u, win_ref[:, d_ff + lo:d_ff + hi], preferred_element_type=F32)
            yield
            act = ((hg * (jnp.tanh(hg) + 1.0)) * hu).astype(BF16)
            acc += jnp.dot(act, wout_ref[lo:hi, :], preferred_element_type=F32)
            yield
        y = x + mod_ref[5:6, :] * acc
        if final:
            ms = jnp.mean(y * y, axis=-1, keepdims=True)
            y = (y * lax.rsqrt(ms + EPS)) * fg_ref[...]
        out_ref[rows, :] = y

    _interleave([slab(h) for h in range(FFN_ROW_SPLITS)])


def _ffn(x, mod3, g2, win, wout, fg, final):
    b, s, d = x.shape
    tm = TM_FFN
    d_ff = wout.shape[0]
    tiles = pl.cdiv(d_ff, MXU_DIM)
    per_chunk = pl.cdiv(tiles, FFN_CHUNKS) * MXU_DIM
    bounds = tuple(min(d_ff, i * per_chunk) for i in range(FFN_CHUNKS + 1))
    tok = pl.BlockSpec((None, tm, d), lambda bb, i: (bb, i, 0))
    resident = lambda shape: pl.BlockSpec(shape, lambda bb, i: (0, 0), pipeline_mode=pl.Buffered(1))
    return pl.pallas_call(
        functools.partial(_ffn_kernel, final=final, bounds=bounds),
        grid=(b, s // tm),
        in_specs=[tok,
                  pl.BlockSpec((None, 6, d), lambda bb, i: (bb, 0, 0)),
                  pl.BlockSpec((1, d), lambda bb, i: (0, 0)),
                  resident(win.shape), resident(wout.shape),
                  pl.BlockSpec((1, d), lambda bb, i: (0, 0))],
        out_specs=tok,
        out_shape=jax.ShapeDtypeStruct((b, s, d), F32),
        compiler_params=_params("parallel", "parallel"),
        name="ffn",
    )(x, mod3, g2, win, wout, fg)


def _inv_freq(dim, theta):
    return theta ** (-jnp.arange(0, dim, 2, dtype=F32) / dim)


def _rope_tables(s):
    in_head = np.arange(LANES) % HEAD_DIM
    half = HEAD_DIM // 2
    quarter = half // 2
    inv_a = (AXIAL_THETA ** (-jnp.asarray(2 * (in_head % quarter), F32) / half))[None, :]
    ang_row = jnp.arange(s // GRID_W, dtype=jnp.int32).astype(F32)[:, None] * inv_a
    ang_col = jnp.arange(GRID_W, dtype=jnp.int32).astype(F32)[:, None] * inv_a
    by_row = (in_head < half)[None, None, :]
    first_a = ((in_head % half) < quarter)[None, None, :]

    small = lax.optimization_barrier(
        (jnp.cos(ang_row), jnp.cos(ang_col), jnp.sin(ang_row), jnp.sin(ang_col)))

    def grid(of_row, of_col):
        return jnp.where(by_row, of_row[:, None, :], of_col[None, :, :])

    a_cos = grid(small[0], small[1]).reshape(s, LANES)
    a_sin = grid(small[2], small[3])
    a_lo = jnp.where(first_a, -a_sin, 0.0).reshape(s, LANES)
    a_hi = jnp.where(first_a, 0.0, a_sin).reshape(s, LANES)
    t = jnp.arange(s, dtype=jnp.int32)
    ap = t.astype(F32)[:, None] * _inv_freq(PARTIAL_ROT_DIM, PARTIAL_THETA)[None, :]
    rest = HEAD_DIM - PARTIAL_ROT_DIM
    zp = jnp.zeros_like(ap)
    p_cos = jnp.concatenate([jnp.cos(ap), jnp.cos(ap), jnp.ones((s, rest), F32)], axis=1)
    p_lo = jnp.concatenate([-jnp.sin(ap), zp, jnp.zeros((s, rest), F32)], axis=1)
    p_hi = jnp.concatenate([zp, jnp.sin(ap), jnp.zeros((s, rest), F32)], axis=1)
    two = lambda a: jnp.tile(a, (1, LANES // HEAD_DIM))
    return a_cos, a_lo, a_hi, two(p_cos), two(p_lo), two(p_hi)


def kernel(x, c, w_ada, b_ada, norm1_g, w_qkv, q_norm_a, k_norm_a, w_proj_a, w_proj_b, w_gate,
           b_gate, w_o, norm2_g, w_ffn_in, w_ffn_out, final_norm_g):
    b, s, d = x.shape
    depth = w_ada.shape[0]
    assert s % TM_QKV == 0 and s % (B_GROUPS[-1][1] * WIN) == 0 and TK % TM_QKV == 0 and s % TK == 0
    tabs = _rope_tables(s)
    heads_per_vreg = LANES // HEAD_DIM
    lane_head = np.arange(MXU_DIM) // HEAD_DIM
    bd = jnp.asarray(lane_head[:, None] == lane_head[None, :], BF16)
    b_q_lo = A_Q_W + 2 * A_KV_W
    col_scale = jnp.ones((w_qkv.shape[-1],), F32).at[b_q_lo:b_q_lo + B_W].set(SCALE * LOG2E)
    c_pad = jnp.pad(c, ((0, 16 - b), (0, 0)))
    for l in range(depth):
        mod3 = _modulation(c_pad, w_ada, b_ada[l], l)[:b].reshape(b, 6, d)
        qg = jnp.tile(q_norm_a[l] * (SCALE * LOG2E), heads_per_vreg).reshape(1, LANES)
        kg = jnp.tile(k_norm_a[l], heads_per_vreg).reshape(1, LANES)
        g1 = norm1_g[l].reshape(1, d)
        wq = (w_qkv[l] * col_scale[None, :]).astype(BF16)
        qat, ka, vat, *groups = _qkv_project(x, mod3, g1, wq, qg, kg, bd, tabs)
        attn = _global_attention(qat, ka, vat)
        branch = [_banded_attention(qkv_g, g) for g, qkv_g in enumerate(groups)]
        x = _mix(x, mod3, g1, attn, [o for o, _ in branch], [e for _, e in branch],
                 (0.5 * w_gate[l]).astype(BF16), (0.5 * b_gate[l]).reshape(1, -1), w_proj_a[l].astype(BF16),
                 w_proj_b[l].astype(BF16), (0.5 * w_o[l]).astype(BF16))
        d_ff = w_ffn_out.shape[1]
        half_gate = jnp.where(jnp.arange(2 * d_ff) < d_ff, 0.5, 1.0).astype(F32)
        x = _ffn(x, mod3, norm2_g[l].reshape(1, d), (w_ffn_in[l] * half_gate[None, :]).astype(BF16),
                 w_ffn_out[l].astype(BF16), final_norm_g.reshape(1, d), final=(l == depth - 1))
    return x
```

```python
import functools

import jax
import jax.numpy as jnp
import numpy as np
from jax import lax
from jax.experimental import pallas as pl
from jax.experimental.pallas import tpu as pltpu

HEAD_DIM = 64
A_Q_HEADS = 8
A_KV_HEADS = 2
A_GROUP = A_Q_HEADS // A_KV_HEADS
B_GROUPS = ((128, 1), (512, 4), (2048, 16))
B_HEADS_PER_GROUP = 4
B_HEADS = B_HEADS_PER_GROUP * len(B_GROUPS)
A_Q_W = A_Q_HEADS * HEAD_DIM
A_KV_W = A_KV_HEADS * HEAD_DIM
B_W = B_HEADS * HEAD_DIM
B_OUT_W = B_HEADS_PER_GROUP * HEAD_DIM
GRID_W = 64
AXIAL_THETA = 10000.0
PARTIAL_THETA = 500000.0
PARTIAL_ROT_DIM = HEAD_DIM // 4
EPS = 1e-6
NEG_INF = -1e30
SCALE = HEAD_DIM ** -0.5
LOG2E = 1.4426950408889634
LN2 = 0.6931471805599453

LANES = 128
MXU_DIM = 256
VMEM_LIMIT = 56 * 1024 * 1024

F32 = jnp.float32
BF16 = jnp.bfloat16

TM_QKV = 1024
QKV_ROW_SPLITS = 2
TQ = 128
TK = 1024
TM_MIX = 1024
MIX_ROW_SPLITS = 1
TM_FFN = 1024
FFN_CHUNKS = 3
FFN_ROW_SPLITS = 2
BLK = 128
WIN = 256
SWA_MAX_STORE_STRIDE = 4
VT_ROWS = HEAD_DIM + 16
GQA_UNROLL = 2
GQA_TILES = 8


def _params(*sem):
    return pltpu.CompilerParams(dimension_semantics=sem, vmem_limit_bytes=VMEM_LIMIT)


def _mod_kernel(c_ref, w_ref, b_ref, o_ref):
    c = c_ref[...]
    cond = c * jax.nn.sigmoid(c)
    c_hi = cond.astype(BF16)
    c_lo = (cond - c_hi.astype(F32)).astype(BF16)
    w = w_ref[...]
    w_hi = w.astype(BF16)
    w_lo = (w - w_hi.astype(F32)).astype(BF16)
    acc = jnp.dot(c_hi, w_hi, preferred_element_type=F32)
    acc += jnp.dot(c_hi, w_lo, preferred_element_type=F32)
    acc += jnp.dot(c_lo, w_hi, preferred_element_type=F32)
    o_ref[...] = acc + b_ref[...]


def _modulation(c_pad, w_ada, b_ada, layer):
    rows, d = c_pad.shape
    n = w_ada.shape[2]
    tn = 1024
    return pl.pallas_call(
        _mod_kernel,
        grid=(n // tn,),
        in_specs=[pl.BlockSpec((rows, d), lambda j: (0, 0)),
                  pl.BlockSpec((None, d, tn), lambda j: (layer, 0, j)),
                  pl.BlockSpec((1, tn), lambda j: (0, j))],
        out_specs=pl.BlockSpec((rows, tn), lambda j: (0, j)),
        out_shape=jax.ShapeDtypeStruct((rows, n), F32),
        compiler_params=_params("parallel"),
        name="mod",
    )(c_pad, w_ada, b_ada.reshape(1, n))


def _modulated_norm(x, gain, shift, scale):
    ms = jnp.mean(x * x, axis=-1, keepdims=True)
    return ((x * lax.rsqrt(ms + EPS)) * (gain * (1.0 + scale)) + shift).astype(BF16)


def _interleave(programs):
    done = object()
    live = []
    for prog in programs:
        live.append(prog)
        live = [p for p in live if next(p, done) is not done]
    while live:
        live = [p for p in live if next(p, done) is not done]


def _rope(y, cos, sin_lo, sin_hi, shift):
    return (y * cos + pltpu.roll(y, LANES - shift, 1) * sin_lo
            + pltpu.roll(y, shift, 1) * sin_hi)


def _qkv_kernel(x_ref, mod_ref, g1_ref, w_ref, qg_ref, kg_ref, bd_ref,
                ac_ref, asl_ref, ash_ref, pc_ref, psl_ref, psh_ref,
                qat_ref, ka_ref, vat_ref, grp0_ref, grp1_ref, grp2_ref, slab_ref):
    tm = x_ref.shape[0]
    th = tm // QKV_ROW_SPLITS
    def slab(h):
        rows = slice(h * th, (h + 1) * th)
        u = _modulated_norm(x_ref[rows, :], g1_ref[...], mod_ref[0:1, :], mod_ref[1:2, :])
        yield

        def proj(lo, width):
            return jnp.dot(u, w_ref[:, lo:lo + width], preferred_element_type=F32)

        def head_norm_rope(wide, gain):
            out = []
            for lo in range(0, wide.shape[1], MXU_DIM):
                tile = wide[:, lo:lo + MXU_DIM]
                w = tile.shape[1]
                ss = jnp.dot((tile * tile).astype(BF16), bd_ref[:w, :w], preferred_element_type=F32)
                for c in range(w // LANES):
                    cols = slice(c * LANES, (c + 1) * LANES)
                    y = (tile[:, cols] * lax.rsqrt(ss[:, cols] * (1.0 / HEAD_DIM) + EPS)) * gain
                    out.append(_rope(y, ac_ref[rows, :], asl_ref[rows, :], ash_ref[rows, :], HEAD_DIM // 4))
            return out[0] if len(out) == 1 else jnp.concatenate(out, axis=1)

        def branch_b(g, between):
            out_ref, dil = (grp0_ref, grp1_ref, grp2_ref)[g], B_GROUPS[g][1]
            base = A_Q_W + 2 * A_KV_W
            halves = B_OUT_W // LANES
            for part in range(3):
                wide = proj(base + part * B_W + g * B_OUT_W, B_OUT_W)
                for c in range(halves):
                    chunk = wide[:, c * LANES:(c + 1) * LANES]
                    if part < 2:
                        chunk = _rope(chunk, pc_ref[rows, :], psl_ref[rows, :], psh_ref[rows, :],
                                      PARTIAL_ROT_DIM // 2)
                    slab = part * halves + c
                    if dil == 1:
                        out_ref[0, rows, slab * LANES:(slab + 1) * LANES] = chunk.astype(BF16)
                    else:
                        slab_ref[g - 1, slab, rows, :] = chunk
                between()
                yield
            if dil > 1:
                n = th // dil
                for r in range(dil):
                    for slab in range(3 * halves):
                        out_ref[r, h * n:(h + 1) * n, slab * LANES:(slab + 1) * LANES] = (
                            slab_ref[g - 1, slab, pl.ds(h * th + r, n, stride=dil), :].astype(BF16))
                yield

        raw_q = proj(0, A_Q_W)
        yield
        raw_kv = proj(A_Q_W, 2 * A_KV_W)
        yield
        q_tiles = []

        def store_q():
            qat = jnp.concatenate(q_tiles, axis=1).T.astype(BF16)
            for i in range(th // TQ):
                qat_ref[h * (th // TQ) + i] = qat[:, i * TQ:(i + 1) * TQ]

        def store_k():
            ka_ref[rows, :] = head_norm_rope(raw_kv[:, :A_KV_W], kg_ref[...]).astype(BF16)

        def store_v():
            vat = raw_kv[:, A_KV_W:].T.astype(BF16)
            pad_row = lax.broadcasted_iota(jnp.int32, (VT_ROWS - HEAD_DIM, th), 0)
            ones_row = jnp.where(pad_row == 0, 1.0, 0.0).astype(BF16)
            for hk in range(A_KV_HEADS):
                vat_ref[hk, 0:HEAD_DIM, rows] = vat[hk * HEAD_DIM:(hk + 1) * HEAD_DIM, :]
                vat_ref[hk, HEAD_DIM:VT_ROWS, rows] = ones_row

        steps = [functools.partial(lambda lo: q_tiles.append(head_norm_rope(raw_q[:, lo:lo + MXU_DIM], qg_ref[...])), lo)
                 for lo in range(0, A_Q_W, MXU_DIM)] + [store_q, store_k, store_v]

        def between():
            if steps:
                steps.pop(0)()

        for g in (2, 1, 0):
            yield from branch_b(g, between)
        while steps:
            between()
            yield

    _interleave([slab(h) for h in range(QKV_ROW_SPLITS)])


def _qkv_project(x, mod3, g1, w_qkv, qg, kg, bd, tabs):
    b, s, d = x.shape
    tm = TM_QKV
    ncols = w_qkv.shape[1]
    row_tab = pl.BlockSpec((tm, LANES), lambda i, bb: (i, 0))
    const2 = lambda shape: pl.BlockSpec(shape, lambda i, bb: (0, 0))
    tok = lambda w: pl.BlockSpec((None, tm, w), lambda i, bb: (bb, i, 0))
    out_shapes = (
        jax.ShapeDtypeStruct((b, s // TQ, A_Q_W, TQ), BF16),
        jax.ShapeDtypeStruct((b, s, A_KV_W), BF16),
        jax.ShapeDtypeStruct((b, s // TK, A_KV_HEADS, VT_ROWS, TK), BF16),
    ) + tuple(jax.ShapeDtypeStruct((b, dil, s // dil, 3 * B_OUT_W), BF16) for _, dil in B_GROUPS)
    out_specs = (
        pl.BlockSpec((None, tm // TQ, A_Q_W, TQ), lambda i, bb: (bb, i, 0, 0)),
        tok(A_KV_W),
        pl.BlockSpec((None, None, A_KV_HEADS, VT_ROWS, tm),
                     lambda i, bb: (bb, i // (TK // tm), 0, 0, i % (TK // tm))),
    ) + tuple(pl.BlockSpec((None, dil, tm // dil, 3 * B_OUT_W), lambda i, bb: (bb, 0, i, 0))
              for _, dil in B_GROUPS)
    return pl.pallas_call(
        _qkv_kernel,
        grid=(s // tm, b),
        in_specs=[tok(d),
                  pl.BlockSpec((None, 6, d), lambda i, bb: (bb, 0, 0)),
                  const2((1, d)),
                  const2((d, ncols)),
                  const2((1, LANES)), const2((1, LANES)), const2((MXU_DIM, MXU_DIM)),
                  row_tab, row_tab, row_tab, row_tab, row_tab, row_tab],
        out_specs=out_specs,
        out_shape=out_shapes,
        scratch_shapes=[pltpu.VMEM((len(B_GROUPS) - 1, 3 * B_OUT_W // LANES, tm, LANES), F32)],
        compiler_params=_params("parallel", "parallel"),
        name="qkv",
    )(x, mod3, g1, w_qkv, qg, kg, bd, *tabs)


def _gqa_kernel(qt_ref, k_ref, vt_ref, o_ref, qx_ref, s_ref):
    nk = vt_ref.shape[0]
    ncol = A_GROUP * TQ
    units = [(t, hk) for t in range(GQA_TILES) for hk in range(A_KV_HEADS)]

    qx_ref[...] = jnp.zeros(qx_ref.shape, BF16)
    for u, (t, hk) in enumerate(units):
        for g in range(A_GROUP):
            h = hk * A_GROUP + g
            qx_ref[u, hk * HEAD_DIM:(hk + 1) * HEAD_DIM, g * TQ:(g + 1) * TQ] = (
                qt_ref[t, h * HEAD_DIM:(h + 1) * HEAD_DIM, :])

    def phase(score_unit, value_unit, m_prev):
        qx = None if score_unit is None else qx_ref[score_unit]

        def body(j, carry):
            mrun, acc = carry
            rows = pl.ds(pl.multiple_of(j * TK, TK), TK)
            if score_unit is not None:
                st = jnp.dot(k_ref[rows, :], qx, preferred_element_type=F32)
                s_ref[score_unit % 2, rows, :] = st
                mrun = jnp.maximum(mrun, jnp.max(st.reshape(TK // 8, 8, ncol), axis=0))
            if value_unit is not None:
                p = jnp.exp2((s_ref[value_unit % 2, rows, :] - m_prev).astype(BF16))
                acc = acc + jnp.dot(vt_ref[j, units[value_unit][1]], p, preferred_element_type=F32)
            return mrun, acc

        init = (jnp.full((8, ncol), NEG_INF, F32), jnp.zeros((VT_ROWS, ncol), F32))
        mrun, acc = lax.fori_loop(0, nk, body, init, unroll=GQA_UNROLL)
        return jnp.max(mrun, axis=0, keepdims=True), acc

    m = None
    for idx in range(len(units) + 1):
        score_unit = idx if idx < len(units) else None
        value_unit = idx - 1 if idx > 0 else None
        m, acc = phase(score_unit, value_unit, m)
        if value_unit is not None:
            t, hk = units[value_unit]
            out_t = acc[0:HEAD_DIM, :] / acc[HEAD_DIM:HEAD_DIM + 1, :]
            stacked = jnp.concatenate([out_t[:, g * TQ:(g + 1) * TQ] for g in range(A_GROUP)], axis=0)
            o_ref[t * TQ:(t + 1) * TQ, hk * A_GROUP * HEAD_DIM:(hk + 1) * A_GROUP * HEAD_DIM] = (
                stacked.T.astype(BF16))


def _global_attention(qat, ka, vat):
    b, nq = qat.shape[0], qat.shape[1]
    s = ka.shape[1]
    nk = vat.shape[1]
    ncol = A_GROUP * TQ
    return pl.pallas_call(
        _gqa_kernel,
        grid=(b, nq // GQA_TILES),
        in_specs=[pl.BlockSpec((None, GQA_TILES, A_Q_W, TQ), lambda bb, i: (bb, i, 0, 0)),
                  pl.BlockSpec((None, s, A_KV_W), lambda bb, i: (bb, 0, 0)),
                  pl.BlockSpec((None, nk, A_KV_HEADS, VT_ROWS, TK), lambda bb, i: (bb, 0, 0, 0, 0))],
        out_specs=pl.BlockSpec((None, GQA_TILES * TQ, A_Q_W), lambda bb, i: (bb, i, 0)),
        out_shape=jax.ShapeDtypeStruct((b, s, A_Q_W), BF16),
        scratch_shapes=[pltpu.VMEM((GQA_TILES * A_KV_HEADS, A_KV_W, ncol), BF16),
                        pltpu.VMEM((2, s, ncol), F32)],
        compiler_params=_params("parallel", "parallel"),
        name="gqa",
    )(qat, ka, vat)


def _swa_kernel(qkv_ref, o_ref, lse_ref, sc_ref, p_ref, top_ref, bias_ref, *, radius):
    dil, seq = qkv_ref.shape[0], qkv_ref.shape[1]
    nblk = seq // BLK
    halves = B_OUT_W // LANES
    lane = lax.broadcasted_iota(jnp.int32, (BLK, LANES), 1)
    first_head = lane < HEAD_DIM
    head_mask = [jnp.where(first_head, 1.0, 0.0).astype(BF16), jnp.where(first_head, 0.0, 1.0).astype(BF16)]
    ones_win = jnp.ones((WIN, LANES), BF16)

    rel = (lax.broadcasted_iota(jnp.int32, (BLK, WIN), 0)
           - lax.broadcasted_iota(jnp.int32, (BLK, WIN), 1))
    for n in range(bias_ref.shape[0]):
        bias_ref[n] = jnp.where(jnp.abs(rel + n * radius) <= radius, 0.0, NEG_INF)

    def geometry(i):
        r = i // nblk
        i0 = pl.multiple_of((i % nblk) * BLK, BLK)
        ws = pl.multiple_of(jnp.clip(i0 - radius, 0, seq - WIN), radius)
        return r, i0, ws

    def scores(i, slot):
        r, i0, ws = geometry(i)
        bias = bias_ref[(i0 - ws) // radius]
        for c in range(halves):
            q = qkv_ref[r, pl.ds(i0, BLK), c * LANES:(c + 1) * LANES]
            k = qkv_ref[r, pl.ds(ws, WIN), B_OUT_W + c * LANES:B_OUT_W + (c + 1) * LANES]
            for hh in range(2):
                sc_ref[slot, 2 * c + hh] = lax.dot_general(
                    q * head_mask[hh], k, (((1,), (1,)), ((), ())), preferred_element_type=F32) + bias

    def probabilities(slot):
        for c in range(halves):
            tops = []
            for hh in range(2):
                sc = sc_ref[slot, 2 * c + hh]
                m = jnp.max(sc, axis=-1, keepdims=True)
                p_ref[slot, 2 * c + hh] = jnp.exp2((sc - m).astype(BF16))
                tops.append(m)
            top_ref[slot, c] = jnp.where(first_head, tops[0], tops[1])

    def outputs(i, slot):
        r, i0, ws = geometry(i)
        for c in range(halves):
            v = qkv_ref[r, pl.ds(ws, WIN), 2 * B_OUT_W + c * LANES:2 * B_OUT_W + (c + 1) * LANES]
            v_ones = jnp.concatenate([v, ones_win], axis=1)
            outs = [jnp.dot(p_ref[slot, 2 * c + hh], v_ones, preferred_element_type=F32) for hh in range(2)]
            num = jnp.where(first_head, outs[0][:, :LANES], outs[1][:, :LANES])
            den = jnp.where(first_head, outs[0][:, LANES:], outs[1][:, LANES:])
            classes = o_ref.shape[1]
            step = dil // classes
            rows = pl.ds(i0 * step + r // classes, BLK, stride=step) if step > 1 else pl.ds(i0, BLK)
            o_ref[c, r % classes, rows, :] = num / den
            lse_ref[c, r % classes, rows, :] = (top_ref[slot, c] + jnp.log2(den)) * LN2

    total = dil * nblk
    scores(0, 0)
    probabilities(0)
    scores(1, 1)

    def body(i, _):
        outputs(i - 2, i % 2)
        probabilities((i - 1) % 2)
        scores(i, i % 2)
        return 0

    lax.fori_loop(2, total, body, 0, unroll=2)
    outputs(total - 2, total % 2)
    probabilities((total - 1) % 2)
    outputs(total - 1, (total - 1) % 2)


def _banded_attention(qkv, group):
    window, dil = B_GROUPS[group]
    b, _, seq, width = qkv.shape
    radius = window // 2 // dil
    halves = B_OUT_W // LANES
    classes = max(1, dil // SWA_MAX_STORE_STRIDE)
    out_spec = pl.BlockSpec((None, halves, classes, dil * seq // classes, LANES), lambda bb: (bb, 0, 0, 0, 0))
    out_shape = jax.ShapeDtypeStruct((b, halves, classes, dil * seq // classes, LANES), F32)
    return pl.pallas_call(
        functools.partial(_swa_kernel, radius=radius),
        grid=(b,),
        in_specs=[pl.BlockSpec((None, dil, seq, width), lambda bb: (bb, 0, 0, 0))],
        out_specs=(out_spec, out_spec),
        out_shape=(out_shape, out_shape),
        scratch_shapes=[pltpu.VMEM((2, B_HEADS_PER_GROUP, BLK, WIN), F32),
                        pltpu.VMEM((2, B_HEADS_PER_GROUP, BLK, WIN), BF16),
                        pltpu.VMEM((2, B_OUT_W // LANES, BLK, LANES), F32),
                        pltpu.VMEM(((WIN - BLK) // radius + 1, BLK, WIN), F32)],
        compiler_params=_params("parallel"),
        name=f"swa{group}",
    )(qkv)


def _mix_kernel(x_ref, mod_ref, g1_ref, attn_ref, o0_ref, o1_ref, o2_ref, l0_ref, l1_ref, l2_ref,
                wg_ref, bg_ref, wpa_ref, wpb_ref, wo_ref, out_ref, nat_ref):
    d = x_ref.shape[-1]
    halves = B_OUT_W // LANES
    th = x_ref.shape[0] // MIX_ROW_SPLITS

    def slab(h):
        rows = slice(h * th, (h + 1) * th)
        x = x_ref[rows, :]
        ya = jnp.dot(attn_ref[rows, :], wpa_ref[...], preferred_element_type=F32)
        yield
        u = _modulated_norm(x, g1_ref[...], mod_ref[0:1, :], mod_ref[1:2, :])

        def token_order(src, c, slot):
            classes = src.shape[1]
            if classes == 1:
                return src[c, 0, rows, :]
            n = th // classes
            for k in range(classes):
                nat_ref[slot, pl.ds(h * th + k, n, stride=classes), :] = src[c, k, h * n:(h + 1) * n, :]
            return nat_ref[slot, rows, :]

        def combine(c):
            groups = len(B_GROUPS)
            lses = [token_order(l_ref, c, (2 * c) * groups + g) for g, l_ref in enumerate((l0_ref, l1_ref, l2_ref))]
            outs = [token_order(o_ref, c, (2 * c + 1) * groups + g)
                    for g, o_ref in enumerate((o0_ref, o1_ref, o2_ref))]
            top = jnp.maximum(jnp.maximum(lses[0], lses[1]), lses[2])
            es = [jnp.exp(l - top) for l in lses]
            den = es[0] + es[1] + es[2]
            return sum((e / den) * o for e, o in zip(es, outs)).astype(BF16)

        def twice_gates(j, parts=4):
            w = 2 * d // parts
            cols = slice(j * w, (j + 1) * w)
            return jnp.tanh(jnp.dot(u, wg_ref[:, cols], preferred_element_type=F32) + bg_ref[:, cols]) + 1.0

        gate, comb = [], []
        for j in range(4):
            gate.append(twice_gates(j))
            yield
            if j < halves:
                comb.append(combine(j))
                yield
            if j == 1:
                gated_a = jnp.concatenate(gate[:2], axis=1) * ya
        yb = jnp.dot(jnp.concatenate(comb, axis=1), wpb_ref[...], preferred_element_type=F32)
        yield
        merged = (gated_a + jnp.concatenate(gate[2:], axis=1) * yb).astype(BF16)
        yield
        mix = jnp.dot(merged, wo_ref[...], preferred_element_type=F32)
        out_ref[rows, :] = x + mod_ref[2:3, :] * mix

    _interleave([slab(h) for h in range(MIX_ROW_SPLITS)])


def _mix(x, mod3, g1, attn, os_, lses, wg, bg, wpa, wpb, wo):
    b, s, d = x.shape
    tm = TM_MIX
    tok = lambda w: pl.BlockSpec((None, tm, w), lambda bb, i: (bb, i, 0))
    const2 = lambda shape: pl.BlockSpec(shape, lambda bb, i: (0, 0))
    halves = B_OUT_W // LANES
    sub = [pl.BlockSpec((None, halves, o.shape[2], tm // o.shape[2], LANES), lambda bb, i: (bb, 0, 0, i, 0))
           for o in os_]
    return pl.pallas_call(
        _mix_kernel,
        grid=(b, s // tm),
        in_specs=[tok(d),
                  pl.BlockSpec((None, 6, d), lambda bb, i: (bb, 0, 0)),
                  const2((1, d)),
                  tok(A_Q_W),
                  *sub, *sub,
                  const2(wg.shape), const2(bg.shape), const2(wpa.shape), const2(wpb.shape),
                  const2(wo.shape)],
        out_specs=tok(d),
        out_shape=jax.ShapeDtypeStruct((b, s, d), F32),
        scratch_shapes=[pltpu.VMEM((2 * halves * len(B_GROUPS), tm, LANES), F32)],
        compiler_params=_params("parallel", "parallel"),
        name="mix",
    )(x, mod3, g1, attn, *os_, *lses, wg, bg, wpa, wpb, wo)


def _ffn_kernel(x_ref, mod_ref, g2_ref, win_ref, wout_ref, fg_ref, out_ref, *, final, bounds):
    d_ff = wout_ref.shape[0]
    th = x_ref.shape[0] // FFN_ROW_SPLITS

    def slab(h):
        rows = slice(h * th, (h + 1) * th)
        x = x_ref[rows, :]
        u = _modulated_norm(x, g2_ref[...], mod_ref[3:4, :], mod_ref[4:5, :])
        yield
        acc = jnp.zeros(x.shape, F32)
        for lo, hi in zip(bounds[:-1], bounds[1:]):
            hg = jnp.dot(u, win_ref[:, lo:hi], preferred_element_type=F32)
            yield
            hu = jnp.dot(u, win_ref[:, d_ff + lo:d_ff + hi], preferred_element_type=F32)
            yield
            act = ((hg * (jnp.tanh(hg) + 1.0)) * hu).astype(BF16)
            acc += jnp.dot(act, wout_ref[lo:hi, :], preferred_element_type=F32)
            yield
        y = x + mod_ref[5:6, :] * acc
        if final:
            ms = jnp.mean(y * y, axis=-1, keepdims=True)
            y = (y * lax.rsqrt(ms + EPS)) * fg_ref[...]
        out_ref[rows, :] = y

    _interleave([slab(h) for h in range(FFN_ROW_SPLITS)])


def _ffn(x, mod3, g2, win, wout, fg, final):
    b, s, d = x.shape
    tm = TM_FFN
    d_ff = wout.shape[0]
    tiles = pl.cdiv(d_ff, MXU_DIM)
    per_chunk = pl.cdiv(tiles, FFN_CHUNKS) * MXU_DIM
    bounds = tuple(min(d_ff, i * per_chunk) for i in range(FFN_CHUNKS + 1))
    tok = pl.BlockSpec((None, tm, d), lambda bb, i: (bb, i, 0))
    resident = lambda shape: pl.BlockSpec(shape, lambda bb, i: (0, 0), pipeline_mode=pl.Buffered(1))
    return pl.pallas_call(
        functools.partial(_ffn_kernel, final=final, bounds=bounds),
        grid=(b, s // tm),
        in_specs=[tok,
                  pl.BlockSpec((None, 6, d), lambda bb, i: (bb, 0, 0)),
                  pl.BlockSpec((1, d), lambda bb, i: (0, 0)),
                  resident(win.shape), resident(wout.shape),
                  pl.BlockSpec((1, d), lambda bb, i: (0, 0))],
        out_specs=tok,
        out_shape=jax.ShapeDtypeStruct((b, s, d), F32),
        compiler_params=_params("parallel", "parallel"),
        name="ffn",
    )(x, mod3, g2, win, wout, fg)


def _inv_freq(dim, theta):
    return theta ** (-jnp.arange(0, dim, 2, dtype=F32) / dim)


def _rope_tables(s):
    in_head = np.arange(LANES) % HEAD_DIM
    half = HEAD_DIM // 2
    quarter = half // 2
    inv_a = (AXIAL_THETA ** (-jnp.asarray(2 * (in_head % quarter), F32) / half))[None, :]
    ang_row = jnp.arange(s // GRID_W, dtype=jnp.int32).astype(F32)[:, None] * inv_a
    ang_col = jnp.arange(GRID_W, dtype=jnp.int32).astype(F32)[:, None] * inv_a
    by_row = (in_head < half)[None, None, :]
    first_a = ((in_head % half) < quarter)[None, None, :]

    small = lax.optimization_barrier(
        (jnp.cos(ang_row), jnp.cos(ang_col), jnp.sin(ang_row), jnp.sin(ang_col)))

    def grid(of_row, of_col):
        return jnp.where(by_row, of_row[:, None, :], of_col[None, :, :])

    a_cos = grid(small[0], small[1]).reshape(s, LANES)
    a_sin = grid(small[2], small[3])
    a_lo = jnp.where(first_a, -a_sin, 0.0).reshape(s, LANES)
    a_hi = jnp.where(first_a, 0.0, a_sin).reshape(s, LANES)
    t = jnp.arange(s, dtype=jnp.int32)
    ap = t.astype(F32)[:, None] * _inv_freq(PARTIAL_ROT_DIM, PARTIAL_THETA)[None, :]
    rest = HEAD_DIM - PARTIAL_ROT_DIM
    zp = jnp.zeros_like(ap)
    p_cos = jnp.concatenate([jnp.cos(ap), jnp.cos(ap), jnp.ones((s, rest), F32)], axis=1)
    p_lo = jnp.concatenate([-jnp.sin(ap), zp, jnp.zeros((s, rest), F32)], axis=1)
    p_hi = jnp.concatenate([zp, jnp.sin(ap), jnp.zeros((s, rest), F32)], axis=1)
    two = lambda a: jnp.tile(a, (1, LANES // HEAD_DIM))
    return a_cos, a_lo, a_hi, two(p_cos), two(p_lo), two(p_hi)


def kernel(x, c, w_ada, b_ada, norm1_g, w_qkv, q_norm_a, k_norm_a, w_proj_a, w_proj_b, w_gate,
           b_gate, w_o, norm2_g, w_ffn_in, w_ffn_out, final_norm_g):
    b, s, d = x.shape
    depth = w_ada.shape[0]
    assert s % TM_QKV == 0 and s % (B_GROUPS[-1][1] * WIN) == 0 and TK % TM_QKV == 0 and s % TK == 0
    tabs = _rope_tables(s)
    heads_per_vreg = LANES // HEAD_DIM
    lane_head = np.arange(MXU_DIM) // HEAD_DIM
    bd = jnp.asarray(lane_head[:, None] == lane_head[None, :], BF16)
    b_q_lo = A_Q_W + 2 * A_KV_W
    col_scale = jnp.ones((w_qkv.shape[-1],), F32).at[b_q_lo:b_q_lo + B_W].set(SCALE * LOG2E)
    c_pad = jnp.pad(c, ((0, 16 - b), (0, 0)))
    for l in range(depth):
        mod3 = _modulation(c_pad, w_ada, b_ada[l], l)[:b].reshape(b, 6, d)
        qg = jnp.tile(q_norm_a[l] * (SCALE * LOG2E), heads_per_vreg).reshape(1, LANES)
        kg = jnp.tile(k_norm_a[l], heads_per_vreg).reshape(1, LANES)
        g1 = norm1_g[l].reshape(1, d)
        wq = (w_qkv[l] * col_scale[None, :]).astype(BF16)
        qat, ka, vat, *groups = _qkv_project(x, mod3, g1, wq, qg, kg, bd, tabs)
        attn = _global_attention(qat, ka, vat)
        branch = [_banded_attention(qkv_g, g) for g, qkv_g in enumerate(groups)]
        x = _mix(x, mod3, g1, attn, [o for o, _ in branch], [e for _, e in branch],
                 (0.5 * w_gate[l]).astype(BF16), (0.5 * b_gate[l]).reshape(1, -1), w_proj_a[l].astype(BF16),
                 w_proj_b[l].astype(BF16), (0.5 * w_o[l]).astype(BF16))
        d_ff = w_ffn_out.shape[1]
        half_gate = jnp.where(jnp.arange(2 * d_ff) < d_ff, 0.5, 1.0).astype(F32)
        x = _ffn(x, mod3, norm2_g[l].reshape(1, d), (w_ffn_in[l] * half_gate[None, :]).astype(BF16),
                 w_ffn_out[l].astype(BF16), final_norm_g.reshape(1, d), final=(l == depth - 1))
    return x
```

```python
import functools

import jax
import jax.numpy as jnp
import numpy as np
from jax import lax
from jax.experimental import pallas as pl
from jax.experimental.pallas import tpu as pltpu

HEAD_DIM = 64
A_Q_HEADS = 8
A_KV_HEADS = 2
A_GROUP = A_Q_HEADS // A_KV_HEADS
B_GROUPS = ((128, 1), (512, 4), (2048, 16))
B_HEADS_PER_GROUP = 4
B_HEADS = B_HEADS_PER_GROUP * len(B_GROUPS)
A_Q_W = A_Q_HEADS * HEAD_DIM
A_KV_W = A_KV_HEADS * HEAD_DIM
B_W = B_HEADS * HEAD_DIM
B_OUT_W = B_HEADS_PER_GROUP * HEAD_DIM
GRID_W = 64
AXIAL_THETA = 10000.0
PARTIAL_THETA = 500000.0
PARTIAL_ROT_DIM = HEAD_DIM // 4
EPS = 1e-6
NEG_INF = -1e30
SCALE = HEAD_DIM ** -0.5
LOG2E = 1.4426950408889634
LN2 = 0.6931471805599453

LANES = 128
MXU_DIM = 256
VMEM_LIMIT = 56 * 1024 * 1024

F32 = jnp.float32
BF16 = jnp.bfloat16

TM_QKV = 1024
QKV_ROW_SPLITS = 2
TQ = 128
TK = 1024
TM_MIX = 1024
MIX_ROW_SPLITS = 1
TM_FFN = 1024
FFN_CHUNKS = 3
FFN_ROW_SPLITS = 2
BLK = 128
WIN = 256
SWA_MAX_STORE_STRIDE = 4
VT_ROWS = HEAD_DIM + 16
GQA_UNROLL = 2
GQA_TILES = 32


def _params(*sem):
    return pltpu.CompilerParams(dimension_semantics=sem, vmem_limit_bytes=VMEM_LIMIT)


def _mod_kernel(c_ref, w_ref, b_ref, o_ref):
    c = c_ref[...]
    cond = c * jax.nn.sigmoid(c)
    c_hi = cond.astype(BF16)
    c_lo = (cond - c_hi.astype(F32)).astype(BF16)
    w = w_ref[...]
    w_hi = w.astype(BF16)
    w_lo = (w - w_hi.astype(F32)).astype(BF16)
    acc = jnp.dot(c_hi, w_hi, preferred_element_type=F32)
    acc += jnp.dot(c_hi, w_lo, preferred_element_type=F32)
    acc += jnp.dot(c_lo, w_hi, preferred_element_type=F32)
    o_ref[...] = acc + b_ref[...]


def _modulation(c_pad, w_ada, b_ada, layer):
    rows, d = c_pad.shape
    n = w_ada.shape[2]
    tn = 1024
    return pl.pallas_call(
        _mod_kernel,
        grid=(n // tn,),
        in_specs=[pl.BlockSpec((rows, d), lambda j: (0, 0)),
                  pl.BlockSpec((None, d, tn), lambda j: (layer, 0, j)),
                  pl.BlockSpec((1, tn), lambda j: (0, j))],
        out_specs=pl.BlockSpec((rows, tn), lambda j: (0, j)),
        out_shape=jax.ShapeDtypeStruct((rows, n), F32),
        compiler_params=_params("parallel"),
        name="mod",
    )(c_pad, w_ada, b_ada.reshape(1, n))


def _modulated_norm(x, gain, shift, scale):
    ms = jnp.mean(x * x, axis=-1, keepdims=True)
    return ((x * lax.rsqrt(ms + EPS)) * (gain * (1.0 + scale)) + shift).astype(BF16)


def _interleave(programs):
    done = object()
    live = []
    for prog in programs:
        live.append(prog)
        live = [p for p in live if next(p, done) is not done]
    while live:
        live = [p for p in live if next(p, done) is not done]


def _rope(y, cos, sin_lo, sin_hi, shift):
    return (y * cos + pltpu.roll(y, LANES - shift, 1) * sin_lo
            + pltpu.roll(y, shift, 1) * sin_hi)


def _qkv_kernel(x_ref, mod_ref, g1_ref, w_ref, qg_ref, kg_ref, bd_ref,
                ac_ref, asl_ref, ash_ref, pc_ref, psl_ref, psh_ref,
                qat_ref, ka_ref, vat_ref, grp0_ref, grp1_ref, grp2_ref, slab_ref):
    tm = x_ref.shape[0]
    th = tm // QKV_ROW_SPLITS
    def slab(h):
        rows = slice(h * th, (h + 1) * th)
        u = _modulated_norm(x_ref[rows, :], g1_ref[...], mod_ref[0:1, :], mod_ref[1:2, :])
        yield

        def proj(lo, width):
            return jnp.dot(u, w_ref[:, lo:lo + width], preferred_element_type=F32)

        def head_norm_rope(wide, gain):
            out = []
            for lo in range(0, wide.shape[1], MXU_DIM):
                tile = wide[:, lo:lo + MXU_DIM]
                w = tile.shape[1]
                ss = jnp.dot((tile * tile).astype(BF16), bd_ref[:w, :w], preferred_element_type=F32)
                for c in range(w // LANES):
                    cols = slice(c * LANES, (c + 1) * LANES)
                    y = (tile[:, cols] * lax.rsqrt(ss[:, cols] * (1.0 / HEAD_DIM) + EPS)) * gain
                    out.append(_rope(y, ac_ref[rows, :], asl_ref[rows, :], ash_ref[rows, :], HEAD_DIM // 4))
            return out[0] if len(out) == 1 else jnp.concatenate(out, axis=1)

        def branch_b(g, between):
            out_ref, dil = (grp0_ref, grp1_ref, grp2_ref)[g], B_GROUPS[g][1]
            base = A_Q_W + 2 * A_KV_W
            halves = B_OUT_W // LANES
            for part in range(3):
                wide = proj(base + part * B_W + g * B_OUT_W, B_OUT_W)
                for c in range(halves):
                    chunk = wide[:, c * LANES:(c + 1) * LANES]
                    if part < 2:
                        chunk = _rope(chunk, pc_ref[rows, :], psl_ref[rows, :], psh_ref[rows, :],
                                      PARTIAL_ROT_DIM // 2)
                    slab = part * halves + c
                    if dil == 1:
                        out_ref[0, rows, slab * LANES:(slab + 1) * LANES] = chunk.astype(BF16)
                    else:
                        slab_ref[g - 1, slab, rows, :] = chunk
                between()
                yield
            if dil > 1:
                n = th // dil
                for r in range(dil):
                    for slab in range(3 * halves):
                        out_ref[r, h * n:(h + 1) * n, slab * LANES:(slab + 1) * LANES] = (
                            slab_ref[g - 1, slab, pl.ds(h * th + r, n, stride=dil), :].astype(BF16))
                yield

        raw_q = proj(0, A_Q_W)
        yield
        raw_kv = proj(A_Q_W, 2 * A_KV_W)
        yield
        q_tiles = []

        def store_q():
            qat = jnp.concatenate(q_tiles, axis=1).T.astype(BF16)
            for i in range(th // TQ):
                qat_ref[h * (th // TQ) + i] = qat[:, i * TQ:(i + 1) * TQ]

        def store_k():
            ka_ref[rows, :] = head_norm_rope(raw_kv[:, :A_KV_W], kg_ref[...]).astype(BF16)

        def store_v():
            vat = raw_kv[:, A_KV_W:].T.astype(BF16)
            pad_row = lax.broadcasted_iota(jnp.int32, (VT_ROWS - HEAD_DIM, th), 0)
            ones_row = jnp.where(pad_row == 0, 1.0, 0.0).astype(BF16)
            for hk in range(A_KV_HEADS):
                vat_ref[hk, 0:HEAD_DIM, rows] = vat[hk * HEAD_DIM:(hk + 1) * HEAD_DIM, :]
                vat_ref[hk, HEAD_DIM:VT_ROWS, rows] = ones_row

        steps = [functools.partial(lambda lo: q_tiles.append(head_norm_rope(raw_q[:, lo:lo + MXU_DIM], qg_ref[...])), lo)
                 for lo in range(0, A_Q_W, MXU_DIM)] + [store_q, store_k, store_v]

        def between():
            if steps:
                steps.pop(0)()

        for g in (2, 1, 0):
            yield from branch_b(g, between)
        while steps:
            between()
            yield

    _interleave([slab(h) for h in range(QKV_ROW_SPLITS)])


def _qkv_project(x, mod3, g1, w_qkv, qg, kg, bd, tabs):
    b, s, d = x.shape
    tm = TM_QKV
    ncols = w_qkv.shape[1]
    row_tab = pl.BlockSpec((tm, LANES), lambda i, bb: (i, 0))
    const2 = lambda shape: pl.BlockSpec(shape, lambda i, bb: (0, 0))
    tok = lambda w: pl.BlockSpec((None, tm, w), lambda i, bb: (bb, i, 0))
    out_shapes = (
        jax.ShapeDtypeStruct((b, s // TQ, A_Q_W, TQ), BF16),
        jax.ShapeDtypeStruct((b, s, A_KV_W), BF16),
        jax.ShapeDtypeStruct((b, s // TK, A_KV_HEADS, VT_ROWS, TK), BF16),
    ) + tuple(jax.ShapeDtypeStruct((b, dil, s // dil, 3 * B_OUT_W), BF16) for _, dil in B_GROUPS)
    out_specs = (
        pl.BlockSpec((None, tm // TQ, A_Q_W, TQ), lambda i, bb: (bb, i, 0, 0)),
        tok(A_KV_W),
        pl.BlockSpec((None, None, A_KV_HEADS, VT_ROWS, tm),
                     lambda i, bb: (bb, i // (TK // tm), 0, 0, i % (TK // tm))),
    ) + tuple(pl.BlockSpec((None, dil, tm // dil, 3 * B_OUT_W), lambda i, bb: (bb, 0, i, 0))
              for _, dil in B_GROUPS)
    return pl.pallas_call(
        _qkv_kernel,
        grid=(s // tm, b),
        in_specs=[tok(d),
                  pl.BlockSpec((None, 6, d), lambda i, bb: (bb, 0, 0)),
                  const2((1, d)),
                  const2((d, ncols)),
                  const2((1, LANES)), const2((1, LANES)), const2((MXU_DIM, MXU_DIM)),
                  row_tab, row_tab, row_tab, row_tab, row_tab, row_tab],
        out_specs=out_specs,
        out_shape=out_shapes,
        scratch_shapes=[pltpu.VMEM((len(B_GROUPS) - 1, 3 * B_OUT_W // LANES, tm, LANES), F32)],
        compiler_params=_params("parallel", "parallel"),
        name="qkv",
    )(x, mod3, g1, w_qkv, qg, kg, bd, *tabs)


def _gqa_kernel(qt_ref, k_ref, vt_ref, o_ref, qx_ref, s0_ref, s1_ref):
    nk = vt_ref.shape[0]
    ncol = A_GROUP * TQ
    n_units = GQA_TILES * A_KV_HEADS

    qx_ref[...] = jnp.zeros(qx_ref.shape, BF16)
    for t in range(GQA_TILES):
        for hk in range(A_KV_HEADS):
            for g in range(A_GROUP):
                h = hk * A_GROUP + g
                qx_ref[t * A_KV_HEADS + hk, hk * HEAD_DIM:(hk + 1) * HEAD_DIM, g * TQ:(g + 1) * TQ] = (
                    qt_ref[t, h * HEAD_DIM:(h + 1) * HEAD_DIM, :])

    s_refs = (s0_ref, s1_ref)

    def phase(score_unit, value_unit, parity, m_prev):
        qx = None if score_unit is None else qx_ref[score_unit]

        def body(j, carry):
            mrun, acc = carry
            rows = pl.ds(pl.multiple_of(j * TK, TK), TK)
            if score_unit is not None:
                st = jnp.dot(k_ref[rows, :], qx, preferred_element_type=F32)
                s_refs[parity][rows, :] = st
                mrun = jnp.maximum(mrun, jnp.max(st.reshape(TK // 8, 8, ncol), axis=0))
            if value_unit is not None:
                p = jnp.exp2((s_refs[1 - parity][rows, :] - m_prev).astype(BF16))
                acc = acc + jnp.dot(vt_ref[j, 1 - parity], p, preferred_element_type=F32)
            return mrun, acc

        init = (jnp.full((8, ncol), NEG_INF, F32), jnp.zeros((VT_ROWS, ncol), F32))
        mrun, acc = lax.fori_loop(0, nk, body, init, unroll=GQA_UNROLL)
        if value_unit is not None:
            out_t = acc[0:HEAD_DIM, :] / acc[HEAD_DIM:HEAD_DIM + 1, :]
            stacked = jnp.concatenate([out_t[:, g * TQ:(g + 1) * TQ] for g in range(A_GROUP)], axis=0)
            rows = pl.ds(pl.multiple_of((value_unit // A_KV_HEADS) * TQ, TQ), TQ)
            o_ref[1 - parity, rows, :] = stacked.T.astype(BF16)
        return jnp.max(mrun, axis=0, keepdims=True)

    assert A_KV_HEADS == 2 and n_units % 2 == 0
    m = phase(0, None, 0, None)

    def unit_pair(i, m_prev):
        m_odd = phase(2 * i + 1, 2 * i, 1, m_prev)
        return phase(2 * i + 2, 2 * i + 1, 0, m_odd)

    m = lax.fori_loop(0, n_units // 2 - 1, unit_pair, m)
    m = phase(n_units - 1, n_units - 2, 1, m)
    phase(None, n_units - 1, 0, m)


def _global_attention(qat, ka, vat):
    b, nq = qat.shape[0], qat.shape[1]
    s = ka.shape[1]
    nk = vat.shape[1]
    ncol = A_GROUP * TQ
    group_w = A_GROUP * HEAD_DIM
    return pl.pallas_call(
        _gqa_kernel,
        grid=(b, nq // GQA_TILES),
        in_specs=[pl.BlockSpec((None, GQA_TILES, A_Q_W, TQ), lambda bb, i: (bb, i, 0, 0)),
                  pl.BlockSpec((None, s, A_KV_W), lambda bb, i: (bb, 0, 0)),
                  pl.BlockSpec((None, nk, A_KV_HEADS, VT_ROWS, TK), lambda bb, i: (bb, 0, 0, 0, 0))],
        out_specs=pl.BlockSpec((None, A_KV_HEADS, GQA_TILES * TQ, group_w), lambda bb, i: (bb, 0, i, 0)),
        out_shape=jax.ShapeDtypeStruct((b, A_KV_HEADS, s, group_w), BF16),
        scratch_shapes=[pltpu.VMEM((GQA_TILES * A_KV_HEADS, A_KV_W, ncol), BF16),
                        pltpu.VMEM((s, ncol), F32), pltpu.VMEM((s, ncol), F32)],
        compiler_params=_params("parallel", "parallel"),
        name="gqa",
    )(qat, ka, vat)


def _swa_kernel(qkv_ref, o_ref, lse_ref, sc_ref, p_ref, top_ref, bias_ref, *, radius):
    dil, seq = qkv_ref.shape[0], qkv_ref.shape[1]
    nblk = seq // BLK
    halves = B_OUT_W // LANES
    lane = lax.broadcasted_iota(jnp.int32, (BLK, LANES), 1)
    first_head = lane < HEAD_DIM
    head_mask = [jnp.where(first_head, 1.0, 0.0).astype(BF16), jnp.where(first_head, 0.0, 1.0).astype(BF16)]
    ones_win = jnp.ones((WIN, LANES), BF16)

    rel = (lax.broadcasted_iota(jnp.int32, (BLK, WIN), 0)
           - lax.broadcasted_iota(jnp.int32, (BLK, WIN), 1))
    for n in range(bias_ref.shape[0]):
        bias_ref[n] = jnp.where(jnp.abs(rel + n * radius) <= radius, 0.0, NEG_INF)

    def geometry(i):
        r = i // nblk
        i0 = pl.multiple_of((i % nblk) * BLK, BLK)
        ws = pl.multiple_of(jnp.clip(i0 - radius, 0, seq - WIN), radius)
        return r, i0, ws

    def scores(i, slot):
        r, i0, ws = geometry(i)
        bias = bias_ref[(i0 - ws) // radius]
        for c in range(halves):
            q = qkv_ref[r, pl.ds(i0, BLK), c * LANES:(c + 1) * LANES]
            k = qkv_ref[r, pl.ds(ws, WIN), B_OUT_W + c * LANES:B_OUT_W + (c + 1) * LANES]
            for hh in range(2):
                sc_ref[slot, 2 * c + hh] = lax.dot_general(
                    q * head_mask[hh], k, (((1,), (1,)), ((), ())), preferred_element_type=F32) + bias

    def probabilities(slot):
        for c in range(halves):
            tops = []
            for hh in range(2):
                sc = sc_ref[slot, 2 * c + hh]
                m = jnp.max(sc, axis=-1, keepdims=True)
                p_ref[slot, 2 * c + hh] = jnp.exp2((sc - m).astype(BF16))
                tops.append(m)
            top_ref[slot, c] = jnp.where(first_head, tops[0], tops[1])

    def outputs(i, slot):
        r, i0, ws = geometry(i)
        for c in range(halves):
            v = qkv_ref[r, pl.ds(ws, WIN), 2 * B_OUT_W + c * LANES:2 * B_OUT_W + (c + 1) * LANES]
            v_ones = jnp.concatenate([v, ones_win], axis=1)
            outs = [jnp.dot(p_ref[slot, 2 * c + hh], v_ones, preferred_element_type=F32) for hh in range(2)]
            num = jnp.where(first_head, outs[0][:, :LANES], outs[1][:, :LANES])
            den = jnp.where(first_head, outs[0][:, LANES:], outs[1][:, LANES:])
            classes = o_ref.shape[1]
            step = dil // classes
            rows = pl.ds(i0 * step + r // classes, BLK, stride=step) if step > 1 else pl.ds(i0, BLK)
            o_ref[c, r % classes, rows, :] = num / den
            lse_ref[c, r % classes, rows, :] = (top_ref[slot, c] + jnp.log2(den)) * LN2

    total = dil * nblk
    scores(0, 0)
    probabilities(0)
    scores(1, 1)

    def body(i, _):
        outputs(i - 2, i % 2)
        probabilities((i - 1) % 2)
        scores(i, i % 2)
        return 0

    lax.fori_loop(2, total, body, 0, unroll=2)
    outputs(total - 2, total % 2)
    probabilities((total - 1) % 2)
    outputs(total - 1, (total - 1) % 2)


def _banded_attention(qkv, group):
    window, dil = B_GROUPS[group]
    b, _, seq, width = qkv.shape
    radius = window // 2 // dil
    halves = B_OUT_W // LANES
    classes = max(1, dil // SWA_MAX_STORE_STRIDE)
    out_spec = pl.BlockSpec((None, halves, classes, dil * seq // classes, LANES), lambda bb: (bb, 0, 0, 0, 0))
    out_shape = jax.ShapeDtypeStruct((b, halves, classes, dil * seq // classes, LANES), F32)
    return pl.pallas_call(
        functools.partial(_swa_kernel, radius=radius),
        grid=(b,),
        in_specs=[pl.BlockSpec((None, dil, seq, width), lambda bb: (bb, 0, 0, 0))],
        out_specs=(out_spec, out_spec),
        out_shape=(out_shape, out_shape),
        scratch_shapes=[pltpu.VMEM((2, B_HEADS_PER_GROUP, BLK, WIN), F32),
                        pltpu.VMEM((2, B_HEADS_PER_GROUP, BLK, WIN), BF16),
                        pltpu.VMEM((2, B_OUT_W // LANES, BLK, LANES), F32),
                        pltpu.VMEM(((WIN - BLK) // radius + 1, BLK, WIN), F32)],
        compiler_params=_params("parallel"),
        name=f"swa{group}",
    )(qkv)


def _mix_kernel(x_ref, mod_ref, g1_ref, attn_ref, o0_ref, o1_ref, o2_ref, l0_ref, l1_ref, l2_ref,
                wg_ref, bg_ref, wpa_ref, wpb_ref, wo_ref, out_ref, nat_ref):
    d = x_ref.shape[-1]
    halves = B_OUT_W // LANES
    th = x_ref.shape[0] // MIX_ROW_SPLITS

    def slab(h):
        rows = slice(h * th, (h + 1) * th)
        x = x_ref[rows, :]
        group_w = attn_ref.shape[-1]
        ya = sum(jnp.dot(attn_ref[hk, rows, :], wpa_ref[hk * group_w:(hk + 1) * group_w, :],
                         preferred_element_type=F32) for hk in range(attn_ref.shape[0]))
        yield
        u = _modulated_norm(x, g1_ref[...], mod_ref[0:1, :], mod_ref[1:2, :])

        def token_order(src, c, slot):
            classes = src.shape[1]
            if classes == 1:
                return src[c, 0, rows, :]
            n = th // classes
            for k in range(classes):
                nat_ref[slot, pl.ds(h * th + k, n, stride=classes), :] = src[c, k, h * n:(h + 1) * n, :]
            return nat_ref[slot, rows, :]

        def combine(c):
            groups = len(B_GROUPS)
            lses = [token_order(l_ref, c, (2 * c) * groups + g) for g, l_ref in enumerate((l0_ref, l1_ref, l2_ref))]
            outs = [token_order(o_ref, c, (2 * c + 1) * groups + g)
                    for g, o_ref in enumerate((o0_ref, o1_ref, o2_ref))]
            top = jnp.maximum(jnp.maximum(lses[0], lses[1]), lses[2])
            es = [jnp.exp(l - top) for l in lses]
            den = es[0] + es[1] + es[2]
            return sum((e / den) * o for e, o in zip(es, outs)).astype(BF16)

        def twice_gates(j, parts=4):
            w = 2 * d // parts
            cols = slice(j * w, (j + 1) * w)
            return jnp.tanh(jnp.dot(u, wg_ref[:, cols], preferred_element_type=F32) + bg_ref[:, cols]) + 1.0

        gate, comb = [], []
        for j in range(4):
            gate.append(twice_gates(j))
            yield
            if j < halves:
                comb.append(combine(j))
                yield
            if j == 1:
                gated_a = jnp.concatenate(gate[:2], axis=1) * ya
        yb = jnp.dot(jnp.concatenate(comb, axis=1), wpb_ref[...], preferred_element_type=F32)
        yield
        merged = (gated_a + jnp.concatenate(gate[2:], axis=1) * yb).astype(BF16)
        yield
        mix = jnp.dot(merged, wo_ref[...], preferred_element_type=F32)
        out_ref[rows, :] = x + mod_ref[2:3, :] * mix

    _interleave([slab(h) for h in range(MIX_ROW_SPLITS)])


def _mix(x, mod3, g1, attn, os_, lses, wg, bg, wpa, wpb, wo):
    b, s, d = x.shape
    tm = TM_MIX
    tok = lambda w: pl.BlockSpec((None, tm, w), lambda bb, i: (bb, i, 0))
    const2 = lambda shape: pl.BlockSpec(shape, lambda bb, i: (0, 0))
    halves = B_OUT_W // LANES
    sub = [pl.BlockSpec((None, halves, o.shape[2], tm // o.shape[2], LANES), lambda bb, i: (bb, 0, 0, i, 0))
           for o in os_]
    return pl.pallas_call(
        _mix_kernel,
        grid=(b, s // tm),
        in_specs=[tok(d),
                  pl.BlockSpec((None, 6, d), lambda bb, i: (bb, 0, 0)),
                  const2((1, d)),
                  pl.BlockSpec((None, attn.shape[1], tm, attn.shape[3]), lambda bb, i: (bb, 0, i, 0)),
                  *sub, *sub,
                  const2(wg.shape), const2(bg.shape), const2(wpa.shape), const2(wpb.shape),
                  const2(wo.shape)],
        out_specs=tok(d),
        out_shape=jax.ShapeDtypeStruct((b, s, d), F32),
        scratch_shapes=[pltpu.VMEM((2 * halves * len(B_GROUPS), tm, LANES), F32)],
        compiler_params=_params("parallel", "parallel"),
        name="mix",
    )(x, mod3, g1, attn, *os_, *lses, wg, bg, wpa, wpb, wo)


def _ffn_kernel(x_ref, mod_ref, g2_ref, win_ref, wout_ref, fg_ref, out_ref, *, final, bounds):
    d_ff = wout_ref.shape[0]
    th = x_ref.shape[0] // FFN_ROW_SPLITS

    def slab(h):
        rows = slice(h * th, (h + 1) * th)
        x = x_ref[rows, :]
        u = _modulated_norm(x, g2_ref[...], mod_ref[3:4, :], mod_ref[4:5, :])
        yield
        acc = jnp.zeros(x.shape, F32)
        for lo, hi in zip(bounds[:-1], bounds[1:]):
            hg = jnp.dot(u, win_ref[:, lo:hi], preferred_element_type=F32)
            yield
            hu = jnp.dot(u, win_ref[:, d_ff + lo:d_ff + hi], preferred_element_type=F32)
            yield
            act = ((hg * (jnp.tanh(hg) + 1.0)) * hu).astype(BF16)
            acc += jnp.dot(act, wout_ref[lo:hi, :], preferred_element_type=F32)
            yield
        y = x + mod_ref[5:6, :] * acc
        if final:
            ms = jnp.mean(y * y, axis=-1, keepdims=True)
            y = (y * lax.rsqrt(ms + EPS)) * fg_ref[...]
        out_ref[rows, :] = y

    _interleave([slab(h) for h in range(FFN_ROW_SPLITS)])


def _ffn(x, mod3, g2, win, wout, fg, final):
    b, s, d = x.shape
    tm = TM_FFN
    d_ff = wout.shape[0]
    tiles = pl.cdiv(d_ff, MXU_DIM)
    per_chunk = pl.cdiv(tiles, FFN_CHUNKS) * MXU_DIM
    bounds = tuple(min(d_ff, i * per_chunk) for i in range(FFN_CHUNKS + 1))
    tok = pl.BlockSpec((None, tm, d), lambda bb, i: (bb, i, 0))
    resident = lambda shape: pl.BlockSpec(shape, lambda bb, i: (0, 0), pipeline_mode=pl.Buffered(1))
    return pl.pallas_call(
        functools.partial(_ffn_kernel, final=final, bounds=bounds),
        grid=(b, s // tm),
        in_specs=[tok,
                  pl.BlockSpec((None, 6, d), lambda bb, i: (bb, 0, 0)),
                  pl.BlockSpec((1, d), lambda bb, i: (0, 0)),
                  resident(win.shape), resident(wout.shape),
                  pl.BlockSpec((1, d), lambda bb, i: (0, 0))],
        out_specs=tok,
        out_shape=jax.ShapeDtypeStruct((b, s, d), F32),
        compiler_params=_params("parallel", "parallel"),
        name="ffn",
    )(x, mod3, g2, win, wout, fg)


def _inv_freq(dim, theta):
    return theta ** (-jnp.arange(0, dim, 2, dtype=F32) / dim)


def _rope_tables(s):
    in_head = np.arange(LANES) % HEAD_DIM
    half = HEAD_DIM // 2
    quarter = half // 2
    inv_a = (AXIAL_THETA ** (-jnp.asarray(2 * (in_head % quarter), F32) / half))[None, :]
    ang_row = jnp.arange(s // GRID_W, dtype=jnp.int32).astype(F32)[:, None] * inv_a
    ang_col = jnp.arange(GRID_W, dtype=jnp.int32).astype(F32)[:, None] * inv_a
    by_row = (in_head < half)[None, None, :]
    first_a = ((in_head % half) < quarter)[None, None, :]

    small = lax.optimization_barrier(
        (jnp.cos(ang_row), jnp.cos(ang_col), jnp.sin(ang_row), jnp.sin(ang_col)))

    def grid(of_row, of_col):
        return jnp.where(by_row, of_row[:, None, :], of_col[None, :, :])

    a_cos = grid(small[0], small[1]).reshape(s, LANES)
    a_sin = grid(small[2], small[3])
    a_lo = jnp.where(first_a, -a_sin, 0.0).reshape(s, LANES)
    a_hi = jnp.where(first_a, 0.0, a_sin).reshape(s, LANES)
    t = jnp.arange(s, dtype=jnp.int32)
    ap = t.astype(F32)[:, None] * _inv_freq(PARTIAL_ROT_DIM, PARTIAL_THETA)[None, :]
    rest = HEAD_DIM - PARTIAL_ROT_DIM
    zp = jnp.zeros_like(ap)
    p_cos = jnp.concatenate([jnp.cos(ap), jnp.cos(ap), jnp.ones((s, rest), F32)], axis=1)
    p_lo = jnp.concatenate([-jnp.sin(ap), zp, jnp.zeros((s, rest), F32)], axis=1)
    p_hi = jnp.concatenate([zp, jnp.sin(ap), jnp.zeros((s, rest), F32)], axis=1)
    two = lambda a: jnp.tile(a, (1, LANES // HEAD_DIM))
    return a_cos, a_lo, a_hi, two(p_cos), two(p_lo), two(p_hi)


def kernel(x, c, w_ada, b_ada, norm1_g, w_qkv, q_norm_a, k_norm_a, w_proj_a, w_proj_b, w_gate,
           b_gate, w_o, norm2_g, w_ffn_in, w_ffn_out, final_norm_g):
    b, s, d = x.shape
    depth = w_ada.shape[0]
    assert s % TM_QKV == 0 and s % (B_GROUPS[-1][1] * WIN) == 0 and TK % TM_QKV == 0 and s % TK == 0
    tabs = _rope_tables(s)
    heads_per_vreg = LANES // HEAD_DIM
    lane_head = np.arange(MXU_DIM) // HEAD_DIM
    bd = jnp.asarray(lane_head[:, None] == lane_head[None, :], BF16)
    b_q_lo = A_Q_W + 2 * A_KV_W
    col_scale = jnp.ones((w_qkv.shape[-1],), F32).at[b_q_lo:b_q_lo + B_W].set(SCALE * LOG2E)
    c_pad = jnp.pad(c, ((0, 16 - b), (0, 0)))
    for l in range(depth):
        mod3 = _modulation(c_pad, w_ada, b_ada[l], l)[:b].reshape(b, 6, d)
        qg = jnp.tile(q_norm_a[l] * (SCALE * LOG2E), heads_per_vreg).reshape(1, LANES)
        kg = jnp.tile(k_norm_a[l], heads_per_vreg).reshape(1, LANES)
        g1 = norm1_g[l].reshape(1, d)
        wq = (w_qkv[l] * col_scale[None, :]).astype(BF16)
        qat, ka, vat, *groups = _qkv_project(x, mod3, g1, wq, qg, kg, bd, tabs)
        attn = _global_attention(qat, ka, vat)
        branch = [_banded_attention(qkv_g, g) for g, qkv_g in enumerate(groups)]
        x = _mix(x, mod3, g1, attn, [o for o, _ in branch], [e for _, e in branch],
                 (0.5 * w_gate[l]).astype(BF16), (0.5 * b_gate[l]).reshape(1, -1), w_proj_a[l].astype(BF16),
                 w_proj_b[l].astype(BF16), (0.5 * w_o[l]).astype(BF16))
        d_ff = w_ffn_out.shape[1]
        half_gate = jnp.where(jnp.arange(2 * d_ff) < d_ff, 0.5, 1.0).astype(F32)
        x = _ffn(x, mod3, norm2_g[l].reshape(1, d), (w_ffn_in[l] * half_gate[None, :]).astype(BF16),
                 w_ffn_out[l].astype(BF16), final_norm_g.reshape(1, d), final=(l == depth - 1))
    return x
```

```python
import functools

import jax
import jax.numpy as jnp
import numpy as np
from jax import lax
from jax.experimental import pallas as pl
from jax.experimental.pallas import tpu as pltpu

HEAD_DIM = 64
A_Q_HEADS = 8
A_KV_HEADS = 2
A_GROUP = A_Q_HEADS // A_KV_HEADS
B_GROUPS = ((128, 1), (512, 4), (2048, 16))
B_HEADS_PER_GROUP = 4
B_HEADS = B_HEADS_PER_GROUP * len(B_GROUPS)
A_Q_W = A_Q_HEADS * HEAD_DIM
A_KV_W = A_KV_HEADS * HEAD_DIM
B_W = B_HEADS * HEAD_DIM
B_OUT_W = B_HEADS_PER_GROUP * HEAD_DIM
GRID_W = 64
AXIAL_THETA = 10000.0
PARTIAL_THETA = 500000.0
PARTIAL_ROT_DIM = HEAD_DIM // 4
EPS = 1e-6
NEG_INF = -1e30
SCALE = HEAD_DIM ** -0.5
LOG2E = 1.4426950408889634
LN2 = 0.6931471805599453

LANES = 128
MXU_DIM = 256
VMEM_LIMIT = 56 * 1024 * 1024

F32 = jnp.float32
BF16 = jnp.bfloat16

TM_QKV = 1024
QKV_ROW_SPLITS = 2
TQ = 128
TK = 1024
TM_MIX = 1024
MIX_ROW_SPLITS = 1
TM_FFN = 1024
FFN_CHUNKS = 3
FFN_ROW_SPLITS = 2
BLK = 128
WIN = 256
SWA_MAX_STORE_STRIDE = 4
VT_ROWS = HEAD_DIM + 16
GQA_UNROLL = 2
GQA_TILES = 32


def _params(*sem):
    return pltpu.CompilerParams(dimension_semantics=sem, vmem_limit_bytes=VMEM_LIMIT)


def _mod_kernel(c_ref, w_ref, b_ref, o_ref):
    c = c_ref[...]
    cond = c * jax.nn.sigmoid(c)
    c_hi = cond.astype(BF16)
    c_lo = (cond - c_hi.astype(F32)).astype(BF16)
    w = w_ref[...]
    w_hi = w.astype(BF16)
    w_lo = (w - w_hi.astype(F32)).astype(BF16)
    acc = jnp.dot(c_hi, w_hi, preferred_element_type=F32)
    acc += jnp.dot(c_hi, w_lo, preferred_element_type=F32)
    acc += jnp.dot(c_lo, w_hi, preferred_element_type=F32)
    o_ref[...] = acc + b_ref[...]


def _modulation(c_pad, w_ada, b_ada, layer):
    rows, d = c_pad.shape
    n = w_ada.shape[2]
    tn = 2048
    return pl.pallas_call(
        _mod_kernel,
        grid=(n // tn,),
        in_specs=[pl.BlockSpec((rows, d), lambda j: (0, 0)),
                  pl.BlockSpec((None, d, tn), lambda j: (layer, 0, j)),
                  pl.BlockSpec((1, tn), lambda j: (0, j))],
        out_specs=pl.BlockSpec((rows, tn), lambda j: (0, j)),
        out_shape=jax.ShapeDtypeStruct((rows, n), F32),
        compiler_params=_params("parallel"),
        name="mod",
    )(c_pad, w_ada, b_ada.reshape(1, n))


def _modulated_norm(x, gain, shift, scale):
    ms = jnp.mean(x * x, axis=-1, keepdims=True)
    return ((x * lax.rsqrt(ms + EPS)) * (gain * (1.0 + scale)) + shift).astype(BF16)


def _interleave(programs):
    done = object()
    live = []
    for prog in programs:
        live.append(prog)
        live = [p for p in live if next(p, done) is not done]
    while live:
        live = [p for p in live if next(p, done) is not done]


def _rope(y, cos, sin_lo, sin_hi, shift):
    return (y * cos + pltpu.roll(y, LANES - shift, 1) * sin_lo
            + pltpu.roll(y, shift, 1) * sin_hi)


def _qkv_kernel(x_ref, mod_ref, g1_ref, w_ref, qg_ref, kg_ref, bd_ref,
                ac_ref, asl_ref, ash_ref, pc_ref, psl_ref, psh_ref,
                qat_ref, ka_ref, vat_ref, grp0_ref, grp1_ref, grp2_ref, slab_ref):
    tm = x_ref.shape[0]
    th = tm // QKV_ROW_SPLITS
    def slab(h):
        rows = slice(h * th, (h + 1) * th)
        u = _modulated_norm(x_ref[rows, :], g1_ref[...], mod_ref[0:1, :], mod_ref[1:2, :])
        yield

        def proj(lo, width):
            return jnp.dot(u, w_ref[:, lo:lo + width], preferred_element_type=F32)

        def head_norm_rope(wide, gain):
            out = []
            for lo in range(0, wide.shape[1], MXU_DIM):
                tile = wide[:, lo:lo + MXU_DIM]
                w = tile.shape[1]
                ss = jnp.dot((tile * tile).astype(BF16), bd_ref[:w, :w], preferred_element_type=F32)
                for c in range(w // LANES):
                    cols = slice(c * LANES, (c + 1) * LANES)
                    y = (tile[:, cols] * lax.rsqrt(ss[:, cols] * (1.0 / HEAD_DIM) + EPS)) * gain
                    out.append(_rope(y, ac_ref[rows, :], asl_ref[rows, :], ash_ref[rows, :], HEAD_DIM // 4))
            return out[0] if len(out) == 1 else jnp.concatenate(out, axis=1)

        def branch_b(g, between):
            out_ref, dil = (grp0_ref, grp1_ref, grp2_ref)[g], B_GROUPS[g][1]
            base = A_Q_W + 2 * A_KV_W
            halves = B_OUT_W // LANES
            for part in range(3):
                wide = proj(base + part * B_W + g * B_OUT_W, B_OUT_W)
                for c in range(halves):
                    chunk = wide[:, c * LANES:(c + 1) * LANES]
                    if part < 2:
                        chunk = _rope(chunk, pc_ref[rows, :], psl_ref[rows, :], psh_ref[rows, :],
                                      PARTIAL_ROT_DIM // 2)
                    slab = part * halves + c
                    if dil == 1:
                        out_ref[0, rows, slab * LANES:(slab + 1) * LANES] = chunk.astype(BF16)
                    else:
                        slab_ref[g - 1, slab, rows, :] = chunk
                between()
                yield
            if dil > 1:
                n = th // dil
                for r in range(dil):
                    for slab in range(3 * halves):
                        out_ref[r, h * n:(h + 1) * n, slab * LANES:(slab + 1) * LANES] = (
                            slab_ref[g - 1, slab, pl.ds(h * th + r, n, stride=dil), :].astype(BF16))
                yield

        raw_q = proj(0, A_Q_W)
        yield
        raw_kv = proj(A_Q_W, 2 * A_KV_W)
        yield
        q_tiles = []

        def store_q():
            qat = jnp.concatenate(q_tiles, axis=1).T.astype(BF16)
            for i in range(th // TQ):
                qat_ref[h * (th // TQ) + i] = qat[:, i * TQ:(i + 1) * TQ]

        def store_k():
            ka_ref[rows, :] = head_norm_rope(raw_kv[:, :A_KV_W], kg_ref[...]).astype(BF16)

        def store_v():
            vat = raw_kv[:, A_KV_W:].T.astype(BF16)
            pad_row = lax.broadcasted_iota(jnp.int32, (VT_ROWS - HEAD_DIM, th), 0)
            ones_row = jnp.where(pad_row == 0, 1.0, 0.0).astype(BF16)
            for hk in range(A_KV_HEADS):
                vat_ref[hk, 0:HEAD_DIM, rows] = vat[hk * HEAD_DIM:(hk + 1) * HEAD_DIM, :]
                vat_ref[hk, HEAD_DIM:VT_ROWS, rows] = ones_row

        steps = [functools.partial(lambda lo: q_tiles.append(head_norm_rope(raw_q[:, lo:lo + MXU_DIM], qg_ref[...])), lo)
                 for lo in range(0, A_Q_W, MXU_DIM)] + [store_q, store_k, store_v]

        def between():
            if steps:
                steps.pop(0)()

        for g in (2, 1, 0):
            yield from branch_b(g, between)
        while steps:
            between()
            yield

    _interleave([slab(h) for h in range(QKV_ROW_SPLITS)])


def _qkv_project(x, mod3, g1, w_qkv, qg, kg, bd, tabs):
    b, s, d = x.shape
    tm = TM_QKV
    ncols = w_qkv.shape[1]
    row_tab = pl.BlockSpec((tm, LANES), lambda i, bb: (i, 0))
    const2 = lambda shape: pl.BlockSpec(shape, lambda i, bb: (0, 0))
    tok = lambda w: pl.BlockSpec((None, tm, w), lambda i, bb: (bb, i, 0))
    out_shapes = (
        jax.ShapeDtypeStruct((b, s // TQ, A_Q_W, TQ), BF16),
        jax.ShapeDtypeStruct((b, s, A_KV_W), BF16),
        jax.ShapeDtypeStruct((b, s // TK, A_KV_HEADS, VT_ROWS, TK), BF16),
    ) + tuple(jax.ShapeDtypeStruct((b, dil, s // dil, 3 * B_OUT_W), BF16) for _, dil in B_GROUPS)
    out_specs = (
        pl.BlockSpec((None, tm // TQ, A_Q_W, TQ), lambda i, bb: (bb, i, 0, 0)),
        tok(A_KV_W),
        pl.BlockSpec((None, None, A_KV_HEADS, VT_ROWS, tm),
                     lambda i, bb: (bb, i // (TK // tm), 0, 0, i % (TK // tm))),
    ) + tuple(pl.BlockSpec((None, dil, tm // dil, 3 * B_OUT_W), lambda i, bb: (bb, 0, i, 0))
              for _, dil in B_GROUPS)
    return pl.pallas_call(
        _qkv_kernel,
        grid=(s // tm, b),
        in_specs=[tok(d),
                  pl.BlockSpec((None, 6, d), lambda i, bb: (bb, 0, 0)),
                  const2((1, d)),
                  const2((d, ncols)),
                  const2((1, LANES)), const2((1, LANES)), const2((MXU_DIM, MXU_DIM)),
                  row_tab, row_tab, row_tab, row_tab, row_tab, row_tab],
        out_specs=out_specs,
        out_shape=out_shapes,
        scratch_shapes=[pltpu.VMEM((len(B_GROUPS) - 1, 3 * B_OUT_W // LANES, tm, LANES), F32)],
        compiler_params=_params("parallel", "parallel"),
        name="qkv",
    )(x, mod3, g1, w_qkv, qg, kg, bd, *tabs)


def _gqa_kernel(qt_ref, k_ref, vt_ref, o_ref, qx_ref, s0_ref, s1_ref):
    nk = vt_ref.shape[0]
    ncol = A_GROUP * TQ
    n_units = GQA_TILES * A_KV_HEADS

    qx_ref[...] = jnp.zeros(qx_ref.shape, BF16)
    for t in range(GQA_TILES):
        for hk in range(A_KV_HEADS):
            for g in range(A_GROUP):
                h = hk * A_GROUP + g
                qx_ref[t * A_KV_HEADS + hk, hk * HEAD_DIM:(hk + 1) * HEAD_DIM, g * TQ:(g + 1) * TQ] = (
                    qt_ref[t, h * HEAD_DIM:(h + 1) * HEAD_DIM, :])

    s_refs = (s0_ref, s1_ref)

    def phase(score_unit, value_unit, parity, m_prev):
        qx = None if score_unit is None else qx_ref[score_unit]

        def body(j, carry):
            mrun, acc = carry
            rows = pl.ds(pl.multiple_of(j * TK, TK), TK)
            if score_unit is not None:
                st = jnp.dot(k_ref[rows, :], qx, preferred_element_type=F32)
                s_refs[parity][rows, :] = st
                mrun = jnp.maximum(mrun, jnp.max(st.reshape(TK // 8, 8, ncol), axis=0))
            if value_unit is not None:
                p = jnp.exp2((s_refs[1 - parity][rows, :] - m_prev).astype(BF16))
                acc = acc + jnp.dot(vt_ref[j, 1 - parity], p, preferred_element_type=F32)
            return mrun, acc

        init = (jnp.full((8, ncol), NEG_INF, F32), jnp.zeros((VT_ROWS, ncol), F32))
        mrun, acc = lax.fori_loop(0, nk, body, init, unroll=GQA_UNROLL)
        if value_unit is not None:
            out_t = acc[0:HEAD_DIM, :] / acc[HEAD_DIM:HEAD_DIM + 1, :]
            stacked = jnp.concatenate([out_t[:, g * TQ:(g + 1) * TQ] for g in range(A_GROUP)], axis=0)
            rows = pl.ds(pl.multiple_of((value_unit // A_KV_HEADS) * TQ, TQ), TQ)
            group_w = A_GROUP * HEAD_DIM
            o_ref[rows, (1 - parity) * group_w:(2 - parity) * group_w] = stacked.T.astype(BF16)
        return jnp.max(mrun, axis=0, keepdims=True)

    assert A_KV_HEADS == 2 and n_units % 2 == 0
    m = phase(0, None, 0, None)

    def unit_pair(i, m_prev):
        m_odd = phase(2 * i + 1, 2 * i, 1, m_prev)
        return phase(2 * i + 2, 2 * i + 1, 0, m_odd)

    m = lax.fori_loop(0, n_units // 2 - 1, unit_pair, m)
    m = phase(n_units - 1, n_units - 2, 1, m)
    phase(None, n_units - 1, 0, m)


def _global_attention(qat, ka, vat):
    b, nq = qat.shape[0], qat.shape[1]
    s = ka.shape[1]
    nk = vat.shape[1]
    ncol = A_GROUP * TQ
    return pl.pallas_call(
        _gqa_kernel,
        grid=(b, nq // GQA_TILES),
        in_specs=[pl.BlockSpec((None, GQA_TILES, A_Q_W, TQ), lambda bb, i: (bb, i, 0, 0)),
                  pl.BlockSpec((None, s, A_KV_W), lambda bb, i: (bb, 0, 0)),
                  pl.BlockSpec((None, nk, A_KV_HEADS, VT_ROWS, TK), lambda bb, i: (bb, 0, 0, 0, 0))],
        out_specs=pl.BlockSpec((None, GQA_TILES * TQ, A_Q_W), lambda bb, i: (bb, i, 0)),
        out_shape=jax.ShapeDtypeStruct((b, s, A_Q_W), BF16),
        scratch_shapes=[pltpu.VMEM((GQA_TILES * A_KV_HEADS, A_KV_W, ncol), BF16),
                        pltpu.VMEM((s, ncol), F32), pltpu.VMEM((s, ncol), F32)],
        compiler_params=_params("parallel", "parallel"),
        name="gqa",
    )(qat, ka, vat)


def _swa_kernel(qkv_ref, o_ref, lse_ref, sc_ref, p_ref, top_ref, bias_ref, *, radius):
    dil, seq = qkv_ref.shape[0], qkv_ref.shape[1]
    nblk = seq // BLK
    halves = B_OUT_W // LANES
    lane = lax.broadcasted_iota(jnp.int32, (BLK, LANES), 1)
    first_head = lane < HEAD_DIM
    head_mask = [jnp.where(first_head, 1.0, 0.0).astype(BF16), jnp.where(first_head, 0.0, 1.0).astype(BF16)]
    ones_win = jnp.ones((WIN, LANES), BF16)

    rel = (lax.broadcasted_iota(jnp.int32, (BLK, WIN), 0)
           - lax.broadcasted_iota(jnp.int32, (BLK, WIN), 1))
    for n in range(bias_ref.shape[0]):
        bias_ref[n] = jnp.where(jnp.abs(rel + n * radius) <= radius, 0.0, NEG_INF)

    def geometry(i):
        r = i // nblk
        i0 = pl.multiple_of((i % nblk) * BLK, BLK)
        ws = pl.multiple_of(jnp.clip(i0 - radius, 0, seq - WIN), radius)
        return r, i0, ws

    def scores(i, slot):
        r, i0, ws = geometry(i)
        bias = bias_ref[(i0 - ws) // radius]
        for c in range(halves):
            q = qkv_ref[r, pl.ds(i0, BLK), c * LANES:(c + 1) * LANES]
            k = qkv_ref[r, pl.ds(ws, WIN), B_OUT_W + c * LANES:B_OUT_W + (c + 1) * LANES]
            for hh in range(2):
                sc_ref[slot, 2 * c + hh] = lax.dot_general(
                    q * head_mask[hh], k, (((1,), (1,)), ((), ())), preferred_element_type=F32) + bias

    def probabilities(slot):
        for c in range(halves):
            tops = []
            for hh in range(2):
                sc = sc_ref[slot, 2 * c + hh]
                m = jnp.max(sc, axis=-1, keepdims=True)
                p_ref[slot, 2 * c + hh] = jnp.exp2((sc - m).astype(BF16))
                tops.append(m)
            top_ref[slot, c] = jnp.where(first_head, tops[0], tops[1])

    def outputs(i, slot):
        r, i0, ws = geometry(i)
        for c in range(halves):
            v = qkv_ref[r, pl.ds(ws, WIN), 2 * B_OUT_W + c * LANES:2 * B_OUT_W + (c + 1) * LANES]
            v_ones = jnp.concatenate([v, ones_win], axis=1)
            outs = [jnp.dot(p_ref[slot, 2 * c + hh], v_ones, preferred_element_type=F32) for hh in range(2)]
            num = jnp.where(first_head, outs[0][:, :LANES], outs[1][:, :LANES])
            den = jnp.where(first_head, outs[0][:, LANES:], outs[1][:, LANES:])
            classes = o_ref.shape[1]
            step = dil // classes
            rows = pl.ds(i0 * step + r // classes, BLK, stride=step) if step > 1 else pl.ds(i0, BLK)
            o_ref[c, r % classes, rows, :] = num / den
            lse_ref[c, r % classes, rows, :] = (top_ref[slot, c] + jnp.log2(den)) * LN2

    total = dil * nblk
    scores(0, 0)
    probabilities(0)
    scores(1, 1)

    def body(i, _):
        outputs(i - 2, i % 2)
        probabilities((i - 1) % 2)
        scores(i, i % 2)
        return 0

    lax.fori_loop(2, total, body, 0, unroll=2)
    outputs(total - 2, total % 2)
    probabilities((total - 1) % 2)
    outputs(total - 1, (total - 1) % 2)


def _banded_attention(qkv, group):
    window, dil = B_GROUPS[group]
    b, _, seq, width = qkv.shape
    radius = window // 2 // dil
    halves = B_OUT_W // LANES
    classes = max(1, dil // SWA_MAX_STORE_STRIDE)
    out_spec = pl.BlockSpec((None, halves, classes, dil * seq // classes, LANES), lambda bb: (bb, 0, 0, 0, 0))
    out_shape = jax.ShapeDtypeStruct((b, halves, classes, dil * seq // classes, LANES), F32)
    return pl.pallas_call(
        functools.partial(_swa_kernel, radius=radius),
        grid=(b,),
        in_specs=[pl.BlockSpec((None, dil, seq, width), lambda bb: (bb, 0, 0, 0))],
        out_specs=(out_spec, out_spec),
        out_shape=(out_shape, out_shape),
        scratch_shapes=[pltpu.VMEM((2, B_HEADS_PER_GROUP, BLK, WIN), F32),
                        pltpu.VMEM((2, B_HEADS_PER_GROUP, BLK, WIN), BF16),
                        pltpu.VMEM((2, B_OUT_W // LANES, BLK, LANES), F32),
                        pltpu.VMEM(((WIN - BLK) // radius + 1, BLK, WIN), F32)],
        compiler_params=_params("parallel"),
        name=f"swa{group}",
    )(qkv)


def _mix_kernel(x_ref, mod_ref, g1_ref, attn_ref, o0_ref, o1_ref, o2_ref, l0_ref, l1_ref, l2_ref,
                wg_ref, bg_ref, wpa_ref, wpb_ref, wo_ref, out_ref, nat_ref):
    d = x_ref.shape[-1]
    halves = B_OUT_W // LANES
    th = x_ref.shape[0] // MIX_ROW_SPLITS

    def slab(h):
        rows = slice(h * th, (h + 1) * th)
        x = x_ref[rows, :]
        ya = jnp.dot(attn_ref[rows, :], wpa_ref[...], preferred_element_type=F32)
        yield
        u = _modulated_norm(x, g1_ref[...], mod_ref[0:1, :], mod_ref[1:2, :])

        def token_order(src, c, slot):
            classes = src.shape[1]
            if classes == 1:
                return src[c, 0, rows, :]
            n = th // classes
            for k in range(classes):
                nat_ref[slot, pl.ds(h * th + k, n, stride=classes), :] = src[c, k, h * n:(h + 1) * n, :]
            return nat_ref[slot, rows, :]

        def combine(c):
            groups = len(B_GROUPS)
            lses = [token_order(l_ref, c, (2 * c) * groups + g) for g, l_ref in enumerate((l0_ref, l1_ref, l2_ref))]
            outs = [token_order(o_ref, c, (2 * c + 1) * groups + g)
                    for g, o_ref in enumerate((o0_ref, o1_ref, o2_ref))]
            top = jnp.maximum(jnp.maximum(lses[0], lses[1]), lses[2])
            es = [jnp.exp(l - top) for l in lses]
            den = es[0] + es[1] + es[2]
            return (sum(e * o for e, o in zip(es, outs)) / den).astype(BF16)

        def twice_gates(j, parts=4):
            w = 2 * d // parts
            cols = slice(j * w, (j + 1) * w)
            return jnp.tanh(jnp.dot(u, wg_ref[:, cols], preferred_element_type=F32) + bg_ref[:, cols]) + 1.0

        gate, comb = [], []
        for j in range(4):
            gate.append(twice_gates(j))
            yield
            if j < halves:
                comb.append(combine(j))
                yield
            if j == 1:
                gated_a = jnp.concatenate(gate[:2], axis=1) * ya
        yb = jnp.dot(jnp.concatenate(comb, axis=1), wpb_ref[...], preferred_element_type=F32)
        yield
        merged = (gated_a + jnp.concatenate(gate[2:], axis=1) * yb).astype(BF16)
        yield
        mix = jnp.dot(merged, wo_ref[...], preferred_element_type=F32)
        out_ref[rows, :] = x + mod_ref[2:3, :] * mix

    _interleave([slab(h) for h in range(MIX_ROW_SPLITS)])


def _mix(x, mod3, g1, attn, os_, lses, wg, bg, wpa, wpb, wo):
    b, s, d = x.shape
    tm = TM_MIX
    tok = lambda w: pl.BlockSpec((None, tm, w), lambda bb, i: (bb, i, 0))
    const2 = lambda shape: pl.BlockSpec(shape, lambda bb, i: (0, 0))
    halves = B_OUT_W // LANES
    sub = [pl.BlockSpec((None, halves, o.shape[2], tm // o.shape[2], LANES), lambda bb, i: (bb, 0, 0, i, 0))
           for o in os_]
    return pl.pallas_call(
        _mix_kernel,
        grid=(b, s // tm),
        in_specs=[tok(d),
                  pl.BlockSpec((None, 6, d), lambda bb, i: (bb, 0, 0)),
                  const2((1, d)),
                  tok(A_Q_W),
                  *sub, *sub,
                  const2(wg.shape), const2(bg.shape), const2(wpa.shape), const2(wpb.shape),
                  const2(wo.shape)],
        out_specs=tok(d),
        out_shape=jax.ShapeDtypeStruct((b, s, d), F32),
        scratch_shapes=[pltpu.VMEM((2 * halves * len(B_GROUPS), tm, LANES), F32)],
        compiler_params=_params("parallel", "parallel"),
        name="mix",
    )(x, mod3, g1, attn, *os_, *lses, wg, bg, wpa, wpb, wo)


def _ffn_kernel(x_ref, mod_ref, g2_ref, win_ref, wout_ref, fg_ref, out_ref, *, final, bounds):
    d_ff = wout_ref.shape[0]
    th = x_ref.shape[0] // FFN_ROW_SPLITS

    def slab(h):
        rows = slice(h * th, (h + 1) * th)
        x = x_ref[rows, :]
        u = _modulated_norm(x, g2_ref[...], mod_ref[3:4, :], mod_ref[4:5, :])
        yield
        acc = jnp.zeros(x.shape, F32)
        for lo, hi in zip(bounds[:-1], bounds[1:]):
            hg = jnp.dot(u, win_ref[:, lo:hi], preferred_element_type=F32)
            yield
            hu = jnp.dot(u, win_ref[:, d_ff + lo:d_ff + hi], preferred_element_type=F32)
            yield
            act = ((hg * (jnp.tanh(hg) + 1.0)) * hu).astype(BF16)
            acc += jnp.dot(act, wout_ref[lo:hi, :], preferred_element_type=F32)
            yield
        y = x + mod_ref[5:6, :] * acc
        if final:
            ms = jnp.mean(y * y, axis=-1, keepdims=True)
            y = (y * lax.rsqrt(ms + EPS)) * fg_ref[...]
        out_ref[rows, :] = y

    _interleave([slab(h) for h in range(FFN_ROW_SPLITS)])


def _ffn(x, mod3, g2, win, wout, fg, final):
    b, s, d = x.shape
    tm = TM_FFN
    d_ff = wout.shape[0]
    tiles = pl.cdiv(d_ff, MXU_DIM)
    per_chunk = pl.cdiv(tiles, FFN_CHUNKS) * MXU_DIM
    bounds = tuple(min(d_ff, i * per_chunk) for i in range(FFN_CHUNKS + 1))
    tok = pl.BlockSpec((None, tm, d), lambda bb, i: (bb, i, 0))
    resident = lambda shape: pl.BlockSpec(shape, lambda bb, i: (0, 0), pipeline_mode=pl.Buffered(1))
    return pl.pallas_call(
        functools.partial(_ffn_kernel, final=final, bounds=bounds),
        grid=(b, s // tm),
        in_specs=[tok,
                  pl.BlockSpec((None, 6, d), lambda bb, i: (bb, 0, 0)),
                  pl.BlockSpec((1, d), lambda bb, i: (0, 0)),
                  resident(win.shape), resident(wout.shape),
                  pl.BlockSpec((1, d), lambda bb, i: (0, 0))],
        out_specs=tok,
        out_shape=jax.ShapeDtypeStruct((b, s, d), F32),
        compiler_params=_params("parallel", "parallel"),
        name="ffn",
    )(x, mod3, g2, win, wout, fg)


def _inv_freq(dim, theta):
    return theta ** (-jnp.arange(0, dim, 2, dtype=F32) / dim)


def _rope_tables(s):
    in_head = np.arange(LANES) % HEAD_DIM
    half = HEAD_DIM // 2
    quarter = half // 2
    inv_a = (AXIAL_THETA ** (-jnp.asarray(2 * (in_head % quarter), F32) / half))[None, :]
    ang_row = jnp.arange(s // GRID_W, dtype=jnp.int32).astype(F32)[:, None] * inv_a
    ang_col = jnp.arange(GRID_W, dtype=jnp.int32).astype(F32)[:, None] * inv_a
    by_row = (in_head < half)[None, None, :]
    first_a = ((in_head % half) < quarter)[None, None, :]

    small = lax.optimization_barrier(
        (jnp.cos(ang_row), jnp.cos(ang_col), jnp.sin(ang_row), jnp.sin(ang_col)))

    def grid(of_row, of_col):
        return jnp.where(by_row, of_row[:, None, :], of_col[None, :, :])

    a_cos = grid(small[0], small[1]).reshape(s, LANES)
    a_sin = grid(small[2], small[3])
    a_lo = jnp.where(first_a, -a_sin, 0.0).reshape(s, LANES)
    a_hi = jnp.where(first_a, 0.0, a_sin).reshape(s, LANES)
    t = jnp.arange(s, dtype=jnp.int32)
    ap = t.astype(F32)[:, None] * _inv_freq(PARTIAL_ROT_DIM, PARTIAL_THETA)[None, :]
    rest = HEAD_DIM - PARTIAL_ROT_DIM
    zp = jnp.zeros_like(ap)
    p_cos = jnp.concatenate([jnp.cos(ap), jnp.cos(ap), jnp.ones((s, rest), F32)], axis=1)
    p_lo = jnp.concatenate([-jnp.sin(ap), zp, jnp.zeros((s, rest), F32)], axis=1)
    p_hi = jnp.concatenate([zp, jnp.sin(ap), jnp.zeros((s, rest), F32)], axis=1)
    two = lambda a: jnp.tile(a, (1, LANES // HEAD_DIM))
    return a_cos, a_lo, a_hi, two(p_cos), two(p_lo), two(p_hi)


def kernel(x, c, w_ada, b_ada, norm1_g, w_qkv, q_norm_a, k_norm_a, w_proj_a, w_proj_b, w_gate,
           b_gate, w_o, norm2_g, w_ffn_in, w_ffn_out, final_norm_g):
    b, s, d = x.shape
    depth = w_ada.shape[0]
    assert s % TM_QKV == 0 and s % (B_GROUPS[-1][1] * WIN) == 0 and TK % TM_QKV == 0 and s % TK == 0
    tabs = _rope_tables(s)
    heads_per_vreg = LANES // HEAD_DIM
    lane_head = np.arange(MXU_DIM) // HEAD_DIM
    bd = jnp.asarray(lane_head[:, None] == lane_head[None, :], BF16)
    b_q_lo = A_Q_W + 2 * A_KV_W
    col_scale = jnp.ones((w_qkv.shape[-1],), F32).at[b_q_lo:b_q_lo + B_W].set(SCALE * LOG2E)
    c_pad = jnp.pad(c, ((0, 16 - b), (0, 0)))
    for l in range(depth):
        mod3 = _modulation(c_pad, w_ada, b_ada[l], l)[:b].reshape(b, 6, d)
        qg = jnp.tile(q_norm_a[l] * (SCALE * LOG2E), heads_per_vreg).reshape(1, LANES)
        kg = jnp.tile(k_norm_a[l], heads_per_vreg).reshape(1, LANES)
        g1 = norm1_g[l].reshape(1, d)
        wq = (w_qkv[l] * col_scale[None, :]).astype(BF16)
        qat, ka, vat, *groups = _qkv_project(x, mod3, g1, wq, qg, kg, bd, tabs)
        attn = _global_attention(qat, ka, vat)
        branch = [_banded_attention(qkv_g, g) for g, qkv_g in enumerate(groups)]
        x = _mix(x, mod3, g1, attn, [o for o, _ in branch], [e for _, e in branch],
                 (0.5 * w_gate[l]).astype(BF16), (0.5 * b_gate[l]).reshape(1, -1), w_proj_a[l].astype(BF16),
                 w_proj_b[l].astype(BF16), (0.5 * w_o[l]).astype(BF16))
        d_ff = w_ffn_out.shape[1]
        half_gate = jnp.where(jnp.arange(2 * d_ff) < d_ff, 0.5, 1.0).astype(F32)
        x = _ffn(x, mod3, norm2_g[l].reshape(1, d), (w_ffn_in[l] * half_gate[None, :]).astype(BF16),
                 w_ffn_out[l].astype(BF16), final_norm_g.reshape(1, d), final=(l == depth - 1))
    return x
```

```python
import functools

import jax
import jax.numpy as jnp
import numpy as np
from jax import lax
from jax.experimental import pallas as pl
from jax.experimental.pallas import tpu as pltpu

HEAD_DIM = 64
A_Q_HEADS = 8
A_KV_HEADS = 2
A_GROUP = A_Q_HEADS // A_KV_HEADS
B_GROUPS = ((128, 1), (512, 4), (2048, 16))
B_HEADS_PER_GROUP = 4
B_HEADS = B_HEADS_PER_GROUP * len(B_GROUPS)
A_Q_W = A_Q_HEADS * HEAD_DIM
A_KV_W = A_KV_HEADS * HEAD_DIM
B_W = B_HEADS * HEAD_DIM
B_OUT_W = B_HEADS_PER_GROUP * HEAD_DIM
GRID_W = 64
AXIAL_THETA = 10000.0
PARTIAL_THETA = 500000.0
PARTIAL_ROT_DIM = HEAD_DIM // 4
EPS = 1e-6
NEG_INF = -1e30
SCALE = HEAD_DIM ** -0.5
LOG2E = 1.4426950408889634
LN2 = 0.6931471805599453

LANES = 128
MXU_DIM = 256
VMEM_LIMIT = 56 * 1024 * 1024

F32 = jnp.float32
BF16 = jnp.bfloat16

TM_QKV = 1024
QKV_ROW_SPLITS = 2
TQ = 128
TK = 1024
TM_MIX = 1024
MIX_ROW_SPLITS = 1
TM_FFN = 1024
FFN_CHUNKS = 3
FFN_ROW_SPLITS = 4
BLK = 128
WIN = 256
SWA_MAX_STORE_STRIDE = 4
VT_ROWS = HEAD_DIM + 16
GQA_UNROLL = 2
GQA_TILES = 32


def _params(*sem):
    return pltpu.CompilerParams(dimension_semantics=sem, vmem_limit_bytes=VMEM_LIMIT)


def _mod_kernel(c_ref, w_ref, b_ref, o_ref):
    c = c_ref[...]
    cond = c * jax.nn.sigmoid(c)
    c_hi = cond.astype(BF16)
    c_lo = (cond - c_hi.astype(F32)).astype(BF16)
    w = w_ref[...]
    w_hi = w.astype(BF16)
    w_lo = (w - w_hi.astype(F32)).astype(BF16)
    acc = jnp.dot(c_hi, w_hi, preferred_element_type=F32)
    acc += jnp.dot(c_hi, w_lo, preferred_element_type=F32)
    acc += jnp.dot(c_lo, w_hi, preferred_element_type=F32)
    o_ref[...] = acc + b_ref[...]


def _modulation(c_pad, w_ada, b_ada, layer):
    rows, d = c_pad.shape
    n = w_ada.shape[2]
    tn = 1024
    return pl.pallas_call(
        _mod_kernel,
        grid=(n // tn,),
        in_specs=[pl.BlockSpec((rows, d), lambda j: (0, 0)),
                  pl.BlockSpec((None, d, tn), lambda j: (layer, 0, j)),
                  pl.BlockSpec((1, tn), lambda j: (0, j))],
        out_specs=pl.BlockSpec((rows, tn), lambda j: (0, j)),
        out_shape=jax.ShapeDtypeStruct((rows, n), F32),
        compiler_params=_params("parallel"),
        name="mod",
    )(c_pad, w_ada, b_ada.reshape(1, n))


def _modulated_norm(x, gain, shift, scale):
    ms = jnp.mean(x * x, axis=-1, keepdims=True)
    return ((x * lax.rsqrt(ms + EPS)) * (gain * (1.0 + scale)) + shift).astype(BF16)


def _interleave(programs):
    done = object()
    live = []
    for prog in programs:
        live.append(prog)
        live = [p for p in live if next(p, done) is not done]
    while live:
        live = [p for p in live if next(p, done) is not done]


def _rope(y, cos, sin_lo, sin_hi, shift):
    return (y * cos + pltpu.roll(y, LANES - shift, 1) * sin_lo
            + pltpu.roll(y, shift, 1) * sin_hi)


def _qkv_kernel(x_ref, mod_ref, g1_ref, w_ref, qg_ref, kg_ref, bd_ref,
                ac_ref, asl_ref, ash_ref, pc_ref, psl_ref, psh_ref,
                qat_ref, ka_ref, vat_ref, grp0_ref, grp1_ref, grp2_ref, slab_ref):
    tm = x_ref.shape[0]
    th = tm // QKV_ROW_SPLITS
    def slab(h):
        rows = slice(h * th, (h + 1) * th)
        u = _modulated_norm(x_ref[rows, :], g1_ref[...], mod_ref[0:1, :], mod_ref[1:2, :])
        yield

        def proj(lo, width):
            return jnp.dot(u, w_ref[:, lo:lo + width], preferred_element_type=F32)

        def head_norm_rope(wide, gain):
            out = []
            for lo in range(0, wide.shape[1], MXU_DIM):
                tile = wide[:, lo:lo + MXU_DIM]
                w = tile.shape[1]
                ss = jnp.dot((tile * tile).astype(BF16), bd_ref[:w, :w], preferred_element_type=F32)
                for c in range(w // LANES):
                    cols = slice(c * LANES, (c + 1) * LANES)
                    y = (tile[:, cols] * lax.rsqrt(ss[:, cols] * (1.0 / HEAD_DIM) + EPS)) * gain
                    out.append(_rope(y, ac_ref[rows, :], asl_ref[rows, :], ash_ref[rows, :], HEAD_DIM // 4))
            return out[0] if len(out) == 1 else jnp.concatenate(out, axis=1)

        def branch_b(g, between):
            out_ref, dil = (grp0_ref, grp1_ref, grp2_ref)[g], B_GROUPS[g][1]
            base = A_Q_W + 2 * A_KV_W
            halves = B_OUT_W // LANES
            for part in range(3):
                wide = proj(base + part * B_W + g * B_OUT_W, B_OUT_W)
                for c in range(halves):
                    chunk = wide[:, c * LANES:(c + 1) * LANES]
                    if part < 2:
                        chunk = _rope(chunk, pc_ref[rows, :], psl_ref[rows, :], psh_ref[rows, :],
                                      PARTIAL_ROT_DIM // 2)
                    slab = part * halves + c
                    if dil == 1:
                        out_ref[0, rows, slab * LANES:(slab + 1) * LANES] = chunk.astype(BF16)
                    else:
                        slab_ref[g - 1, slab, rows, :] = chunk
                between()
                yield
            if dil > 1:
                n = th // dil
                for r in range(dil):
                    for slab in range(3 * halves):
                        out_ref[r, h * n:(h + 1) * n, slab * LANES:(slab + 1) * LANES] = (
                            slab_ref[g - 1, slab, pl.ds(h * th + r, n, stride=dil), :].astype(BF16))
                yield

        raw_q = proj(0, A_Q_W)
        yield
        raw_kv = proj(A_Q_W, 2 * A_KV_W)
        yield
        q_tiles = []

        def store_q():
            qat = jnp.concatenate(q_tiles, axis=1).T.astype(BF16)
            for i in range(th // TQ):
                qat_ref[h * (th // TQ) + i] = qat[:, i * TQ:(i + 1) * TQ]

        def store_k():
            ka_ref[rows, :] = head_norm_rope(raw_kv[:, :A_KV_W], kg_ref[...]).astype(BF16)

        def store_v():
            vat = raw_kv[:, A_KV_W:].T.astype(BF16)
            pad_row = lax.broadcasted_iota(jnp.int32, (VT_ROWS - HEAD_DIM, th), 0)
            ones_row = jnp.where(pad_row == 0, 1.0, 0.0).astype(BF16)
            for hk in range(A_KV_HEADS):
                vat_ref[hk, 0:HEAD_DIM, rows] = vat[hk * HEAD_DIM:(hk + 1) * HEAD_DIM, :]
                vat_ref[hk, HEAD_DIM:VT_ROWS, rows] = ones_row

        steps = [functools.partial(lambda lo: q_tiles.append(head_norm_rope(raw_q[:, lo:lo + MXU_DIM], qg_ref[...])), lo)
                 for lo in range(0, A_Q_W, MXU_DIM)] + [store_q, store_k, store_v]

        def between():
            if steps:
                steps.pop(0)()

        for g in (2, 1, 0):
            yield from branch_b(g, between)
        while steps:
            between()
            yield

    _interleave([slab(h) for h in range(QKV_ROW_SPLITS)])


def _qkv_project(x, mod3, g1, w_qkv, qg, kg, bd, tabs):
    b, s, d = x.shape
    tm = TM_QKV
    ncols = w_qkv.shape[1]
    row_tab = pl.BlockSpec((tm, LANES), lambda i, bb: (i, 0))
    const2 = lambda shape: pl.BlockSpec(shape, lambda i, bb: (0, 0))
    tok = lambda w: pl.BlockSpec((None, tm, w), lambda i, bb: (bb, i, 0))
    out_shapes = (
        jax.ShapeDtypeStruct((b, s // TQ, A_Q_W, TQ), BF16),
        jax.ShapeDtypeStruct((b, s, A_KV_W), BF16),
        jax.ShapeDtypeStruct((b, s // TK, A_KV_HEADS, VT_ROWS, TK), BF16),
    ) + tuple(jax.ShapeDtypeStruct((b, dil, s // dil, 3 * B_OUT_W), BF16) for _, dil in B_GROUPS)
    out_specs = (
        pl.BlockSpec((None, tm // TQ, A_Q_W, TQ), lambda i, bb: (bb, i, 0, 0)),
        tok(A_KV_W),
        pl.BlockSpec((None, None, A_KV_HEADS, VT_ROWS, tm),
                     lambda i, bb: (bb, i // (TK // tm), 0, 0, i % (TK // tm))),
    ) + tuple(pl.BlockSpec((None, dil, tm // dil, 3 * B_OUT_W), lambda i, bb: (bb, 0, i, 0))
              for _, dil in B_GROUPS)
    return pl.pallas_call(
        _qkv_kernel,
        grid=(s // tm, b),
        in_specs=[tok(d),
                  pl.BlockSpec((None, 6, d), lambda i, bb: (bb, 0, 0)),
                  const2((1, d)),
                  const2((d, ncols)),
                  const2((1, LANES)), const2((1, LANES)), const2((MXU_DIM, MXU_DIM)),
                  row_tab, row_tab, row_tab, row_tab, row_tab, row_tab],
        out_specs=out_specs,
        out_shape=out_shapes,
        scratch_shapes=[pltpu.VMEM((len(B_GROUPS) - 1, 3 * B_OUT_W // LANES, tm, LANES), F32)],
        compiler_params=_params("parallel", "parallel"),
        name="qkv",
    )(x, mod3, g1, w_qkv, qg, kg, bd, *tabs)


def _gqa_kernel(qt_ref, k_ref, vt_ref, o_ref, qx_ref, s0_ref, s1_ref):
    nk = vt_ref.shape[0]
    ncol = A_GROUP * TQ
    n_units = GQA_TILES * A_KV_HEADS

    qx_ref[...] = jnp.zeros(qx_ref.shape, BF16)
    for t in range(GQA_TILES):
        for hk in range(A_KV_HEADS):
            for g in range(A_GROUP):
                h = hk * A_GROUP + g
                qx_ref[t * A_KV_HEADS + hk, hk * HEAD_DIM:(hk + 1) * HEAD_DIM, g * TQ:(g + 1) * TQ] = (
                    qt_ref[t, h * HEAD_DIM:(h + 1) * HEAD_DIM, :])

    s_refs = (s0_ref, s1_ref)

    def phase(score_unit, value_unit, parity, m_prev):
        qx = None if score_unit is None else qx_ref[score_unit]

        def body(j, carry):
            mrun, acc = carry
            rows = pl.ds(pl.multiple_of(j * TK, TK), TK)
            if score_unit is not None:
                st = jnp.dot(k_ref[rows, :], qx, preferred_element_type=F32)
                s_refs[parity][rows, :] = st
                mrun = jnp.maximum(mrun, jnp.max(st.reshape(TK // 8, 8, ncol), axis=0))
            if value_unit is not None:
                p = jnp.exp2((s_refs[1 - parity][rows, :] - m_prev).astype(BF16))
                acc = acc + jnp.dot(vt_ref[j, 1 - parity], p, preferred_element_type=F32)
            return mrun, acc

        init = (jnp.full((8, ncol), NEG_INF, F32), jnp.zeros((VT_ROWS, ncol), F32))
        mrun, acc = lax.fori_loop(0, nk, body, init, unroll=GQA_UNROLL)
        if value_unit is not None:
            out_t = acc[0:HEAD_DIM, :] / acc[HEAD_DIM:HEAD_DIM + 1, :]
            stacked = jnp.concatenate([out_t[:, g * TQ:(g + 1) * TQ] for g in range(A_GROUP)], axis=0)
            rows = pl.ds(pl.multiple_of((value_unit // A_KV_HEADS) * TQ, TQ), TQ)
            group_w = A_GROUP * HEAD_DIM
            o_ref[rows, (1 - parity) * group_w:(2 - parity) * group_w] = stacked.T.astype(BF16)
        return jnp.max(mrun, axis=0, keepdims=True)

    assert A_KV_HEADS == 2 and n_units % 2 == 0
    m = phase(0, None, 0, None)

    def unit_pair(i, m_prev):
        m_odd = phase(2 * i + 1, 2 * i, 1, m_prev)
        return phase(2 * i + 2, 2 * i + 1, 0, m_odd)

    m = lax.fori_loop(0, n_units // 2 - 1, unit_pair, m)
    m = phase(n_units - 1, n_units - 2, 1, m)
    phase(None, n_units - 1, 0, m)


def _global_attention(qat, ka, vat):
    b, nq = qat.shape[0], qat.shape[1]
    s = ka.shape[1]
    nk = vat.shape[1]
    ncol = A_GROUP * TQ
    return pl.pallas_call(
        _gqa_kernel,
        grid=(b, nq // GQA_TILES),
        in_specs=[pl.BlockSpec((None, GQA_TILES, A_Q_W, TQ), lambda bb, i: (bb, i, 0, 0)),
                  pl.BlockSpec((None, s, A_KV_W), lambda bb, i: (bb, 0, 0)),
                  pl.BlockSpec((None, nk, A_KV_HEADS, VT_ROWS, TK), lambda bb, i: (bb, 0, 0, 0, 0))],
        out_specs=pl.BlockSpec((None, GQA_TILES * TQ, A_Q_W), lambda bb, i: (bb, i, 0)),
        out_shape=jax.ShapeDtypeStruct((b, s, A_Q_W), BF16),
        scratch_shapes=[pltpu.VMEM((GQA_TILES * A_KV_HEADS, A_KV_W, ncol), BF16),
                        pltpu.VMEM((s, ncol), F32), pltpu.VMEM((s, ncol), F32)],
        compiler_params=_params("parallel", "parallel"),
        name="gqa",
    )(qat, ka, vat)


def _swa_kernel(qkv_ref, o_ref, lse_ref, sc_ref, p_ref, top_ref, bias_ref, *, radius):
    dil, seq = qkv_ref.shape[0], qkv_ref.shape[1]
    nblk = seq // BLK
    halves = B_OUT_W // LANES
    lane = lax.broadcasted_iota(jnp.int32, (BLK, LANES), 1)
    first_head = lane < HEAD_DIM
    head_mask = [jnp.where(first_head, 1.0, 0.0).astype(BF16), jnp.where(first_head, 0.0, 1.0).astype(BF16)]
    ones_win = jnp.ones((WIN, LANES), BF16)

    rel = (lax.broadcasted_iota(jnp.int32, (BLK, WIN), 0)
           - lax.broadcasted_iota(jnp.int32, (BLK, WIN), 1))
    for n in range(bias_ref.shape[0]):
        bias_ref[n] = jnp.where(jnp.abs(rel + n * radius) <= radius, 0.0, NEG_INF)

    def geometry(i):
        r = i // nblk
        i0 = pl.multiple_of((i % nblk) * BLK, BLK)
        ws = pl.multiple_of(jnp.clip(i0 - radius, 0, seq - WIN), radius)
        return r, i0, ws

    def scores(i, slot):
        r, i0, ws = geometry(i)
        bias = bias_ref[(i0 - ws) // radius]
        for c in range(halves):
            q = qkv_ref[r, pl.ds(i0, BLK), c * LANES:(c + 1) * LANES]
            k = qkv_ref[r, pl.ds(ws, WIN), B_OUT_W + c * LANES:B_OUT_W + (c + 1) * LANES]
            for hh in range(2):
                sc_ref[slot, 2 * c + hh] = lax.dot_general(
                    q * head_mask[hh], k, (((1,), (1,)), ((), ())), preferred_element_type=F32) + bias

    def probabilities(slot):
        for c in range(halves):
            tops = []
            for hh in range(2):
                sc = sc_ref[slot, 2 * c + hh]
                m = jnp.max(sc, axis=-1, keepdims=True)
                p_ref[slot, 2 * c + hh] = jnp.exp2((sc - m).astype(BF16))
                tops.append(m)
            top_ref[slot, c] = jnp.where(first_head, tops[0], tops[1])

    def outputs(i, slot):
        r, i0, ws = geometry(i)
        for c in range(halves):
            v = qkv_ref[r, pl.ds(ws, WIN), 2 * B_OUT_W + c * LANES:2 * B_OUT_W + (c + 1) * LANES]
            v_ones = jnp.concatenate([v, ones_win], axis=1)
            outs = [jnp.dot(p_ref[slot, 2 * c + hh], v_ones, preferred_element_type=F32) for hh in range(2)]
            num = jnp.where(first_head, outs[0][:, :LANES], outs[1][:, :LANES])
            den = jnp.where(first_head, outs[0][:, LANES:], outs[1][:, LANES:])
            classes = o_ref.shape[1]
            step = dil // classes
            rows = pl.ds(i0 * step + r // classes, BLK, stride=step) if step > 1 else pl.ds(i0, BLK)
            o_ref[c, r % classes, rows, :] = num / den
            lse_ref[c, r % classes, rows, :] = (top_ref[slot, c] + jnp.log2(den)) * LN2

    total = dil * nblk
    scores(0, 0)
    probabilities(0)
    scores(1, 1)

    def body(i, _):
        outputs(i - 2, i % 2)
        probabilities((i - 1) % 2)
        scores(i, i % 2)
        return 0

    lax.fori_loop(2, total, body, 0, unroll=2)
    outputs(total - 2, total % 2)
    probabilities((total - 1) % 2)
    outputs(total - 1, (total - 1) % 2)


def _banded_attention(qkv, group):
    window, dil = B_GROUPS[group]
    b, _, seq, width = qkv.shape
    radius = window // 2 // dil
    halves = B_OUT_W // LANES
    classes = max(1, dil // SWA_MAX_STORE_STRIDE)
    out_spec = pl.BlockSpec((None, halves, classes, dil * seq // classes, LANES), lambda bb: (bb, 0, 0, 0, 0))
    out_shape = jax.ShapeDtypeStruct((b, halves, classes, dil * seq // classes, LANES), F32)
    return pl.pallas_call(
        functools.partial(_swa_kernel, radius=radius),
        grid=(b,),
        in_specs=[pl.BlockSpec((None, dil, seq, width), lambda bb: (bb, 0, 0, 0))],
        out_specs=(out_spec, out_spec),
        out_shape=(out_shape, out_shape),
        scratch_shapes=[pltpu.VMEM((2, B_HEADS_PER_GROUP, BLK, WIN), F32),
                        pltpu.VMEM((2, B_HEADS_PER_GROUP, BLK, WIN), BF16),
                        pltpu.VMEM((2, B_OUT_W // LANES, BLK, LANES), F32),
                        pltpu.VMEM(((WIN - BLK) // radius + 1, BLK, WIN), F32)],
        compiler_params=_params("parallel"),
        name=f"swa{group}",
    )(qkv)


def _mix_kernel(x_ref, mod_ref, g1_ref, attn_ref, o0_ref, o1_ref, o2_ref, l0_ref, l1_ref, l2_ref,
                wg_ref, bg_ref, wpa_ref, wpb_ref, wo_ref, out_ref, nat_ref):
    d = x_ref.shape[-1]
    halves = B_OUT_W // LANES
    th = x_ref.shape[0] // MIX_ROW_SPLITS

    def slab(h):
        rows = slice(h * th, (h + 1) * th)
        x = x_ref[rows, :]
        ya = jnp.dot(attn_ref[rows, :], wpa_ref[...], preferred_element_type=F32)
        yield
        u = _modulated_norm(x, g1_ref[...], mod_ref[0:1, :], mod_ref[1:2, :])

        def token_order(src, c, slot):
            classes = src.shape[1]
            if classes == 1:
                return src[c, 0, rows, :]
            n = th // classes
            for k in range(classes):
                nat_ref[slot, pl.ds(h * th + k, n, stride=classes), :] = src[c, k, h * n:(h + 1) * n, :]
            return nat_ref[slot, rows, :]

        def combine(c):
            groups = len(B_GROUPS)
            lses = [token_order(l_ref, c, (2 * c) * groups + g) for g, l_ref in enumerate((l0_ref, l1_ref, l2_ref))]
            outs = [token_order(o_ref, c, (2 * c + 1) * groups + g)
                    for g, o_ref in enumerate((o0_ref, o1_ref, o2_ref))]
            top = jnp.maximum(jnp.maximum(lses[0], lses[1]), lses[2])
            es = [jnp.exp(l - top) for l in lses]
            den = es[0] + es[1] + es[2]
            return (sum(e * o for e, o in zip(es, outs)) / den).astype(BF16)

        def twice_gates(j, parts=4):
            w = 2 * d // parts
            cols = slice(j * w, (j + 1) * w)
            return jnp.tanh(jnp.dot(u, wg_ref[:, cols], preferred_element_type=F32) + bg_ref[:, cols]) + 1.0

        gate, comb = [], []
        for j in range(4):
            gate.append(twice_gates(j))
            yield
            if j < halves:
                comb.append(combine(j))
                yield
            if j == 1:
                gated_a = jnp.concatenate(gate[:2], axis=1) * ya
        yb = jnp.dot(jnp.concatenate(comb, axis=1), wpb_ref[...], preferred_element_type=F32)
        yield
        merged = (gated_a + jnp.concatenate(gate[2:], axis=1) * yb).astype(BF16)
        yield
        mix = jnp.dot(merged, wo_ref[...], preferred_element_type=F32)
        out_ref[rows, :] = x + mod_ref[2:3, :] * mix

    _interleave([slab(h) for h in range(MIX_ROW_SPLITS)])


def _mix(x, mod3, g1, attn, os_, lses, wg, bg, wpa, wpb, wo):
    b, s, d = x.shape
    tm = TM_MIX
    tok = lambda w: pl.BlockSpec((None, tm, w), lambda bb, i: (bb, i, 0))
    const2 = lambda shape: pl.BlockSpec(shape, lambda bb, i: (0, 0))
    halves = B_OUT_W // LANES
    sub = [pl.BlockSpec((None, halves, o.shape[2], tm // o.shape[2], LANES), lambda bb, i: (bb, 0, 0, i, 0))
           for o in os_]
    return pl.pallas_call(
        _mix_kernel,
        grid=(b, s // tm),
        in_specs=[tok(d),
                  pl.BlockSpec((None, 6, d), lambda bb, i: (bb, 0, 0)),
                  const2((1, d)),
                  tok(A_Q_W),
                  *sub, *sub,
                  const2(wg.shape), const2(bg.shape), const2(wpa.shape), const2(wpb.shape),
                  const2(wo.shape)],
        out_specs=tok(d),
        out_shape=jax.ShapeDtypeStruct((b, s, d), F32),
        scratch_shapes=[pltpu.VMEM((2 * halves * len(B_GROUPS), tm, LANES), F32)],
        compiler_params=_params("parallel", "parallel"),
        name="mix",
    )(x, mod3, g1, attn, *os_, *lses, wg, bg, wpa, wpb, wo)


def _ffn_kernel(x_ref, mod_ref, g2_ref, win_ref, wout_ref, fg_ref, out_ref, *, final, bounds):
    d_ff = wout_ref.shape[0]
    th = x_ref.shape[0] // FFN_ROW_SPLITS

    def slab(h):
        rows = slice(h * th, (h + 1) * th)
        x = x_ref[rows, :]
        u = _modulated_norm(x, g2_ref[...], mod_ref[3:4, :], mod_ref[4:5, :])
        yield
        acc = jnp.zeros(x.shape, F32)
        for lo, hi in zip(bounds[:-1], bounds[1:]):
            hg = jnp.dot(u, win_ref[:, lo:hi], preferred_element_type=F32)
            yield
            hu = jnp.dot(u, win_ref[:, d_ff + lo:d_ff + hi], preferred_element_type=F32)
            yield
            act = ((hg * (jnp.tanh(hg) + 1.0)) * hu).astype(BF16)
            acc += jnp.dot(act, wout_ref[lo:hi, :], preferred_element_type=F32)
            yield
        y = x + mod_ref[5:6, :] * acc
        if final:
            ms = jnp.mean(y * y, axis=-1, keepdims=True)
            y = (y * lax.rsqrt(ms + EPS)) * fg_ref[...]
        out_ref[rows, :] = y

    _interleave([slab(h) for h in range(FFN_ROW_SPLITS)])


def _ffn(x, mod3, g2, win, wout, fg, final):
    b, s, d = x.shape
    tm = TM_FFN
    d_ff = wout.shape[0]
    tiles = pl.cdiv(d_ff, MXU_DIM)
    per_chunk = pl.cdiv(tiles, FFN_CHUNKS) * MXU_DIM
    bounds = tuple(min(d_ff, i * per_chunk) for i in range(FFN_CHUNKS + 1))
    tok = pl.BlockSpec((None, tm, d), lambda bb, i: (bb, i, 0))
    resident = lambda shape: pl.BlockSpec(shape, lambda bb, i: (0, 0), pipeline_mode=pl.Buffered(1))
    return pl.pallas_call(
        functools.partial(_ffn_kernel, final=final, bounds=bounds),
        grid=(b, s // tm),
        in_specs=[tok,
                  pl.BlockSpec((None, 6, d), lambda bb, i: (bb, 0, 0)),
                  pl.BlockSpec((1, d), lambda bb, i: (0, 0)),
                  resident(win.shape), resident(wout.shape),
                  pl.BlockSpec((1, d), lambda bb, i: (0, 0))],
        out_specs=tok,
        out_shape=jax.ShapeDtypeStruct((b, s, d), F32),
        compiler_params=_params("parallel", "parallel"),
        name="ffn",
    )(x, mod3, g2, win, wout, fg)


def _inv_freq(dim, theta):
    return theta ** (-jnp.arange(0, dim, 2, dtype=F32) / dim)


def _rope_tables(s):
    in_head = np.arange(LANES) % HEAD_DIM
    half = HEAD_DIM // 2
    quarter = half // 2
    inv_a = (AXIAL_THETA ** (-jnp.asarray(2 * (in_head % quarter), F32) / half))[None, :]
    ang_row = jnp.arange(s // GRID_W, dtype=jnp.int32).astype(F32)[:, None] * inv_a
    ang_col = jnp.arange(GRID_W, dtype=jnp.int32).astype(F32)[:, None] * inv_a
    by_row = (in_head < half)[None, None, :]
    first_a = ((in_head % half) < quarter)[None, None, :]

    small = lax.optimization_barrier(
        (jnp.cos(ang_row), jnp.cos(ang_col), jnp.sin(ang_row), jnp.sin(ang_col)))

    def grid(of_row, of_col):
        return jnp.where(by_row, of_row[:, None, :], of_col[None, :, :])

    a_cos = grid(small[0], small[1]).reshape(s, LANES)
    a_sin = grid(small[2], small[3])
    a_lo = jnp.where(first_a, -a_sin, 0.0).reshape(s, LANES)
    a_hi = jnp.where(first_a, 0.0, a_sin).reshape(s, LANES)
    t = jnp.arange(s, dtype=jnp.int32)
    ap = t.astype(F32)[:, None] * _inv_freq(PARTIAL_ROT_DIM, PARTIAL_THETA)[None, :]
    rest = HEAD_DIM - PARTIAL_ROT_DIM
    zp = jnp.zeros_like(ap)
    p_cos = jnp.concatenate([jnp.cos(ap), jnp.cos(ap), jnp.ones((s, rest), F32)], axis=1)
    p_lo = jnp.concatenate([-jnp.sin(ap), zp, jnp.zeros((s, rest), F32)], axis=1)
    p_hi = jnp.concatenate([zp, jnp.sin(ap), jnp.zeros((s, rest), F32)], axis=1)
    two = lambda a: jnp.tile(a, (1, LANES // HEAD_DIM))
    return a_cos, a_lo, a_hi, two(p_cos), two(p_lo), two(p_hi)


def kernel(x, c, w_ada, b_ada, norm1_g, w_qkv, q_norm_a, k_norm_a, w_proj_a, w_proj_b, w_gate,
           b_gate, w_o, norm2_g, w_ffn_in, w_ffn_out, final_norm_g):
    b, s, d = x.shape
    depth = w_ada.shape[0]
    assert s % TM_QKV == 0 and s % (B_GROUPS[-1][1] * WIN) == 0 and TK % TM_QKV == 0 and s % TK == 0
    tabs = _rope_tables(s)
    heads_per_vreg = LANES // HEAD_DIM
    lane_head = np.arange(MXU_DIM) // HEAD_DIM
    bd = jnp.asarray(lane_head[:, None] == lane_head[None, :], BF16)
    b_q_lo = A_Q_W + 2 * A_KV_W
    col_scale = jnp.ones((w_qkv.shape[-1],), F32).at[b_q_lo:b_q_lo + B_W].set(SCALE * LOG2E)
    c_pad = jnp.pad(c, ((0, 16 - b), (0, 0)))
    for l in range(depth):
        mod3 = _modulation(c_pad, w_ada, b_ada[l], l)[:b].reshape(b, 6, d)
        qg = jnp.tile(q_norm_a[l] * (SCALE * LOG2E), heads_per_vreg).reshape(1, LANES)
        kg = jnp.tile(k_norm_a[l], heads_per_vreg).reshape(1, LANES)
        g1 = norm1_g[l].reshape(1, d)
        wq = (w_qkv[l] * col_scale[None, :]).astype(BF16)
        qat, ka, vat, *groups = _qkv_project(x, mod3, g1, wq, qg, kg, bd, tabs)
        attn = _global_attention(qat, ka, vat)
        branch = [_banded_attention(qkv_g, g) for g, qkv_g in enumerate(groups)]
        x = _mix(x, mod3, g1, attn, [o for o, _ in branch], [e for _, e in branch],
                 (0.5 * w_gate[l]).astype(BF16), (0.5 * b_gate[l]).reshape(1, -1), w_proj_a[l].astype(BF16),
                 w_proj_b[l].astype(BF16), (0.5 * w_o[l]).astype(BF16))
        d_ff = w_ffn_out.shape[1]
        half_gate = jnp.where(jnp.arange(2 * d_ff) < d_ff, 0.5, 1.0).astype(F32)
        x = _ffn(x, mod3, norm2_g[l].reshape(1, d), (w_ffn_in[l] * half_gate[None, :]).astype(BF16),
                 w_ffn_out[l].astype(BF16), final_norm_g.reshape(1, d), final=(l == depth - 1))
    return x
```

```python
import functools

import jax
import jax.numpy as jnp
import numpy as np
from jax import lax
from jax.experimental import pallas as pl
from jax.experimental.pallas import tpu as pltpu

HEAD_DIM = 64
A_Q_HEADS = 8
A_KV_HEADS = 2
A_GROUP = A_Q_HEADS // A_KV_HEADS
B_GROUPS = ((128, 1), (512, 4), (2048, 16))
B_HEADS_PER_GROUP = 4
B_HEADS = B_HEADS_PER_GROUP * len(B_GROUPS)
A_Q_W = A_Q_HEADS * HEAD_DIM
A_KV_W = A_KV_HEADS * HEAD_DIM
B_W = B_HEADS * HEAD_DIM
B_OUT_W = B_HEADS_PER_GROUP * HEAD_DIM
GRID_W = 64
AXIAL_THETA = 10000.0
PARTIAL_THETA = 500000.0
PARTIAL_ROT_DIM = HEAD_DIM // 4
EPS = 1e-6
NEG_INF = -1e30
SCALE = HEAD_DIM ** -0.5
LOG2E = 1.4426950408889634
LN2 = 0.6931471805599453

LANES = 128
MXU_DIM = 256
VMEM_LIMIT = 56 * 1024 * 1024

F32 = jnp.float32
BF16 = jnp.bfloat16

TM_QKV = 1024
QKV_ROW_SPLITS = 2
TQ = 128
TK = 1024
TM_MIX = 1024
MIX_ROW_SPLITS = 1
TM_FFN = 1024
FFN_CHUNKS = 3
FFN_ROW_SPLITS = 4
BLK = 128
WIN = 256
SWA_MAX_STORE_STRIDE = 4
VT_ROWS = HEAD_DIM + 16
GQA_UNROLL = 2
GQA_TILES = 32


def _params(*sem):
    return pltpu.CompilerParams(dimension_semantics=sem, vmem_limit_bytes=VMEM_LIMIT)


def _mod_kernel(c_ref, w_ref, b_ref, o_ref):
    c = c_ref[...]
    cond = c * jax.nn.sigmoid(c)
    c_hi = cond.astype(BF16)
    c_lo = (cond - c_hi.astype(F32)).astype(BF16)
    w = w_ref[...]
    w_hi = w.astype(BF16)
    w_lo = (w - w_hi.astype(F32)).astype(BF16)
    acc = jnp.dot(c_hi, w_hi, preferred_element_type=F32)
    acc += jnp.dot(c_hi, w_lo, preferred_element_type=F32)
    acc += jnp.dot(c_lo, w_hi, preferred_element_type=F32)
    o_ref[...] = acc + b_ref[...]


def _modulation(c_pad, w_ada, b_ada, layer):
    rows, d = c_pad.shape
    n = w_ada.shape[2]
    tn = 1024
    return pl.pallas_call(
        _mod_kernel,
        grid=(n // tn,),
        in_specs=[pl.BlockSpec((rows, d), lambda j: (0, 0)),
                  pl.BlockSpec((None, d, tn), lambda j: (layer, 0, j)),
                  pl.BlockSpec((1, tn), lambda j: (0, j))],
        out_specs=pl.BlockSpec((rows, tn), lambda j: (0, j)),
        out_shape=jax.ShapeDtypeStruct((rows, n), F32),
        compiler_params=_params("parallel"),
        name="mod",
    )(c_pad, w_ada, b_ada.reshape(1, n))


def _modulated_norm(x, gain, shift, scale):
    ms = jnp.mean(x * x, axis=-1, keepdims=True)
    return ((x * lax.rsqrt(ms + EPS)) * (gain * (1.0 + scale)) + shift).astype(BF16)


def _interleave(programs):
    done = object()
    live = []
    for prog in programs:
        live.append(prog)
        live = [p for p in live if next(p, done) is not done]
    while live:
        live = [p for p in live if next(p, done) is not done]


def _rope(y, cos, sin_lo, sin_hi, shift):
    return (y * cos + pltpu.roll(y, LANES - shift, 1) * sin_lo
            + pltpu.roll(y, shift, 1) * sin_hi)


def _qkv_kernel(x_ref, mod_ref, g1_ref, w_ref, qg_ref, kg_ref, bd_ref,
                ac_ref, asl_ref, ash_ref, pc_ref, psl_ref, psh_ref,
                qat_ref, ka_ref, vat_ref, grp0_ref, grp1_ref, grp2_ref, slab_ref):
    tm = x_ref.shape[0]
    th = tm // QKV_ROW_SPLITS
    def slab(h):
        rows = slice(h * th, (h + 1) * th)
        u = _modulated_norm(x_ref[rows, :], g1_ref[...], mod_ref[0:1, :], mod_ref[1:2, :])
        yield

        def proj(lo, width):
            return jnp.dot(u, w_ref[:, lo:lo + width], preferred_element_type=F32)

        def head_norm_rope(wide, gain):
            out = []
            for lo in range(0, wide.shape[1], MXU_DIM):
                tile = wide[:, lo:lo + MXU_DIM]
                w = tile.shape[1]
                ss = jnp.dot((tile * tile).astype(BF16), bd_ref[:w, :w], preferred_element_type=F32)
                for c in range(w // LANES):
                    cols = slice(c * LANES, (c + 1) * LANES)
                    y = (tile[:, cols] * lax.rsqrt(ss[:, cols] * (1.0 / HEAD_DIM) + EPS)) * gain
                    out.append(_rope(y, ac_ref[rows, :], asl_ref[rows, :], ash_ref[rows, :], HEAD_DIM // 4))
            return out[0] if len(out) == 1 else jnp.concatenate(out, axis=1)

        def branch_b(g, between):
            out_ref, dil = (grp0_ref, grp1_ref, grp2_ref)[g], B_GROUPS[g][1]
            base = A_Q_W + 2 * A_KV_W
            halves = B_OUT_W // LANES
            for part in range(3):
                wide = proj(base + part * B_W + g * B_OUT_W, B_OUT_W)
                for c in range(halves):
                    chunk = wide[:, c * LANES:(c + 1) * LANES]
                    if part < 2:
                        chunk = _rope(chunk, pc_ref[rows, :], psl_ref[rows, :], psh_ref[rows, :],
                                      PARTIAL_ROT_DIM // 2)
                    slab = part * halves + c
                    if dil == 1:
                        out_ref[0, rows, slab * LANES:(slab + 1) * LANES] = chunk.astype(BF16)
                    else:
                        slab_ref[g - 1, slab, rows, :] = chunk
                between()
                yield
            if dil > 1:
                n = th // dil
                for r in range(dil):
                    for slab in range(3 * halves):
                        out_ref[r, h * n:(h + 1) * n, slab * LANES:(slab + 1) * LANES] = (
                            slab_ref[g - 1, slab, pl.ds(h * th + r, n, stride=dil), :].astype(BF16))
                yield

        raw_q = proj(0, A_Q_W)
        yield
        raw_kv = proj(A_Q_W, 2 * A_KV_W)
        yield
        q_tiles = []

        def store_q():
            qat = jnp.concatenate(q_tiles, axis=1).T.astype(BF16)
            for i in range(th // TQ):
                qat_ref[h * (th // TQ) + i] = qat[:, i * TQ:(i + 1) * TQ]

        def store_k():
            ka_ref[rows, :] = head_norm_rope(raw_kv[:, :A_KV_W], kg_ref[...]).astype(BF16)

        def store_v():
            vat = raw_kv[:, A_KV_W:].T.astype(BF16)
            pad_row = lax.broadcasted_iota(jnp.int32, (VT_ROWS - HEAD_DIM, th), 0)
            ones_row = jnp.where(pad_row == 0, 1.0, 0.0).astype(BF16)
            for hk in range(A_KV_HEADS):
                vat_ref[hk, 0:HEAD_DIM, rows] = vat[hk * HEAD_DIM:(hk + 1) * HEAD_DIM, :]
                vat_ref[hk, HEAD_DIM:VT_ROWS, rows] = ones_row

        steps = [functools.partial(lambda lo: q_tiles.append(head_norm_rope(raw_q[:, lo:lo + MXU_DIM], qg_ref[...])), lo)
                 for lo in range(0, A_Q_W, MXU_DIM)] + [store_q, store_k, store_v]

        def between():
            if steps:
                steps.pop(0)()

        for g in (2, 1, 0):
            yield from branch_b(g, between)
        while steps:
            between()
            yield

    _interleave([slab(h) for h in range(QKV_ROW_SPLITS)])


def _qkv_project(x, mod3, g1, w_qkv, qg, kg, bd, tabs):
    b, s, d = x.shape
    tm = TM_QKV
    ncols = w_qkv.shape[1]
    row_tab = pl.BlockSpec((tm, LANES), lambda i, bb: (i, 0))
    const2 = lambda shape: pl.BlockSpec(shape, lambda i, bb: (0, 0))
    tok = lambda w: pl.BlockSpec((None, tm, w), lambda i, bb: (bb, i, 0))
    out_shapes = (
        jax.ShapeDtypeStruct((b, s // TQ, A_Q_W, TQ), BF16),
        jax.ShapeDtypeStruct((b, s, A_KV_W), BF16),
        jax.ShapeDtypeStruct((b, s // TK, A_KV_HEADS, VT_ROWS, TK), BF16),
    ) + tuple(jax.ShapeDtypeStruct((b, dil, s // dil, 3 * B_OUT_W), BF16) for _, dil in B_GROUPS)
    out_specs = (
        pl.BlockSpec((None, tm // TQ, A_Q_W, TQ), lambda i, bb: (bb, i, 0, 0)),
        tok(A_KV_W),
        pl.BlockSpec((None, None, A_KV_HEADS, VT_ROWS, tm),
                     lambda i, bb: (bb, i // (TK // tm), 0, 0, i % (TK // tm))),
    ) + tuple(pl.BlockSpec((None, dil, tm // dil, 3 * B_OUT_W), lambda i, bb: (bb, 0, i, 0))
              for _, dil in B_GROUPS)
    return pl.pallas_call(
        _qkv_kernel,
        grid=(s // tm, b),
        in_specs=[tok(d),
                  pl.BlockSpec((None, 6, d), lambda i, bb: (bb, 0, 0)),
                  const2((1, d)),
                  const2((d, ncols)),
                  const2((1, LANES)), const2((1, LANES)), const2((MXU_DIM, MXU_DIM)),
                  row_tab, row_tab, row_tab, row_tab, row_tab, row_tab],
        out_specs=out_specs,
        out_shape=out_shapes,
        scratch_shapes=[pltpu.VMEM((len(B_GROUPS) - 1, 3 * B_OUT_W // LANES, tm, LANES), F32)],
        compiler_params=_params("parallel", "parallel"),
        name="qkv",
    )(x, mod3, g1, w_qkv, qg, kg, bd, *tabs)


def _gqa_kernel(qt_ref, k_ref, vt_ref, o_ref, qx_ref, s0_ref, s1_ref):
    nk = vt_ref.shape[0]
    ncol = A_GROUP * TQ
    n_units = GQA_TILES * A_KV_HEADS

    qx_ref[...] = jnp.zeros(qx_ref.shape, BF16)
    for t in range(GQA_TILES):
        for hk in range(A_KV_HEADS):
            for g in range(A_GROUP):
                h = hk * A_GROUP + g
                qx_ref[t * A_KV_HEADS + hk, hk * HEAD_DIM:(hk + 1) * HEAD_DIM, g * TQ:(g + 1) * TQ] = (
                    qt_ref[t, h * HEAD_DIM:(h + 1) * HEAD_DIM, :])

    s_refs = (s0_ref, s1_ref)

    def phase(score_unit, value_unit, parity, m_prev):
        qx = None if score_unit is None else qx_ref[score_unit]

        def body(j, carry):
            mrun, acc = carry
            rows = pl.ds(pl.multiple_of(j * TK, TK), TK)
            if score_unit is not None:
                st = jnp.dot(k_ref[rows, :], qx, preferred_element_type=F32)
                s_refs[parity][rows, :] = st
                mrun = jnp.maximum(mrun, jnp.max(st.reshape(TK // 8, 8, ncol), axis=0))
            if value_unit is not None:
                p = jnp.exp2((s_refs[1 - parity][rows, :] - m_prev).astype(BF16))
                acc = acc + jnp.dot(vt_ref[j, 1 - parity], p, preferred_element_type=F32)
            return mrun, acc

        init = (jnp.full((8, ncol), NEG_INF, F32), jnp.zeros((VT_ROWS, ncol), F32))
        mrun, acc = lax.fori_loop(0, nk, body, init, unroll=GQA_UNROLL)
        if value_unit is not None:
            out_t = acc[0:HEAD_DIM, :] / acc[HEAD_DIM:HEAD_DIM + 1, :]
            stacked = jnp.concatenate([out_t[:, g * TQ:(g + 1) * TQ] for g in range(A_GROUP)], axis=0)
            group_w = A_GROUP * HEAD_DIM
            o_ref[value_unit // A_KV_HEADS, (1 - parity) * group_w:(2 - parity) * group_w, :] = (
                stacked.astype(BF16))
        return jnp.max(mrun, axis=0, keepdims=True)

    assert A_KV_HEADS == 2 and n_units % 2 == 0
    m = phase(0, None, 0, None)

    def unit_pair(i, m_prev):
        m_odd = phase(2 * i + 1, 2 * i, 1, m_prev)
        return phase(2 * i + 2, 2 * i + 1, 0, m_odd)

    m = lax.fori_loop(0, n_units // 2 - 1, unit_pair, m)
    m = phase(n_units - 1, n_units - 2, 1, m)
    phase(None, n_units - 1, 0, m)


def _global_attention(qat, ka, vat):
    b, nq = qat.shape[0], qat.shape[1]
    s = ka.shape[1]
    nk = vat.shape[1]
    ncol = A_GROUP * TQ
    return pl.pallas_call(
        _gqa_kernel,
        grid=(b, nq // GQA_TILES),
        in_specs=[pl.BlockSpec((None, GQA_TILES, A_Q_W, TQ), lambda bb, i: (bb, i, 0, 0)),
                  pl.BlockSpec((None, s, A_KV_W), lambda bb, i: (bb, 0, 0)),
                  pl.BlockSpec((None, nk, A_KV_HEADS, VT_ROWS, TK), lambda bb, i: (bb, 0, 0, 0, 0))],
        out_specs=pl.BlockSpec((None, GQA_TILES, A_Q_W, TQ), lambda bb, i: (bb, i, 0, 0)),
        out_shape=jax.ShapeDtypeStruct((b, s // TQ, A_Q_W, TQ), BF16),
        scratch_shapes=[pltpu.VMEM((GQA_TILES * A_KV_HEADS, A_KV_W, ncol), BF16),
                        pltpu.VMEM((s, ncol), F32), pltpu.VMEM((s, ncol), F32)],
        compiler_params=_params("parallel", "parallel"),
        name="gqa",
    )(qat, ka, vat)


def _swa_kernel(qkv_ref, o_ref, lse_ref, sc_ref, p_ref, top_ref, bias_ref, *, radius):
    dil, seq = qkv_ref.shape[0], qkv_ref.shape[1]
    nblk = seq // BLK
    halves = B_OUT_W // LANES
    lane = lax.broadcasted_iota(jnp.int32, (BLK, LANES), 1)
    first_head = lane < HEAD_DIM
    head_mask = [jnp.where(first_head, 1.0, 0.0).astype(BF16), jnp.where(first_head, 0.0, 1.0).astype(BF16)]
    ones_win = jnp.ones((WIN, LANES), BF16)

    rel = (lax.broadcasted_iota(jnp.int32, (BLK, WIN), 0)
           - lax.broadcasted_iota(jnp.int32, (BLK, WIN), 1))
    for n in range(bias_ref.shape[0]):
        bias_ref[n] = jnp.where(jnp.abs(rel + n * radius) <= radius, 0.0, NEG_INF)

    def geometry(i):
        r = i // nblk
        i0 = pl.multiple_of((i % nblk) * BLK, BLK)
        ws = pl.multiple_of(jnp.clip(i0 - radius, 0, seq - WIN), radius)
        return r, i0, ws

    def scores(i, slot):
        r, i0, ws = geometry(i)
        bias = bias_ref[(i0 - ws) // radius]
        for c in range(halves):
            q = qkv_ref[r, pl.ds(i0, BLK), c * LANES:(c + 1) * LANES]
            k = qkv_ref[r, pl.ds(ws, WIN), B_OUT_W + c * LANES:B_OUT_W + (c + 1) * LANES]
            for hh in range(2):
                sc_ref[slot, 2 * c + hh] = lax.dot_general(
                    q * head_mask[hh], k, (((1,), (1,)), ((), ())), preferred_element_type=F32) + bias

    def probabilities(slot):
        for c in range(halves):
            tops = []
            for hh in range(2):
                sc = sc_ref[slot, 2 * c + hh]
                m = jnp.max(sc, axis=-1, keepdims=True)
                p_ref[slot, 2 * c + hh] = jnp.exp2((sc - m).astype(BF16))
                tops.append(m)
            top_ref[slot, c] = jnp.where(first_head, tops[0], tops[1])

    def outputs(i, slot):
        r, i0, ws = geometry(i)
        for c in range(halves):
            v = qkv_ref[r, pl.ds(ws, WIN), 2 * B_OUT_W + c * LANES:2 * B_OUT_W + (c + 1) * LANES]
            v_ones = jnp.concatenate([v, ones_win], axis=1)
            outs = [jnp.dot(p_ref[slot, 2 * c + hh], v_ones, preferred_element_type=F32) for hh in range(2)]
            num = jnp.where(first_head, outs[0][:, :LANES], outs[1][:, :LANES])
            den = jnp.where(first_head, outs[0][:, LANES:], outs[1][:, LANES:])
            classes = o_ref.shape[1]
            step = dil // classes
            rows = pl.ds(i0 * step + r // classes, BLK, stride=step) if step > 1 else pl.ds(i0, BLK)
            o_ref[c, r % classes, rows, :] = num / den
            lse_ref[c, r % classes, rows, :] = (top_ref[slot, c] + jnp.log2(den)) * LN2

    total = dil * nblk
    scores(0, 0)
    probabilities(0)
    scores(1, 1)

    def body(i, _):
        outputs(i - 2, i % 2)
        probabilities((i - 1) % 2)
        scores(i, i % 2)
        return 0

    lax.fori_loop(2, total, body, 0, unroll=2)
    outputs(total - 2, total % 2)
    probabilities((total - 1) % 2)
    outputs(total - 1, (total - 1) % 2)


def _banded_attention(qkv, group):
    window, dil = B_GROUPS[group]
    b, _, seq, width = qkv.shape
    radius = window // 2 // dil
    halves = B_OUT_W // LANES
    classes = max(1, dil // SWA_MAX_STORE_STRIDE)
    out_spec = pl.BlockSpec((None, halves, classes, dil * seq // classes, LANES), lambda bb: (bb, 0, 0, 0, 0))
    out_shape = jax.ShapeDtypeStruct((b, halves, classes, dil * seq // classes, LANES), F32)
    return pl.pallas_call(
        functools.partial(_swa_kernel, radius=radius),
        grid=(b,),
        in_specs=[pl.BlockSpec((None, dil, seq, width), lambda bb: (bb, 0, 0, 0))],
        out_specs=(out_spec, out_spec),
        out_shape=(out_shape, out_shape),
        scratch_shapes=[pltpu.VMEM((2, B_HEADS_PER_GROUP, BLK, WIN), F32),
                        pltpu.VMEM((2, B_HEADS_PER_GROUP, BLK, WIN), BF16),
                        pltpu.VMEM((2, B_OUT_W // LANES, BLK, LANES), F32),
                        pltpu.VMEM(((WIN - BLK) // radius + 1, BLK, WIN), F32)],
        compiler_params=_params("parallel"),
        name=f"swa{group}",
    )(qkv)


def _mix_kernel(x_ref, mod_ref, g1_ref, attn_ref, o0_ref, o1_ref, o2_ref, l0_ref, l1_ref, l2_ref,
                wg_ref, bg_ref, wpa_ref, wpb_ref, wo_ref, out_ref, nat_ref):
    d = x_ref.shape[-1]
    halves = B_OUT_W // LANES
    th = x_ref.shape[0] // MIX_ROW_SPLITS

    def slab(h):
        rows = slice(h * th, (h + 1) * th)
        x = x_ref[rows, :]
        tiles = th // TQ
        ya = jnp.concatenate(
            [lax.dot_general(attn_ref[h * tiles + i], wpa_ref[...], (((0,), (0,)), ((), ())),
                             preferred_element_type=F32) for i in range(tiles)], axis=0)
        yield
        u = _modulated_norm(x, g1_ref[...], mod_ref[0:1, :], mod_ref[1:2, :])

        def token_order(src, c, slot):
            classes = src.shape[1]
            if classes == 1:
                return src[c, 0, rows, :]
            n = th // classes
            for k in range(classes):
                nat_ref[slot, pl.ds(h * th + k, n, stride=classes), :] = src[c, k, h * n:(h + 1) * n, :]
            return nat_ref[slot, rows, :]

        def combine(c):
            groups = len(B_GROUPS)
            lses = [token_order(l_ref, c, (2 * c) * groups + g) for g, l_ref in enumerate((l0_ref, l1_ref, l2_ref))]
            outs = [token_order(o_ref, c, (2 * c + 1) * groups + g)
                    for g, o_ref in enumerate((o0_ref, o1_ref, o2_ref))]
            top = jnp.maximum(jnp.maximum(lses[0], lses[1]), lses[2])
            es = [jnp.exp(l - top) for l in lses]
            den = es[0] + es[1] + es[2]
            return (sum(e * o for e, o in zip(es, outs)) / den).astype(BF16)

        def twice_gates(j, parts=4):
            w = 2 * d // parts
            cols = slice(j * w, (j + 1) * w)
            return jnp.tanh(jnp.dot(u, wg_ref[:, cols], preferred_element_type=F32) + bg_ref[:, cols]) + 1.0

        gate, comb = [], []
        for j in range(4):
            gate.append(twice_gates(j))
            yield
            if j < halves:
                comb.append(combine(j))
                yield
            if j == 1:
                gated_a = jnp.concatenate(gate[:2], axis=1) * ya
        yb = jnp.dot(jnp.concatenate(comb, axis=1), wpb_ref[...], preferred_element_type=F32)
        yield
        merged = (gated_a + jnp.concatenate(gate[2:], axis=1) * yb).astype(BF16)
        yield
        mix = jnp.dot(merged, wo_ref[...], preferred_element_type=F32)
        out_ref[rows, :] = x + mod_ref[2:3, :] * mix

    _interleave([slab(h) for h in range(MIX_ROW_SPLITS)])


def _mix(x, mod3, g1, attn, os_, lses, wg, bg, wpa, wpb, wo):
    b, s, d = x.shape
    tm = TM_MIX
    tok = lambda w: pl.BlockSpec((None, tm, w), lambda bb, i: (bb, i, 0))
    const2 = lambda shape: pl.BlockSpec(shape, lambda bb, i: (0, 0))
    halves = B_OUT_W // LANES
    sub = [pl.BlockSpec((None, halves, o.shape[2], tm // o.shape[2], LANES), lambda bb, i: (bb, 0, 0, i, 0))
           for o in os_]
    return pl.pallas_call(
        _mix_kernel,
        grid=(b, s // tm),
        in_specs=[tok(d),
                  pl.BlockSpec((None, 6, d), lambda bb, i: (bb, 0, 0)),
                  const2((1, d)),
                  pl.BlockSpec((None, tm // TQ, A_Q_W, TQ), lambda bb, i: (bb, i, 0, 0)),
                  *sub, *sub,
                  const2(wg.shape), const2(bg.shape), const2(wpa.shape), const2(wpb.shape),
                  const2(wo.shape)],
        out_specs=tok(d),
        out_shape=jax.ShapeDtypeStruct((b, s, d), F32),
        scratch_shapes=[pltpu.VMEM((2 * halves * len(B_GROUPS), tm, LANES), F32)],
        compiler_params=_params("parallel", "parallel"),
        name="mix",
    )(x, mod3, g1, attn, *os_, *lses, wg, bg, wpa, wpb, wo)


def _ffn_kernel(x_ref, mod_ref, g2_ref, win_ref, wout_ref, fg_ref, out_ref, *, final, bounds):
    d_ff = wout_ref.shape[0]
    th = x_ref.shape[0] // FFN_ROW_SPLITS

    def slab(h):
        rows = slice(h * th, (h + 1) * th)
        x = x_ref[rows, :]
        u = _modulated_norm(x, g2_ref[...], mod_ref[3:4, :], mod_ref[4:5, :])
        yield
        acc = jnp.zeros(x.shape, F32)
        for lo, hi in zip(bounds[:-1], bounds[1:]):
            hg = jnp.dot(u, win_ref[:, lo:hi], preferred_element_type=F32)
            yield
            hu = jnp.dot(u, win_ref[:, d_ff + lo:d_ff + hi], preferred_element_type=F32)
            yield
            act = ((hg * (jnp.tanh(hg) + 1.0)) * hu).astype(BF16)
            acc += jnp.dot(act, wout_ref[lo:hi, :], preferred_element_type=F32)
            yield
        y = x + mod_ref[5:6, :] * acc
        if final:
            ms = jnp.mean(y * y, axis=-1, keepdims=True)
            y = (y * lax.rsqrt(ms + EPS)) * fg_ref[...]
        out_ref[rows, :] = y

    _interleave([slab(h) for h in range(FFN_ROW_SPLITS)])


def _ffn(x, mod3, g2, win, wout, fg, final):
    b, s, d = x.shape
    tm = TM_FFN
    d_ff = wout.shape[0]
    tiles = pl.cdiv(d_ff, MXU_DIM)
    per_chunk = pl.cdiv(tiles, FFN_CHUNKS) * MXU_DIM
    bounds = tuple(min(d_ff, i * per_chunk) for i in range(FFN_CHUNKS + 1))
    tok = pl.BlockSpec((None, tm, d), lambda bb, i: (bb, i, 0))
    resident = lambda shape: pl.BlockSpec(shape, lambda bb, i: (0, 0), pipeline_mode=pl.Buffered(1))
    return pl.pallas_call(
        functools.partial(_ffn_kernel, final=final, bounds=bounds),
        grid=(b, s // tm),
        in_specs=[tok,
                  pl.BlockSpec((None, 6, d), lambda bb, i: (bb, 0, 0)),
                  pl.BlockSpec((1, d), lambda bb, i: (0, 0)),
                  resident(win.shape), resident(wout.shape),
                  pl.BlockSpec((1, d), lambda bb, i: (0, 0))],
        out_specs=tok,
        out_shape=jax.ShapeDtypeStruct((b, s, d), F32),
        compiler_params=_params("parallel", "parallel"),
        name="ffn",
    )(x, mod3, g2, win, wout, fg)


def _inv_freq(dim, theta):
    return theta ** (-jnp.arange(0, dim, 2, dtype=F32) / dim)


def _rope_tables(s):
    in_head = np.arange(LANES) % HEAD_DIM
    half = HEAD_DIM // 2
    quarter = half // 2
    inv_a = (AXIAL_THETA ** (-jnp.asarray(2 * (in_head % quarter), F32) / half))[None, :]
    ang_row = jnp.arange(s // GRID_W, dtype=jnp.int32).astype(F32)[:, None] * inv_a
    ang_col = jnp.arange(GRID_W, dtype=jnp.int32).astype(F32)[:, None] * inv_a
    by_row = (in_head < half)[None, None, :]
    first_a = ((in_head % half) < quarter)[None, None, :]

    small = lax.optimization_barrier(
        (jnp.cos(ang_row), jnp.cos(ang_col), jnp.sin(ang_row), jnp.sin(ang_col)))

    def grid(of_row, of_col):
        return jnp.where(by_row, of_row[:, None, :], of_col[None, :, :])

    a_cos = grid(small[0], small[1]).reshape(s, LANES)
    a_sin = grid(small[2], small[3])
    a_lo = jnp.where(first_a, -a_sin, 0.0).reshape(s, LANES)
    a_hi = jnp.where(first_a, 0.0, a_sin).reshape(s, LANES)
    t = jnp.arange(s, dtype=jnp.int32)
    ap = t.astype(F32)[:, None] * _inv_freq(PARTIAL_ROT_DIM, PARTIAL_THETA)[None, :]
    rest = HEAD_DIM - PARTIAL_ROT_DIM
    zp = jnp.zeros_like(ap)
    p_cos = jnp.concatenate([jnp.cos(ap), jnp.cos(ap), jnp.ones((s, rest), F32)], axis=1)
    p_lo = jnp.concatenate([-jnp.sin(ap), zp, jnp.zeros((s, rest), F32)], axis=1)
    p_hi = jnp.concatenate([zp, jnp.sin(ap), jnp.zeros((s, rest), F32)], axis=1)
    two = lambda a: jnp.tile(a, (1, LANES // HEAD_DIM))
    return a_cos, a_lo, a_hi, two(p_cos), two(p_lo), two(p_hi)


def kernel(x, c, w_ada, b_ada, norm1_g, w_qkv, q_norm_a, k_norm_a, w_proj_a, w_proj_b, w_gate,
           b_gate, w_o, norm2_g, w_ffn_in, w_ffn_out, final_norm_g):
    b, s, d = x.shape
    depth = w_ada.shape[0]
    assert s % TM_QKV == 0 and s % (B_GROUPS[-1][1] * WIN) == 0 and TK % TM_QKV == 0 and s % TK == 0
    tabs = _rope_tables(s)
    heads_per_vreg = LANES // HEAD_DIM
    lane_head = np.arange(MXU_DIM) // HEAD_DIM
    bd = jnp.asarray(lane_head[:, None] == lane_head[None, :], BF16)
    b_q_lo = A_Q_W + 2 * A_KV_W
    col_scale = jnp.ones((w_qkv.shape[-1],), F32).at[b_q_lo:b_q_lo + B_W].set(SCALE * LOG2E)
    c_pad = jnp.pad(c, ((0, 16 - b), (0, 0)))
    for l in range(depth):
        mod3 = _modulation(c_pad, w_ada, b_ada[l], l)[:b].reshape(b, 6, d)
        qg = jnp.tile(q_norm_a[l] * (SCALE * LOG2E), heads_per_vreg).reshape(1, LANES)
        kg = jnp.tile(k_norm_a[l], heads_per_vreg).reshape(1, LANES)
        g1 = norm1_g[l].reshape(1, d)
        wq = (w_qkv[l] * col_scale[None, :]).astype(BF16)
        qat, ka, vat, *groups = _qkv_project(x, mod3, g1, wq, qg, kg, bd, tabs)
        attn = _global_attention(qat, ka, vat)
        branch = [_banded_attention(qkv_g, g) for g, qkv_g in enumerate(groups)]
        x = _mix(x, mod3, g1, attn, [o for o, _ in branch], [e for _, e in branch],
                 (0.5 * w_gate[l]).astype(BF16), (0.5 * b_gate[l]).reshape(1, -1), w_proj_a[l].astype(BF16),
                 w_proj_b[l].astype(BF16), (0.5 * w_o[l]).astype(BF16))
        d_ff = w_ffn_out.shape[1]
        half_gate = jnp.where(jnp.arange(2 * d_ff) < d_ff, 0.5, 1.0).astype(F32)
        x = _ffn(x, mod3, norm2_g[l].reshape(1, d), (w_ffn_in[l] * half_gate[None, :]).astype(BF16),
                 w_ffn_out[l].astype(BF16), final_norm_g.reshape(1, d), final=(l == depth - 1))
    return x
```

```python
import functools

import jax
import jax.numpy as jnp
import numpy as np
from jax import lax
from jax.experimental import pallas as pl
from jax.experimental.pallas import tpu as pltpu

HEAD_DIM = 64
A_Q_HEADS = 8
A_KV_HEADS = 2
A_GROUP = A_Q_HEADS // A_KV_HEADS
B_GROUPS = ((128, 1), (512, 4), (2048, 16))
B_HEADS_PER_GROUP = 4
B_HEADS = B_HEADS_PER_GROUP * len(B_GROUPS)
A_Q_W = A_Q_HEADS * HEAD_DIM
A_KV_W = A_KV_HEADS * HEAD_DIM
B_W = B_HEADS * HEAD_DIM
B_OUT_W = B_HEADS_PER_GROUP * HEAD_DIM
GRID_W = 64
AXIAL_THETA = 10000.0
PARTIAL_THETA = 500000.0
PARTIAL_ROT_DIM = HEAD_DIM // 4
EPS = 1e-6
NEG_INF = -1e30
SCALE = HEAD_DIM ** -0.5
LOG2E = 1.4426950408889634
LN2 = 0.6931471805599453

LANES = 128
MXU_DIM = 256
VMEM_LIMIT = 56 * 1024 * 1024

F32 = jnp.float32
BF16 = jnp.bfloat16

TM_QKV = 1024
QKV_ROW_SPLITS = 2
TQ = 128
TK = 1024
TM_MIX = 1024
MIX_ROW_SPLITS = 1
TM_FFN = 1024
FFN_CHUNKS = 3
FFN_ROW_SPLITS = 4
BLK = 128
WIN = 256
SWA_MAX_STORE_STRIDE = 4
VT_ROWS = HEAD_DIM + 16
GQA_UNROLL = 2
GQA_TILES = 32


def _params(*sem):
    return pltpu.CompilerParams(dimension_semantics=sem, vmem_limit_bytes=VMEM_LIMIT)


def _mod_kernel(c_ref, w_ref, b_ref, o_ref):
    c = c_ref[...]
    cond = c * jax.nn.sigmoid(c)
    c_hi = cond.astype(BF16)
    c_lo = (cond - c_hi.astype(F32)).astype(BF16)
    w = w_ref[...]
    w_hi = w.astype(BF16)
    w_lo = (w - w_hi.astype(F32)).astype(BF16)
    acc = jnp.dot(c_hi, w_hi, preferred_element_type=F32)
    acc += jnp.dot(c_hi, w_lo, preferred_element_type=F32)
    acc += jnp.dot(c_lo, w_hi, preferred_element_type=F32)
    o_ref[...] = acc + b_ref[...]


def _modulation(c_pad, w_ada, b_ada, layer):
    rows, d = c_pad.shape
    n = w_ada.shape[2]
    tn = 1024
    return pl.pallas_call(
        _mod_kernel,
        grid=(n // tn,),
        in_specs=[pl.BlockSpec((rows, d), lambda j: (0, 0)),
                  pl.BlockSpec((None, d, tn), lambda j: (layer, 0, j)),
                  pl.BlockSpec((1, tn), lambda j: (0, j))],
        out_specs=pl.BlockSpec((rows, tn), lambda j: (0, j)),
        out_shape=jax.ShapeDtypeStruct((rows, n), F32),
        compiler_params=_params("parallel"),
        name="mod",
    )(c_pad, w_ada, b_ada.reshape(1, n))


def _modulated_norm(x, gain, shift, scale):
    ms = jnp.mean(x * x, axis=-1, keepdims=True)
    return ((x * lax.rsqrt(ms + EPS)) * (gain * (1.0 + scale)) + shift).astype(BF16)


def _interleave(programs):
    done = object()
    live = []
    for prog in programs:
        live.append(prog)
        live = [p for p in live if next(p, done) is not done]
    while live:
        live = [p for p in live if next(p, done) is not done]


def _rope(y, cos, sin_lo, sin_hi, shift):
    return (y * cos + pltpu.roll(y, LANES - shift, 1) * sin_lo
            + pltpu.roll(y, shift, 1) * sin_hi)


def _qkv_kernel(x_ref, mod_ref, g1_ref, w_ref, qg_ref, kg_ref, bd_ref,
                ac_ref, asl_ref, ash_ref, pc_ref, psl_ref, psh_ref,
                qat_ref, ka_ref, vat_ref, grp0_ref, grp1_ref, grp2_ref, slab_ref):
    tm = x_ref.shape[0]
    th = tm // QKV_ROW_SPLITS
    def slab(h):
        rows = slice(h * th, (h + 1) * th)
        u = _modulated_norm(x_ref[rows, :], g1_ref[...], mod_ref[0:1, :], mod_ref[1:2, :])
        yield

        def proj(lo, width):
            return jnp.dot(u, w_ref[:, lo:lo + width], preferred_element_type=F32)

        def head_norm_rope(wide, gain):
            out = []
            for lo in range(0, wide.shape[1], MXU_DIM):
                tile = wide[:, lo:lo + MXU_DIM]
                w = tile.shape[1]
                ss = jnp.dot((tile * tile).astype(BF16), bd_ref[:w, :w], preferred_element_type=F32)
                for c in range(w // LANES):
                    cols = slice(c * LANES, (c + 1) * LANES)
                    y = (tile[:, cols] * lax.rsqrt(ss[:, cols] * (1.0 / HEAD_DIM) + EPS)) * gain
                    out.append(_rope(y, ac_ref[rows, :], asl_ref[rows, :], ash_ref[rows, :], HEAD_DIM // 4))
            return out[0] if len(out) == 1 else jnp.concatenate(out, axis=1)

        def branch_b(g, between):
            out_ref, dil = (grp0_ref, grp1_ref, grp2_ref)[g], B_GROUPS[g][1]
            base = A_Q_W + 2 * A_KV_W
            halves = B_OUT_W // LANES
            for part in range(3):
                wide = proj(base + part * B_W + g * B_OUT_W, B_OUT_W)
                for c in range(halves):
                    chunk = wide[:, c * LANES:(c + 1) * LANES]
                    if part < 2:
                        chunk = _rope(chunk, pc_ref[rows, :], psl_ref[rows, :], psh_ref[rows, :],
                                      PARTIAL_ROT_DIM // 2)
                    slab = part * halves + c
                    if dil == 1:
                        out_ref[0, rows, slab * LANES:(slab + 1) * LANES] = chunk.astype(BF16)
                    else:
                        slab_ref[g - 1, slab, rows, :] = chunk
                between()
                yield
            if dil > 1:
                n = th // dil
                for r in range(dil):
                    for slab in range(3 * halves):
                        out_ref[r, h * n:(h + 1) * n, slab * LANES:(slab + 1) * LANES] = (
                            slab_ref[g - 1, slab, pl.ds(h * th + r, n, stride=dil), :].astype(BF16))
                yield

        raw_q = proj(0, A_Q_W)
        yield
        raw_kv = proj(A_Q_W, 2 * A_KV_W)
        yield
        q_tiles = []

        def store_q():
            qat = jnp.concatenate(q_tiles, axis=1).T.astype(BF16)
            for i in range(th // TQ):
                qat_ref[h * (th // TQ) + i] = qat[:, i * TQ:(i + 1) * TQ]

        def store_k():
            ka_ref[rows, :] = head_norm_rope(raw_kv[:, :A_KV_W], kg_ref[...]).astype(BF16)

        def store_v():
            vat = raw_kv[:, A_KV_W:].T.astype(BF16)
            pad_row = lax.broadcasted_iota(jnp.int32, (VT_ROWS - HEAD_DIM, th), 0)
            ones_row = jnp.where(pad_row == 0, 1.0, 0.0).astype(BF16)
            for hk in range(A_KV_HEADS):
                vat_ref[hk, 0:HEAD_DIM, rows] = vat[hk * HEAD_DIM:(hk + 1) * HEAD_DIM, :]
                vat_ref[hk, HEAD_DIM:VT_ROWS, rows] = ones_row

        steps = [functools.partial(lambda lo: q_tiles.append(head_norm_rope(raw_q[:, lo:lo + MXU_DIM], qg_ref[...])), lo)
                 for lo in range(0, A_Q_W, MXU_DIM)] + [store_q, store_k, store_v]

        def between():
            if steps:
                steps.pop(0)()

        for g in (2, 1, 0):
            yield from branch_b(g, between)
        while steps:
            between()
            yield

    _interleave([slab(h) for h in range(QKV_ROW_SPLITS)])


def _qkv_project(x, mod3, g1, w_qkv, qg, kg, bd, tabs):
    b, s, d = x.shape
    tm = TM_QKV
    ncols = w_qkv.shape[1]
    row_tab = pl.BlockSpec((tm, LANES), lambda i, bb: (i, 0))
    const2 = lambda shape: pl.BlockSpec(shape, lambda i, bb: (0, 0))
    tok = lambda w: pl.BlockSpec((None, tm, w), lambda i, bb: (bb, i, 0))
    out_shapes = (
        jax.ShapeDtypeStruct((b, s // TQ, A_Q_W, TQ), BF16),
        jax.ShapeDtypeStruct((b, s, A_KV_W), BF16),
        jax.ShapeDtypeStruct((b, s // TK, A_KV_HEADS, VT_ROWS, TK), BF16),
    ) + tuple(jax.ShapeDtypeStruct((b, dil, s // dil, 3 * B_OUT_W), BF16) for _, dil in B_GROUPS)
    out_specs = (
        pl.BlockSpec((None, tm // TQ, A_Q_W, TQ), lambda i, bb: (bb, i, 0, 0)),
        tok(A_KV_W),
        pl.BlockSpec((None, None, A_KV_HEADS, VT_ROWS, tm),
                     lambda i, bb: (bb, i // (TK // tm), 0, 0, i % (TK // tm))),
    ) + tuple(pl.BlockSpec((None, dil, tm // dil, 3 * B_OUT_W), lambda i, bb: (bb, 0, i, 0))
              for _, dil in B_GROUPS)
    return pl.pallas_call(
        _qkv_kernel,
        grid=(s // tm, b),
        in_specs=[tok(d),
                  pl.BlockSpec((None, 6, d), lambda i, bb: (bb, 0, 0)),
                  const2((1, d)),
                  const2((d, ncols)),
                  const2((1, LANES)), const2((1, LANES)), const2((MXU_DIM, MXU_DIM)),
                  row_tab, row_tab, row_tab, row_tab, row_tab, row_tab],
        out_specs=out_specs,
        out_shape=out_shapes,
        scratch_shapes=[pltpu.VMEM((len(B_GROUPS) - 1, 3 * B_OUT_W // LANES, tm, LANES), F32)],
        compiler_params=_params("parallel", "parallel"),
        name="qkv",
    )(x, mod3, g1, w_qkv, qg, kg, bd, *tabs)


def _gqa_kernel(qt_ref, k_ref, vt_ref, o_ref, qx_ref, s0_ref, s1_ref):
    nk = vt_ref.shape[0]
    ncol = A_GROUP * TQ
    n_units = GQA_TILES * A_KV_HEADS

    qx_ref[...] = jnp.zeros(qx_ref.shape, BF16)
    for t in range(GQA_TILES):
        for hk in range(A_KV_HEADS):
            for g in range(A_GROUP):
                h = hk * A_GROUP + g
                qx_ref[t * A_KV_HEADS + hk, hk * HEAD_DIM:(hk + 1) * HEAD_DIM, g * TQ:(g + 1) * TQ] = (
                    qt_ref[t, h * HEAD_DIM:(h + 1) * HEAD_DIM, :])

    s_refs = (s0_ref, s1_ref)

    def phase(score_unit, value_unit, parity, m_prev):
        qx = None if score_unit is None else qx_ref[score_unit]

        def body(j, carry):
            mrun, acc = carry
            rows = pl.ds(pl.multiple_of(j * TK, TK), TK)
            if score_unit is not None:
                st = jnp.dot(k_ref[rows, :], qx, preferred_element_type=F32)
                s_refs[parity][rows, :] = st
                mrun = jnp.maximum(mrun, jnp.max(st.reshape(TK // 8, 8, ncol), axis=0))
            if value_unit is not None:
                p = jnp.exp2((s_refs[1 - parity][rows, :] - m_prev).astype(BF16))
                acc = acc + jnp.dot(vt_ref[j, 1 - parity], p, preferred_element_type=F32)
            return mrun, acc

        init = (jnp.full((8, ncol), NEG_INF, F32), jnp.zeros((VT_ROWS, ncol), F32))
        mrun, acc = lax.fori_loop(0, nk, body, init, unroll=GQA_UNROLL)
        if value_unit is not None:
            out_t = acc[0:HEAD_DIM, :] / acc[HEAD_DIM:HEAD_DIM + 1, :]
            stacked = jnp.concatenate([out_t[:, g * TQ:(g + 1) * TQ] for g in range(A_GROUP)], axis=0)
            group_w = A_GROUP * HEAD_DIM
            o_ref[value_unit // A_KV_HEADS, (1 - parity) * group_w:(2 - parity) * group_w, :] = (
                stacked.astype(BF16))
        return jnp.max(mrun, axis=0, keepdims=True)

    assert A_KV_HEADS == 2 and n_units % 2 == 0
    m = phase(0, None, 0, None)

    def unit_pair(i, m_prev):
        m_odd = phase(2 * i + 1, 2 * i, 1, m_prev)
        return phase(2 * i + 2, 2 * i + 1, 0, m_odd)

    m = lax.fori_loop(0, n_units // 2 - 1, unit_pair, m)
    m = phase(n_units - 1, n_units - 2, 1, m)
    phase(None, n_units - 1, 0, m)


def _global_attention(qat, ka, vat):
    b, nq = qat.shape[0], qat.shape[1]
    s = ka.shape[1]
    nk = vat.shape[1]
    ncol = A_GROUP * TQ
    return pl.pallas_call(
        _gqa_kernel,
        grid=(b, nq // GQA_TILES),
        in_specs=[pl.BlockSpec((None, GQA_TILES, A_Q_W, TQ), lambda bb, i: (bb, i, 0, 0)),
                  pl.BlockSpec((None, s, A_KV_W), lambda bb, i: (bb, 0, 0)),
                  pl.BlockSpec((None, nk, A_KV_HEADS, VT_ROWS, TK), lambda bb, i: (bb, 0, 0, 0, 0))],
        out_specs=pl.BlockSpec((None, GQA_TILES, A_Q_W, TQ), lambda bb, i: (bb, i, 0, 0)),
        out_shape=jax.ShapeDtypeStruct((b, s // TQ, A_Q_W, TQ), BF16),
        scratch_shapes=[pltpu.VMEM((GQA_TILES * A_KV_HEADS, A_KV_W, ncol), BF16),
                        pltpu.VMEM((s, ncol), F32), pltpu.VMEM((s, ncol), F32)],
        compiler_params=_params("parallel", "parallel"),
        name="gqa",
    )(qat, ka, vat)


def _swa_kernel(qkv_ref, o_ref, lse_ref, sc_ref, p_ref, top_ref, bias_ref, *, radius):
    dil, seq = qkv_ref.shape[0], qkv_ref.shape[1]
    nblk = seq // BLK
    halves = B_OUT_W // LANES
    lane = lax.broadcasted_iota(jnp.int32, (BLK, LANES), 1)
    first_head = lane < HEAD_DIM
    head_mask = [jnp.where(first_head, 1.0, 0.0).astype(BF16), jnp.where(first_head, 0.0, 1.0).astype(BF16)]
    ones_win = jnp.ones((WIN, LANES), BF16)

    rel = (lax.broadcasted_iota(jnp.int32, (BLK, WIN), 0)
           - lax.broadcasted_iota(jnp.int32, (BLK, WIN), 1))
    for n in range(bias_ref.shape[0]):
        bias_ref[n] = jnp.where(jnp.abs(rel + n * radius) <= radius, 0.0, NEG_INF)

    def geometry(i):
        r = i // nblk
        i0 = pl.multiple_of((i % nblk) * BLK, BLK)
        ws = pl.multiple_of(jnp.clip(i0 - radius, 0, seq - WIN), radius)
        return r, i0, ws

    def scores(i, slot):
        r, i0, ws = geometry(i)
        bias = bias_ref[(i0 - ws) // radius]
        for c in range(halves):
            q = qkv_ref[r, pl.ds(i0, BLK), c * LANES:(c + 1) * LANES]
            k = qkv_ref[r, pl.ds(ws, WIN), B_OUT_W + c * LANES:B_OUT_W + (c + 1) * LANES]
            for hh in range(2):
                sc_ref[slot, 2 * c + hh] = lax.dot_general(
                    q * head_mask[hh], k, (((1,), (1,)), ((), ())), preferred_element_type=F32) + bias

    def probabilities(slot):
        for c in range(halves):
            tops = []
            for hh in range(2):
                sc = sc_ref[slot, 2 * c + hh]
                m = jnp.max(sc, axis=-1, keepdims=True)
                p_ref[slot, 2 * c + hh] = jnp.exp2((sc - m).astype(BF16))
                tops.append(m)
            top_ref[slot, c] = jnp.where(first_head, tops[0], tops[1])

    def outputs(i, slot):
        r, i0, ws = geometry(i)
        for c in range(halves):
            v = qkv_ref[r, pl.ds(ws, WIN), 2 * B_OUT_W + c * LANES:2 * B_OUT_W + (c + 1) * LANES]
            v_ones = jnp.concatenate([v, ones_win], axis=1)
            outs = [jnp.dot(p_ref[slot, 2 * c + hh], v_ones, preferred_element_type=F32) for hh in range(2)]
            num = jnp.where(first_head, outs[0][:, :LANES], outs[1][:, :LANES])
            den = jnp.where(first_head, outs[0][:, LANES:], outs[1][:, LANES:])
            classes = o_ref.shape[1]
            step = dil // classes
            rows = pl.ds(i0 * step + r // classes, BLK, stride=step) if step > 1 else pl.ds(i0, BLK)
            o_ref[c, r % classes, rows, :] = num / den
            lse_ref[c, r % classes, rows, :] = (top_ref[slot, c] + jnp.log2(den)) * LN2

    total = dil * nblk
    scores(0, 0)
    probabilities(0)
    scores(1, 1)

    def body(i, _):
        outputs(i - 2, i % 2)
        probabilities((i - 1) % 2)
        scores(i, i % 2)
        return 0

    lax.fori_loop(2, total, body, 0, unroll=2)
    outputs(total - 2, total % 2)
    probabilities((total - 1) % 2)
    outputs(total - 1, (total - 1) % 2)


def _banded_attention(qkv, group):
    window, dil = B_GROUPS[group]
    b, _, seq, width = qkv.shape
    radius = window // 2 // dil
    halves = B_OUT_W // LANES
    classes = max(1, dil // SWA_MAX_STORE_STRIDE)
    out_spec = pl.BlockSpec((None, halves, classes, dil * seq // classes, LANES), lambda bb: (bb, 0, 0, 0, 0))
    out_shape = jax.ShapeDtypeStruct((b, halves, classes, dil * seq // classes, LANES), F32)
    return pl.pallas_call(
        functools.partial(_swa_kernel, radius=radius),
        grid=(b,),
        in_specs=[pl.BlockSpec((None, dil, seq, width), lambda bb: (bb, 0, 0, 0))],
        out_specs=(out_spec, out_spec),
        out_shape=(out_shape, out_shape),
        scratch_shapes=[pltpu.VMEM((2, B_HEADS_PER_GROUP, BLK, WIN), F32),
                        pltpu.VMEM((2, B_HEADS_PER_GROUP, BLK, WIN), BF16),
                        pltpu.VMEM((2, B_OUT_W // LANES, BLK, LANES), F32),
                        pltpu.VMEM(((WIN - BLK) // radius + 1, BLK, WIN), F32)],
        compiler_params=_params("parallel"),
        name=f"swa{group}",
    )(qkv)


def _mix_kernel(x_ref, mod_ref, g1_ref, attn_ref, o0_ref, o1_ref, o2_ref, l0_ref, l1_ref, l2_ref,
                wg_ref, bg_ref, wpa_ref, wpb_ref, wo_ref, out_ref, nat_ref):
    d = x_ref.shape[-1]
    halves = B_OUT_W // LANES
    th = x_ref.shape[0] // MIX_ROW_SPLITS

    def slab(h):
        rows = slice(h * th, (h + 1) * th)
        x = x_ref[rows, :]
        tiles = th // TQ
        pair = MXU_DIM // TQ
        ya = jnp.concatenate(
            [lax.dot_general(jnp.concatenate([attn_ref[h * tiles + i + k] for k in range(pair)], axis=1),
                             wpa_ref[...], (((0,), (0,)), ((), ())), preferred_element_type=F32)
             for i in range(0, tiles, pair)], axis=0)
        yield
        u = _modulated_norm(x, g1_ref[...], mod_ref[0:1, :], mod_ref[1:2, :])

        def token_order(src, c, slot):
            classes = src.shape[1]
            if classes == 1:
                return src[c, 0, rows, :]
            n = th // classes
            for k in range(classes):
                nat_ref[slot, pl.ds(h * th + k, n, stride=classes), :] = src[c, k, h * n:(h + 1) * n, :]
            return nat_ref[slot, rows, :]

        def combine(c):
            groups = len(B_GROUPS)
            lses = [token_order(l_ref, c, (2 * c) * groups + g) for g, l_ref in enumerate((l0_ref, l1_ref, l2_ref))]
            outs = [token_order(o_ref, c, (2 * c + 1) * groups + g)
                    for g, o_ref in enumerate((o0_ref, o1_ref, o2_ref))]
            top = jnp.maximum(jnp.maximum(lses[0], lses[1]), lses[2])
            es = [jnp.exp(l - top) for l in lses]
            den = es[0] + es[1] + es[2]
            return (sum(e * o for e, o in zip(es, outs)) / den).astype(BF16)

        def twice_gates(j, parts=4):
            w = 2 * d // parts
            cols = slice(j * w, (j + 1) * w)
            return jnp.tanh(jnp.dot(u, wg_ref[:, cols], preferred_element_type=F32) + bg_ref[:, cols]) + 1.0

        gate, comb = [], []
        for j in range(4):
            gate.append(twice_gates(j))
            yield
            if j < halves:
                comb.append(combine(j))
                yield
            if j == 1:
                gated_a = jnp.concatenate(gate[:2], axis=1) * ya
        yb = jnp.dot(jnp.concatenate(comb, axis=1), wpb_ref[...], preferred_element_type=F32)
        yield
        merged = (gated_a + jnp.concatenate(gate[2:], axis=1) * yb).astype(BF16)
        yield
        mix = jnp.dot(merged, wo_ref[...], preferred_element_type=F32)
        out_ref[rows, :] = x + mod_ref[2:3, :] * mix

    _interleave([slab(h) for h in range(MIX_ROW_SPLITS)])


def _mix(x, mod3, g1, attn, os_, lses, wg, bg, wpa, wpb, wo):
    b, s, d = x.shape
    tm = TM_MIX
    tok = lambda w: pl.BlockSpec((None, tm, w), lambda bb, i: (bb, i, 0))
    const2 = lambda shape: pl.BlockSpec(shape, lambda bb, i: (0, 0))
    halves = B_OUT_W // LANES
    sub = [pl.BlockSpec((None, halves, o.shape[2], tm // o.shape[2], LANES), lambda bb, i: (bb, 0, 0, i, 0))
           for o in os_]
    return pl.pallas_call(
        _mix_kernel,
        grid=(b, s // tm),
        in_specs=[tok(d),
                  pl.BlockSpec((None, 6, d), lambda bb, i: (bb, 0, 0)),
                  const2((1, d)),
                  pl.BlockSpec((None, tm // TQ, A_Q_W, TQ), lambda bb, i: (bb, i, 0, 0)),
                  *sub, *sub,
                  const2(wg.shape), const2(bg.shape), const2(wpa.shape), const2(wpb.shape),
                  const2(wo.shape)],
        out_specs=tok(d),
        out_shape=jax.ShapeDtypeStruct((b, s, d), F32),
        scratch_shapes=[pltpu.VMEM((2 * halves * len(B_GROUPS), tm, LANES), F32)],
        compiler_params=_params("parallel", "parallel"),
        name="mix",
    )(x, mod3, g1, attn, *os_, *lses, wg, bg, wpa, wpb, wo)


def _ffn_kernel(x_ref, mod_ref, g2_ref, win_ref, wout_ref, fg_ref, out_ref, *, final, bounds):
    d_ff = wout_ref.shape[0]
    th = x_ref.shape[0] // FFN_ROW_SPLITS

    def slab(h):
        rows = slice(h * th, (h + 1) * th)
        x = x_ref[rows, :]
        u = _modulated_norm(x, g2_ref[...], mod_ref[3:4, :], mod_ref[4:5, :])
        yield
        acc = jnp.zeros(x.shape, F32)
        for lo, hi in zip(bounds[:-1], bounds[1:]):
            hg = jnp.dot(u, win_ref[:, lo:hi], preferred_element_type=F32)
            yield
            hu = jnp.dot(u, win_ref[:, d_ff + lo:d_ff + hi], preferred_element_type=F32)
            yield
            act = ((hg * (jnp.tanh(hg) + 1.0)) * hu).astype(BF16)
            acc += jnp.dot(act, wout_ref[lo:hi, :], preferred_element_type=F32)
            yield
        y = x + mod_ref[5:6, :] * acc
        if final:
            ms = jnp.mean(y * y, axis=-1, keepdims=True)
            y = (y * lax.rsqrt(ms + EPS)) * fg_ref[...]
        out_ref[rows, :] = y

    _interleave([slab(h) for h in range(FFN_ROW_SPLITS)])


def _ffn(x, mod3, g2, win, wout, fg, final):
    b, s, d = x.shape
    tm = TM_FFN
    d_ff = wout.shape[0]
    tiles = pl.cdiv(d_ff, MXU_DIM)
    per_chunk = pl.cdiv(tiles, FFN_CHUNKS) * MXU_DIM
    bounds = tuple(min(d_ff, i * per_chunk) for i in range(FFN_CHUNKS + 1))
    tok = pl.BlockSpec((None, tm, d), lambda bb, i: (bb, i, 0))
    resident = lambda shape: pl.BlockSpec(shape, lambda bb, i: (0, 0), pipeline_mode=pl.Buffered(1))
    return pl.pallas_call(
        functools.partial(_ffn_kernel, final=final, bounds=bounds),
        grid=(b, s // tm),
        in_specs=[tok,
                  pl.BlockSpec((None, 6, d), lambda bb, i: (bb, 0, 0)),
                  pl.BlockSpec((1, d), lambda bb, i: (0, 0)),
                  resident(win.shape), resident(wout.shape),
                  pl.BlockSpec((1, d), lambda bb, i: (0, 0))],
        out_specs=tok,
        out_shape=jax.ShapeDtypeStruct((b, s, d), F32),
        compiler_params=_params("parallel", "parallel"),
        name="ffn",
    )(x, mod3, g2, win, wout, fg)


def _inv_freq(dim, theta):
    return theta ** (-jnp.arange(0, dim, 2, dtype=F32) / dim)


def _rope_tables(s):
    in_head = np.arange(LANES) % HEAD_DIM
    half = HEAD_DIM // 2
    quarter = half // 2
    inv_a = (AXIAL_THETA ** (-jnp.asarray(2 * (in_head % quarter), F32) / half))[None, :]
    ang_row = jnp.arange(s // GRID_W, dtype=jnp.int32).astype(F32)[:, None] * inv_a
    ang_col = jnp.arange(GRID_W, dtype=jnp.int32).astype(F32)[:, None] * inv_a
    by_row = (in_head < half)[None, None, :]
    first_a = ((in_head % half) < quarter)[None, None, :]

    small = lax.optimization_barrier(
        (jnp.cos(ang_row), jnp.cos(ang_col), jnp.sin(ang_row), jnp.sin(ang_col)))

    def grid(of_row, of_col):
        return jnp.where(by_row, of_row[:, None, :], of_col[None, :, :])

    a_cos = grid(small[0], small[1]).reshape(s, LANES)
    a_sin = grid(small[2], small[3])
    a_lo = jnp.where(first_a, -a_sin, 0.0).reshape(s, LANES)
    a_hi = jnp.where(first_a, 0.0, a_sin).reshape(s, LANES)
    t = jnp.arange(s, dtype=jnp.int32)
    ap = t.astype(F32)[:, None] * _inv_freq(PARTIAL_ROT_DIM, PARTIAL_THETA)[None, :]
    rest = HEAD_DIM - PARTIAL_ROT_DIM
    zp = jnp.zeros_like(ap)
    p_cos = jnp.concatenate([jnp.cos(ap), jnp.cos(ap), jnp.ones((s, rest), F32)], axis=1)
    p_lo = jnp.concatenate([-jnp.sin(ap), zp, jnp.zeros((s, rest), F32)], axis=1)
    p_hi = jnp.concatenate([zp, jnp.sin(ap), jnp.zeros((s, rest), F32)], axis=1)
    two = lambda a: jnp.tile(a, (1, LANES // HEAD_DIM))
    return a_cos, a_lo, a_hi, two(p_cos), two(p_lo), two(p_hi)


def kernel(x, c, w_ada, b_ada, norm1_g, w_qkv, q_norm_a, k_norm_a, w_proj_a, w_proj_b, w_gate,
           b_gate, w_o, norm2_g, w_ffn_in, w_ffn_out, final_norm_g):
    b, s, d = x.shape
    depth = w_ada.shape[0]
    assert s % TM_QKV == 0 and s % (B_GROUPS[-1][1] * WIN) == 0 and TK % TM_QKV == 0 and s % TK == 0
    tabs = _rope_tables(s)
    heads_per_vreg = LANES // HEAD_DIM
    lane_head = np.arange(MXU_DIM) // HEAD_DIM
    bd = jnp.asarray(lane_head[:, None] == lane_head[None, :], BF16)
    b_q_lo = A_Q_W + 2 * A_KV_W
    col_scale = jnp.ones((w_qkv.shape[-1],), F32).at[b_q_lo:b_q_lo + B_W].set(SCALE * LOG2E)
    c_pad = jnp.pad(c, ((0, 16 - b), (0, 0)))
    for l in range(depth):
        mod3 = _modulation(c_pad, w_ada, b_ada[l], l)[:b].reshape(b, 6, d)
        qg = jnp.tile(q_norm_a[l] * (SCALE * LOG2E), heads_per_vreg).reshape(1, LANES)
        kg = jnp.tile(k_norm_a[l], heads_per_vreg).reshape(1, LANES)
        g1 = norm1_g[l].reshape(1, d)
        wq = (w_qkv[l] * col_scale[None, :]).astype(BF16)
        qat, ka, vat, *groups = _qkv_project(x, mod3, g1, wq, qg, kg, bd, tabs)
        attn = _global_attention(qat, ka, vat)
        branch = [_banded_attention(qkv_g, g) for g, qkv_g in enumerate(groups)]
        x = _mix(x, mod3, g1, attn, [o for o, _ in branch], [e for _, e in branch],
                 (0.5 * w_gate[l]).astype(BF16), (0.5 * b_gate[l]).reshape(1, -1), w_proj_a[l].astype(BF16),
                 w_proj_b[l].astype(BF16), (0.5 * w_o[l]).astype(BF16))
        d_ff = w_ffn_out.shape[1]
        half_gate = jnp.where(jnp.arange(2 * d_ff) < d_ff, 0.5, 1.0).astype(F32)
        x = _ffn(x, mod3, norm2_g[l].reshape(1, d), (w_ffn_in[l] * half_gate[None, :]).astype(BF16),
                 w_ffn_out[l].astype(BF16), final_norm_g.reshape(1, d), final=(l == depth - 1))
    return x
```
